```python
import jax, jax.numpy as jnp
from jax import lax
import numpy as np

D_MODEL = 2048
BATCH = 4
SEQ = 4096
DEPTH = 4
DEC_BATCH = 8
DEC_SEQ = 32
PAST_LEN = 1024

CHUNK = 64
LEFT_CHUNKS = 8
BAND_LEFT = LEFT_CHUNKS * CHUNK
BAND_KEYS = BAND_LEFT + CHUNK
N_HEADS = 16
HEAD_DIM = D_MODEL // N_HEADS
REL_CLIP = 2 * CHUNK
N_REL = 2 * REL_CLIP + 1
POOL_WINDOWS = (2, 4, 8, 16)
N_POOL_GROUPS = len(POOL_WINDOWS)
POOL_GROUP = D_MODEL // N_POOL_GROUPS
POOL_HIST = max(POOL_WINDOWS) - 1
N_MEM = 256
MEM_HEADS = 4
MEM_HEAD_DIM = D_MODEL // MEM_HEADS
D_FF = -(-8 * D_MODEL // (3 * 256)) * 256
N_ATTN_LAYERS = (DEPTH + 1) // 2
N_POOL_LAYERS = DEPTH // 2
ALPHA = (2.0 * DEPTH) ** 0.25
BETA = (8.0 * DEPTH) ** -0.25
LN_EPS = 1e-5
NEG_BIG = -1e30

kernel_name = 'streaming_band_attn_pool_hybrid_step'


def post_ln(x, g, b):
    xf = x.astype(jnp.float32)
    mu = jnp.mean(xf, axis=-1, keepdims=True)
    xc = xf - mu
    var = jnp.mean(xc * xc, axis=-1, keepdims=True)
    return (xc * lax.rsqrt(var + LN_EPS) * g.astype(jnp.float32) + b.astype(jnp.float32)).astype(x.dtype)


def split_qkv(x, w_qkv):
    B, T, _ = x.shape
    qkv = (x @ w_qkv).reshape(B, T, 3, N_HEADS, HEAD_DIM)
    return qkv[:, :, 0], qkv[:, :, 1], qkv[:, :, 2]


def rel_bias_matrix(table, qpos, kpos):
    idx = jnp.clip(qpos[:, None] - kpos[None, :], -REL_CLIP, REL_CLIP) + REL_CLIP
    return table[:, idx].astype(jnp.float32)


def band_attn_prompt(x, w_qkv, w_o, table):
    B, S, _ = x.shape
    q, k, v = split_qkv(x, w_qkv)
    pad = ((0, 0), (BAND_LEFT, 0), (0, 0), (0, 0))
    kp = jnp.pad(k, pad)
    vp = jnp.pad(v, pad)
    n_chunks = S // CHUNK
    band_idx = jnp.arange(BAND_KEYS)
    bias = rel_bias_matrix(table, jnp.arange(CHUNK), band_idx - BAND_LEFT)
    scale = HEAD_DIM ** -0.5

    def one_chunk(c):
        start = c * CHUNK
        qc = lax.dynamic_slice_in_dim(q, start, CHUNK, axis=1)
        kc = lax.dynamic_slice_in_dim(kp, start, BAND_KEYS, axis=1)
        vc = lax.dynamic_slice_in_dim(vp, start, BAND_KEYS, axis=1)
        valid = (start - BAND_LEFT + band_idx) >= 0
        s = jnp.einsum('bqhd,bkhd->bhqk', qc, kc).astype(jnp.float32) * scale + bias[None]
        s = jnp.where(valid[None, None, None, :], s, NEG_BIG)
        p = jax.nn.softmax(s, axis=-1).astype(vc.dtype)
        return jnp.einsum('bhqk,bkhd->bqhd', p, vc)

    o = lax.map(one_chunk, jnp.arange(n_chunks))
    o = jnp.transpose(o, (1, 0, 2, 3, 4)).reshape(B, S, D_MODEL)
    return o @ w_o, k, v


def band_attn_sample(x, ck, cv, w_qkv, w_o, table):
    B, T, _ = x.shape
    P = ck.shape[1]
    q, k, v = split_qkv(x, w_qkv)
    keys = jnp.concatenate([ck, k], axis=1)
    vals = jnp.concatenate([cv, v], axis=1)
    kpos = jnp.concatenate([jnp.arange(P) - P, jnp.arange(T)])
    bias = rel_bias_matrix(table, jnp.arange(T), kpos)
    s = jnp.einsum('bqhd,bkhd->bhqk', q, keys).astype(jnp.float32) * (HEAD_DIM ** -0.5) + bias[None]
    p = jax.nn.softmax(s, axis=-1).astype(vals.dtype)
    o = jnp.einsum('bhqk,bkhd->bqhd', p, vals).reshape(B, T, D_MODEL)
    return o @ w_o, k, v


def pool_mix(x_ext, n_hist, w_pool, scale):
    B, L, _ = x_ext.shape
    T = L - n_hist
    xf = x_ext.astype(jnp.float32)
    cs = jnp.concatenate([jnp.zeros((B, 1, D_MODEL), jnp.float32), jnp.cumsum(xf, axis=1)], axis=1)
    hi = n_hist + jnp.arange(T) + 1
    outs = []
    for g, w in enumerate(POOL_WINDOWS):
        sl = slice(g * POOL_GROUP, (g + 1) * POOL_GROUP)
        lo = jnp.maximum(hi - w, 0)
        cs_g = cs[:, :, sl]
        cnt = (hi - lo).astype(jnp.float32)
        mean = (cs_g[:, hi] - cs_g[:, lo]) / cnt[None, :, None]
        outs.append(mean - xf[:, n_hist:, sl])
    u = jnp.stack(outs, axis=2).astype(x_ext.dtype)
    y = jnp.einsum('btgc,gcd->btgd', u, w_pool).reshape(B, T, D_MODEL)
    return y * scale


def mem_kv(mem, w_kv):
    B, M, _ = mem.shape
    kv = (mem @ w_kv).reshape(B, M, 2, MEM_HEADS, MEM_HEAD_DIM)
    return kv[:, :, 0], kv[:, :, 1]


def mem_cross_attn(x, mk, mv, w_q, w_o):
    B, T, _ = x.shape
    q = (x @ w_q).reshape(B, T, MEM_HEADS, MEM_HEAD_DIM)
    s = jnp.einsum('bqhd,bkhd->bhqk', q, mk).astype(jnp.float32) * (MEM_HEAD_DIM ** -0.5)
    p = jax.nn.softmax(s, axis=-1).astype(mv.dtype)
    o = jnp.einsum('bhqk,bkhd->bqhd', p, mv).reshape(B, T, D_MODEL)
    return o @ w_o


def swiglu(x, w_in, w_out):
    h = x @ w_in
    gate, up = h[..., :D_FF], h[..., D_FF:]
    return (jax.nn.silu(gate) * up) @ w_out


def setup_inputs(seed: int = 0) -> dict:
    key = jax.random.key(seed)
    ks = jax.random.split(key, 24)
    f32 = jnp.float32

    def nrm(k, shape, s):
        return jax.random.normal(k, shape, f32) * s

    D = D_MODEL
    past_band = min(BAND_LEFT, PAST_LEN)
    w_qk = nrm(ks[8], (N_ATTN_LAYERS, D, 2 * D), D ** -0.5)
    w_v = nrm(ks[9], (N_ATTN_LAYERS, D, D), D ** -0.5 * BETA)
    w_mk = nrm(ks[15], (DEPTH, D, D), D ** -0.5)
    w_mv = nrm(ks[16], (DEPTH, D, D), D ** -0.5 * BETA)
    return {
        'x_prompt': nrm(ks[0], (BATCH, SEQ, D), 1.0),
        'x_sample': nrm(ks[1], (DEC_BATCH, DEC_SEQ, D), 1.0),
        'cache_attn_k': nrm(ks[2], (N_ATTN_LAYERS, DEC_BATCH, past_band, N_HEADS, HEAD_DIM), 1.0),
        'cache_attn_v': nrm(ks[3], (N_ATTN_LAYERS, DEC_BATCH, past_band, N_HEADS, HEAD_DIM), 1.0),
        'state_pool': nrm(ks[4], (N_POOL_LAYERS, DEC_BATCH, POOL_HIST, D), 1.0),
        'cache_mem_k': nrm(ks[5], (DEPTH, DEC_BATCH, N_MEM, MEM_HEADS, MEM_HEAD_DIM), 1.0),
        'cache_mem_v': nrm(ks[6], (DEPTH, DEC_BATCH, N_MEM, MEM_HEADS, MEM_HEAD_DIM), 1.0),
        'mem_prompt': nrm(ks[7], (BATCH, N_MEM, D), 1.0),
        'w_qkv': jnp.concatenate([w_qk, w_v], axis=-1),
        'w_attn_o': nrm(ks[10], (N_ATTN_LAYERS, D, D), D ** -0.5 * BETA),
        'rel_bias': nrm(ks[11], (N_ATTN_LAYERS, N_HEADS, N_REL), 0.5),
        'w_pool': nrm(ks[12], (N_POOL_LAYERS, N_POOL_GROUPS, POOL_GROUP, POOL_GROUP), POOL_GROUP ** -0.5 * BETA),
        'pool_scale': 1.0 + nrm(ks[13], (N_POOL_LAYERS, D), 0.02),
        'w_mem_q': nrm(ks[14], (DEPTH, D, D), D ** -0.5),
        'w_mem_kv': jnp.concatenate([w_mk, w_mv], axis=-1),
        'w_mem_o': nrm(ks[17], (DEPTH, D, D), D ** -0.5 * BETA),
        'w_ffn_in': nrm(ks[18], (DEPTH, D, 2 * D_FF), D ** -0.5 * BETA),
        'w_ffn_out': nrm(ks[19], (DEPTH, D_FF, D), D_FF ** -0.5 * BETA),
        'ln_g': 1.0 + nrm(ks[20], (DEPTH, 3, D), 0.02),
        'ln_b': nrm(ks[21], (DEPTH, 3, D), 0.02),
    }


def reference(x_prompt, x_sample, cache_attn_k, cache_attn_v, state_pool, cache_mem_k, cache_mem_v,
              mem_prompt, w_qkv, w_attn_o, rel_bias, w_pool, pool_scale, w_mem_q, w_mem_kv, w_mem_o,
              w_ffn_in, w_ffn_out, ln_g, ln_b):
    S = x_prompt.shape[1]
    keep_prompt = min(BAND_LEFT, S)
    yp, ys = x_prompt, x_sample
    ak_p, av_p, pool_p, mk_p, mv_p = [], [], [], [], []
    ak_s, av_s, pool_s = [], [], []
    for i in range(DEPTH):
        if i % 2 == 0:
            a = i // 2
            mp, kp_, vp_ = band_attn_prompt(yp, w_qkv[a], w_attn_o[a], rel_bias[a])
            ms, ks_, vs_ = band_attn_sample(ys, cache_attn_k[a], cache_attn_v[a], w_qkv[a], w_attn_o[a], rel_bias[a])
            ak_p.append(kp_[:, S - keep_prompt:])
            av_p.append(vp_[:, S - keep_prompt:])
            ak_s.append(ks_)
            av_s.append(vs_)
        else:
            p = i // 2
            mp = pool_mix(yp, 0, w_pool[p], pool_scale[p])
            ext = jnp.concatenate([state_pool[p], ys], axis=1)
            ms = pool_mix(ext, POOL_HIST, w_pool[p], pool_scale[p])
            pool_p.append(yp[:, S - POOL_HIST:])
            pool_s.append(ext[:, ext.shape[1] - POOL_HIST:])
        yp = post_ln(ALPHA * yp + mp, ln_g[i, 0], ln_b[i, 0])
        ys = post_ln(ALPHA * ys + ms, ln_g[i, 0], ln_b[i, 0])
        mk, mv = mem_kv(mem_prompt, w_mem_kv[i])
        mk_p.append(mk)
        mv_p.append(mv)
        yp = post_ln(ALPHA * yp + mem_cross_attn(yp, mk, mv, w_mem_q[i], w_mem_o[i]), ln_g[i, 1], ln_b[i, 1])
        ys = post_ln(ALPHA * ys + mem_cross_attn(ys, cache_mem_k[i], cache_mem_v[i], w_mem_q[i], w_mem_o[i]), ln_g[i, 1], ln_b[i, 1])
        yp = post_ln(ALPHA * yp + swiglu(yp, w_ffn_in[i], w_ffn_out[i]), ln_g[i, 2], ln_b[i, 2])
        ys = post_ln(ALPHA * ys + swiglu(ys, w_ffn_in[i], w_ffn_out[i]), ln_g[i, 2], ln_b[i, 2])
    new_attn_k_prompt = jnp.stack(ak_p)
    new_attn_v_prompt = jnp.stack(av_p)
    new_pool_prompt = jnp.stack(pool_p)
    new_mem_k_prompt = jnp.stack(mk_p)
    new_mem_v_prompt = jnp.stack(mv_p)
    new_attn_k_sample = jnp.stack(ak_s)
    new_attn_v_sample = jnp.stack(av_s)
    new_pool_sample = jnp.stack(pool_s)
    return (yp, ys, new_attn_k_prompt, new_attn_v_prompt, new_pool_prompt, new_mem_k_prompt,
            new_mem_v_prompt, new_attn_k_sample, new_attn_v_sample, new_pool_sample)
```

```python
import functools

import jax
import jax.numpy as jnp
from jax import lax
from jax.experimental import pallas as pl
from jax.experimental.pallas import tpu as pltpu

F32 = jnp.float32
BF16 = jnp.bfloat16

D_MODEL = 2048
BATCH = 4
SEQ = 4096
DEPTH = 4
DEC_BATCH = 8
DEC_SEQ = 32
CHUNK = 64
BAND_LEFT = 512
BAND_KEYS = BAND_LEFT + CHUNK
N_HEADS = 16
HEAD_DIM = D_MODEL // N_HEADS
REL_CLIP = 2 * CHUNK
POOL_WINDOWS = (2, 4, 8, 16)
POOL_GROUP = D_MODEL // len(POOL_WINDOWS)
POOL_HIST = max(POOL_WINDOWS) - 1
N_MEM = 256
MEM_HEADS = 4
MEM_HEAD_DIM = D_MODEL // MEM_HEADS
D_FF = 5632
ALPHA = (2.0 * DEPTH) ** 0.25
LN_EPS = 1e-5
NEG_BIG = -1e30

M_PROMPT = BATCH * SEQ
M_SAMPLE = DEC_BATCH * DEC_SEQ
M_ALL = M_PROMPT + M_SAMPLE

TM_MM = 1280
TM_LN = 640
TN_MM = 1024
TN_FF = 512
TK_FF = 1408
QB = 512
HG = 2
TM_MEM = 1024
TM_POOL = 512
HALO = 16

V7X_VMEM_BYTES = 64 * 1024 * 1024


def _vmem_limit(nbytes):
    return int(min(nbytes + (12 << 20), V7X_VMEM_BYTES - (6 << 20)))


def _params(sem, nbytes):
    return pltpu.CompilerParams(dimension_semantics=sem, vmem_limit_bytes=_vmem_limit(nbytes))


def _mm_kernel(a_ref, w_ref, o_ref, wb_ref, *, scale):
    @pl.when(pl.program_id(1) == 0)
    def _():
        wb_ref[...] = w_ref[...].astype(BF16)

    acc = jnp.dot(a_ref[...], wb_ref[...], preferred_element_type=F32)
    if scale != 1.0:
        acc = acc * scale
    o_ref[...] = acc.astype(o_ref.dtype)


def _matmul(a, w, layer, col0, n_out, tm, out_dtype, scale=1.0):
    m, k = a.shape
    tn = TN_MM
    assert m % tm == 0 and n_out % tn == 0 and col0 % tn == 0
    cb = col0 // tn
    osz = jnp.dtype(out_dtype).itemsize
    nbytes = 2 * tm * k * 2 + 2 * k * tn * 4 + k * tn * 2 + 2 * tm * tn * osz + tm * tn * 4
    return pl.pallas_call(
        functools.partial(_mm_kernel, scale=scale),
        grid=(n_out // tn, m // tm),
        in_specs=[pl.BlockSpec((tm, k), lambda j, i: (i, 0)),
                  pl.BlockSpec((None, k, tn), lambda j, i: (layer, 0, j + cb))],
        out_specs=pl.BlockSpec((tm, tn), lambda j, i: (i, j)),
        out_shape=jax.ShapeDtypeStruct((m, n_out), out_dtype),
        scratch_shapes=[pltpu.VMEM((k, tn), BF16)],
        compiler_params=_params(("arbitrary", "arbitrary"), nbytes),
    )(a, w)


def _swiglu_kernel(a_ref, wg_ref, wu_ref, o_ref, wgb_ref, wub_ref):
    @pl.when(pl.program_id(1) == 0)
    def _():
        wgb_ref[...] = wg_ref[...].astype(BF16)
        wub_ref[...] = wu_ref[...].astype(BF16)

    a = a_ref[...]
    gate = jnp.dot(a, wgb_ref[...], preferred_element_type=F32)
    up = jnp.dot(a, wub_ref[...], preferred_element_type=F32)
    o_ref[...] = (gate / (1.0 + jnp.exp(-gate)) * up).astype(o_ref.dtype)


def _swiglu_in(a, w_in, layer):
    m, k = a.shape
    tm, tn = TM_MM, TN_FF
    nj = D_FF // tn
    nbytes = 2 * tm * k * 2 + 4 * k * tn * 4 + 2 * k * tn * 2 + 2 * tm * tn * 2 + 3 * tm * tn * 4
    return pl.pallas_call(
        _swiglu_kernel,
        grid=(nj, m // tm),
        in_specs=[pl.BlockSpec((tm, k), lambda j, i: (i, 0)),
                  pl.BlockSpec((None, k, tn), lambda j, i: (layer, 0, j)),
                  pl.BlockSpec((None, k, tn), lambda j, i: (layer, 0, j + nj))],
        out_specs=pl.BlockSpec((tm, tn), lambda j, i: (i, j)),
        out_shape=jax.ShapeDtypeStruct((m, D_FF), BF16),
        scratch_shapes=[pltpu.VMEM((k, tn), BF16), pltpu.VMEM((k, tn), BF16)],
        compiler_params=_params(("arbitrary", "arbitrary"), nbytes),
    )(a, w_in, w_in)


def _layer_norm(z, g, b):
    mu = jnp.mean(z, axis=-1, keepdims=True)
    zc = z - mu
    var = jnp.mean(zc * zc, axis=-1, keepdims=True)
    return zc * lax.rsqrt(var + LN_EPS) * g + b


def _mm_ln_kernel(a_ref, w_ref, x_ref, g_ref, b_ref, of_ref, ob_ref, *, nk):
    k = pl.program_id(1)
    part = jnp.dot(a_ref[...], w_ref[...], preferred_element_type=F32)

    def finish(acc):
        y = _layer_norm(ALPHA * x_ref[...] + acc, g_ref[...], b_ref[...])
        of_ref[...] = y
        ob_ref[...] = y.astype(BF16)

    if nk == 1:
        finish(part)
    else:
        @pl.when(k == 0)
        def _():
            of_ref[...] = part

        @pl.when(jnp.logical_and(k > 0, k < nk - 1))
        def _():
            of_ref[...] += part

        @pl.when(k == nk - 1)
        def _():
            finish(of_ref[...] + part)


def _mm_ln(a, w, x, g, b, tk):
    m, kdim = a.shape
    tm = TM_LN
    nk = kdim // tk
    assert m % tm == 0 and kdim % tk == 0
    nbytes = (2 * tm * tk * 2 + 2 * tk * D_MODEL * 2 + 2 * tm * D_MODEL * 4
              + 2 * tm * D_MODEL * 4 + 2 * tm * D_MODEL * 2 + 2 * tm * D_MODEL * 4)
    return pl.pallas_call(
        functools.partial(_mm_ln_kernel, nk=nk),
        grid=(m // tm, nk),
        in_specs=[pl.BlockSpec((tm, tk), lambda i, k: (i, k)),
                  pl.BlockSpec((tk, D_MODEL), lambda i, k: (k, 0)),
                  pl.BlockSpec((tm, D_MODEL), lambda i, k: (i, 0)),
                  pl.BlockSpec((1, D_MODEL), lambda i, k: (0, 0)),
                  pl.BlockSpec((1, D_MODEL), lambda i, k: (0, 0))],
        out_specs=[pl.BlockSpec((tm, D_MODEL), lambda i, k: (i, 0)),
                   pl.BlockSpec((tm, D_MODEL), lambda i, k: (i, 0))],
        out_shape=[jax.ShapeDtypeStruct((m, D_MODEL), F32),
                   jax.ShapeDtypeStruct((m, D_MODEL), BF16)],
        compiler_params=_params(("arbitrary", "arbitrary"), nbytes),
    )(a, w, x, g, b)


def _dot_nt(a, b):
    return lax.dot_general(a, b, (((1,), (1,)), ((), ())), preferred_element_type=F32)


def _band_attn_kernel(q_ref, kl_ref, kc_ref, vl_ref, vc_ref, bias_ref, o_ref):
    has_left = pl.program_id(2) > 0
    for h in range(HG):
        sl = slice(h * HEAD_DIM, (h + 1) * HEAD_DIM)
        q = q_ref[:, sl]
        s_l = _dot_nt(q, kl_ref[:, sl].astype(BF16)) + bias_ref[h, :, :QB]
        s_c = _dot_nt(q, kc_ref[:, sl].astype(BF16)) + bias_ref[h, :, QB:]
        s_l = jnp.where(has_left, s_l, NEG_BIG)
        m = jnp.maximum(jnp.max(s_l, axis=-1, keepdims=True), jnp.max(s_c, axis=-1, keepdims=True))
        p_l = jnp.exp(s_l - m)
        p_c = jnp.exp(s_c - m)
        denom = jnp.sum(p_l, axis=-1, keepdims=True) + jnp.sum(p_c, axis=-1, keepdims=True)
        o = (jnp.dot(p_l.astype(BF16), vl_ref[:, sl].astype(BF16), preferred_element_type=F32)
             + jnp.dot(p_c.astype(BF16), vc_ref[:, sl].astype(BF16), preferred_element_type=F32))
        o_ref[:, sl] = (o / denom).astype(o_ref.dtype)


def _band_attn_prompt(q, kv, bias_blk):
    nb = SEQ // QB
    w = HG * HEAD_DIM
    vb = D_MODEL // w
    cur = lambda g, b, i: (b * nb + i, g)
    left = lambda g, b, i: (b * nb + jnp.maximum(i - 1, 0), g)
    cur_v = lambda g, b, i: (b * nb + i, vb + g)
    left_v = lambda g, b, i: (b * nb + jnp.maximum(i - 1, 0), vb + g)
    nbytes = (2 * QB * w * 2 + 8 * QB * w * 4 + 2 * HG * QB * 2 * QB * 4 + 2 * QB * w * 2
              + 6 * QB * QB * 4)
    return pl.pallas_call(
        _band_attn_kernel,
        grid=(N_HEADS // HG, BATCH, nb),
        in_specs=[pl.BlockSpec((QB, w), cur),
                  pl.BlockSpec((QB, w), left),
                  pl.BlockSpec((QB, w), cur),
                  pl.BlockSpec((QB, w), left_v),
                  pl.BlockSpec((QB, w), cur_v),
                  pl.BlockSpec((HG, QB, 2 * QB), lambda g, b, i: (g, 0, 0))],
        out_specs=pl.BlockSpec((QB, w), cur),
        out_shape=jax.ShapeDtypeStruct((M_ALL, D_MODEL), BF16),
        compiler_params=_params(("arbitrary", "arbitrary", "arbitrary"), nbytes),
    )(q, kv, kv, kv, kv, bias_blk)


def _band_attn_sample_kernel(o_any, q_ref, kn_ref, vn_ref, ck_ref, cv_ref, bc_ref, bn_ref, o_ref):
    del o_any
    for h in range(HG):
        sl = slice(h * HEAD_DIM, (h + 1) * HEAD_DIM)
        q = q_ref[:, sl]
        s_c = _dot_nt(q, ck_ref[:, sl].astype(BF16)) + bc_ref[h]
        s_n = _dot_nt(q, kn_ref[:, sl].astype(BF16)) + bn_ref[h]
        m = jnp.maximum(jnp.max(s_c, axis=-1, keepdims=True), jnp.max(s_n, axis=-1, keepdims=True))
        p_c = jnp.exp(s_c - m)
        p_n = jnp.exp(s_n - m)
        denom = jnp.sum(p_c, axis=-1, keepdims=True) + jnp.sum(p_n, axis=-1, keepdims=True)
        o = (jnp.dot(p_c.astype(BF16), cv_ref[:, sl].astype(BF16), preferred_element_type=F32)
             + jnp.dot(p_n.astype(BF16), vn_ref[:, sl].astype(BF16), preferred_element_type=F32))
        o_ref[:, sl] = (o / denom).astype(o_ref.dtype)


def _band_attn_sample(o, q, kv, cache_k, cache_v, bias_c, bias_n, a):
    t = DEC_SEQ
    p = bias_c.shape[-1]
    w = HG * HEAD_DIM
    vb = D_MODEL // w
    rb = M_PROMPT // t
    nbytes = 2 * (t * w * 2 + 2 * t * w * 4 + 2 * p * w * 4 + HG * t * (p + t) * 4 + t * w * 2) + 8 * t * p * 4
    return pl.pallas_call(
        _band_attn_sample_kernel,
        grid=(DEC_BATCH, N_HEADS // HG),
        in_specs=[pl.BlockSpec(memory_space=pl.ANY),
                  pl.BlockSpec((t, w), lambda b, g: (rb + b, g)),
                  pl.BlockSpec((t, w), lambda b, g: (rb + b, g)),
                  pl.BlockSpec((t, w), lambda b, g: (rb + b, vb + g)),
                  pl.BlockSpec((p, w), lambda b, g: (a * DEC_BATCH + b, g)),
                  pl.BlockSpec((p, w), lambda b, g: (a * DEC_BATCH + b, g)),
                  pl.BlockSpec((HG, t, p), lambda b, g: (g, 0, 0)),
                  pl.BlockSpec((HG, t, t), lambda b, g: (g, 0, 0))],
        out_specs=pl.BlockSpec((t, w), lambda b, g: (rb + b, g)),
        out_shape=jax.ShapeDtypeStruct(o.shape, o.dtype),
        input_output_aliases={0: 0},
        compiler_params=_params(("arbitrary", "arbitrary"), nbytes),
    )(o, q, kv, kv, cache_k, cache_v, bias_c, bias_n)


def _mem_attn_body(q_ref, k_ref, v_ref, o_ref):
    for h in range(MEM_HEADS):
        sl = slice(h * MEM_HEAD_DIM, (h + 1) * MEM_HEAD_DIM)
        s = _dot_nt(q_ref[:, sl], k_ref[:, sl].astype(BF16))
        m = jnp.max(s, axis=-1, keepdims=True)
        p = jnp.exp(s - m)
        denom = jnp.sum(p, axis=-1, keepdims=True)
        o = jnp.dot(p.astype(BF16), v_ref[:, sl].astype(BF16), preferred_element_type=F32)
        o_ref[:, sl] = (o / denom).astype(o_ref.dtype)


def _mem_attn_prompt_kernel(q_ref, k_ref, v_ref, o_ref):
    _mem_attn_body(q_ref, k_ref, v_ref, o_ref)


def _mem_attn_sample_kernel(o_any, q_ref, k_ref, v_ref, o_ref):
    del o_any
    _mem_attn_body(q_ref, k_ref, v_ref, o_ref)


def _mem_attn_prompt(q, mem_kv):
    tm = TM_MEM
    nt = SEQ // tm
    nbytes = 2 * (tm * D_MODEL * 2 + 2 * N_MEM * D_MODEL * 4 + tm * D_MODEL * 2) + 6 * tm * N_MEM * 4
    return pl.pallas_call(
        _mem_attn_prompt_kernel,
        grid=(BATCH, nt),
        in_specs=[pl.BlockSpec((tm, D_MODEL), lambda b, i: (b * nt + i, 0)),
                  pl.BlockSpec((N_MEM, D_MODEL), lambda b, i: (b, 0)),
                  pl.BlockSpec((N_MEM, D_MODEL), lambda b, i: (b, 1))],
        out_specs=pl.BlockSpec((tm, D_MODEL), lambda b, i: (b * nt + i, 0)),
        out_shape=jax.ShapeDtypeStruct((M_ALL, D_MODEL), BF16),
        compiler_params=_params(("arbitrary", "arbitrary"), nbytes),
    )(q, mem_kv, mem_kv)


def _mem_attn_sample(o, q, cache_k, cache_v, layer):
    t = DEC_SEQ
    rb = M_PROMPT // t
    nbytes = 2 * (2 * t * D_MODEL * 2 + 2 * N_MEM * D_MODEL * 4) + 6 * t * N_MEM * 4
    return pl.pallas_call(
        _mem_attn_sample_kernel,
        grid=(DEC_BATCH,),
        in_specs=[pl.BlockSpec(memory_space=pl.ANY),
                  pl.BlockSpec((t, D_MODEL), lambda b: (rb + b, 0)),
                  pl.BlockSpec((N_MEM, D_MODEL), lambda b: (layer * DEC_BATCH + b, 0)),
                  pl.BlockSpec((N_MEM, D_MODEL), lambda b: (layer * DEC_BATCH + b, 0))],
        out_specs=pl.BlockSpec((t, D_MODEL), lambda b: (rb + b, 0)),
        out_shape=jax.ShapeDtypeStruct(o.shape, o.dtype),
        input_output_aliases={0: 0},
        compiler_params=_params(("arbitrary",), nbytes),
    )(o, q, cache_k, cache_v)


def _pool_body(x_ref, halo, pos0, n_hist, wp_ref, ps_ref, g_ref, b_ref, of_ref, ob_ref,
               ext_ref, sa_ref, sb_ref):
    tm = x_ref.shape[0]
    top = 2 * HALO
    rows = tm + top
    zeros = jnp.zeros((HALO, D_MODEL), F32)
    ext_ref[0:HALO, :] = zeros
    ext_ref[HALO:top, :] = halo
    ext_ref[top:rows, :] = x_ref[...]
    sa_ref[0:HALO, :] = zeros[:, :POOL_GROUP]
    sb_ref[0:HALO, :] = zeros[:, :POOL_GROUP]
    seen = (pos0 + n_hist + 1 + lax.broadcasted_iota(jnp.int32, (tm, 1), 0)).astype(F32)
    for gi, win in enumerate(POOL_WINDOWS):
        cs = slice(gi * POOL_GROUP, (gi + 1) * POOL_GROUP)
        src, d = None, 1
        bufs = (sa_ref, sb_ref)
        lvl = 0
        while 2 * d < win:
            if src is None:
                val = ext_ref[HALO:rows, cs] + ext_ref[HALO - d:rows - d, cs]
            else:
                val = src[HALO:rows, :] + src[HALO - d:rows - d, :]
            dst = bufs[lvl % 2]
            dst[HALO:rows, :] = val
            src, d, lvl = dst, 2 * d, lvl + 1
        if src is None:
            wsum = ext_ref[top:rows, cs] + ext_ref[top - d:rows - d, cs]
        else:
            wsum = src[top:rows, :] + src[top - d:rows - d, :]
        xg = x_ref[:, cs]
        u = wsum / jnp.minimum(seen, float(win)) - xg
        y = jnp.dot(u.astype(BF16), wp_ref[gi], preferred_element_type=F32) * ps_ref[:, cs]
        of_ref[:, cs] = ALPHA * xg + y
    out = _layer_norm(of_ref[...], g_ref[...], b_ref[...])
    of_ref[...] = out
    ob_ref[...] = out.astype(BF16)


def _pool_prompt_kernel(x_ref, halo_ref, wp_ref, ps_ref, g_ref, b_ref, of_ref, ob_ref,
                        ext_ref, sa_ref, sb_ref):
    it = pl.program_id(0) % (SEQ // TM_POOL)
    halo = jnp.where(it > 0, halo_ref[...], 0.0)
    _pool_body(x_ref, halo, it * TM_POOL, 0, wp_ref, ps_ref, g_ref, b_ref, of_ref, ob_ref,
               ext_ref, sa_ref, sb_ref)


def _pool_sample_kernel(of_any, ob_any, x_ref, halo_ref, wp_ref, ps_ref, g_ref, b_ref, of_ref, ob_ref,
                        ext_ref, sa_ref, sb_ref):
    del of_any, ob_any
    _pool_body(x_ref, halo_ref[...], 0, POOL_HIST, wp_ref, ps_ref, g_ref, b_ref, of_ref, ob_ref,
               ext_ref, sa_ref, sb_ref)


def _pool_scratch(tm):
    return [pltpu.VMEM((tm + 2 * HALO, D_MODEL), F32),
            pltpu.VMEM((tm + 2 * HALO, POOL_GROUP), F32),
            pltpu.VMEM((tm + 2 * HALO, POOL_GROUP), F32)]


def _pool_const_specs(n):
    zero = lambda *_: (0,) * n
    return [pl.BlockSpec((len(POOL_WINDOWS), POOL_GROUP, POOL_GROUP), lambda *_: (0, 0, 0)),
            pl.BlockSpec((1, D_MODEL), lambda *_: (0, 0)),
            pl.BlockSpec((1, D_MODEL), lambda *_: (0, 0)),
            pl.BlockSpec((1, D_MODEL), lambda *_: (0, 0))]


def _pool_prompt(x, wp, ps, g, b):
    tm = TM_POOL
    hb = tm // HALO
    nbytes = (2 * tm * D_MODEL * 4 + 2 * HALO * D_MODEL * 4 + 2 * wp.size * 2 + 2 * tm * D_MODEL * 6
              + (tm + 2 * HALO) * (D_MODEL + 2 * POOL_GROUP) * 4 + 2 * tm * D_MODEL * 4)
    return pl.pallas_call(
        _pool_prompt_kernel,
        grid=(M_PROMPT // tm,),
        in_specs=[pl.BlockSpec((tm, D_MODEL), lambda i: (i, 0)),
                  pl.BlockSpec((HALO, D_MODEL), lambda i: (jnp.maximum(i * hb - 1, 0), 0))]
                 + _pool_const_specs(2),
        out_specs=[pl.BlockSpec((tm, D_MODEL), lambda i: (i, 0)),
                   pl.BlockSpec((tm, D_MODEL), lambda i: (i, 0))],
        out_shape=[jax.ShapeDtypeStruct((M_ALL, D_MODEL), F32),
                   jax.ShapeDtypeStruct((M_ALL, D_MODEL), BF16)],
        scratch_shapes=_pool_scratch(tm),
        compiler_params=_params(("arbitrary",), nbytes),
    )(x, x, wp, ps, g, b)


def _pool_sample(of, ob, x, state, wp, ps, g, b):
    tm = DEC_SEQ
    rb = M_PROMPT // tm
    nbytes = (2 * tm * D_MODEL * 4 + 2 * HALO * D_MODEL * 4 + 2 * wp.size * 2 + 2 * tm * D_MODEL * 6
              + (tm + 2 * HALO) * (D_MODEL + 2 * POOL_GROUP) * 4 + 2 * tm * D_MODEL * 4)
    return pl.pallas_call(
        _pool_sample_kernel,
        grid=(DEC_BATCH,),
        in_specs=[pl.BlockSpec(memory_space=pl.ANY),
                  pl.BlockSpec(memory_space=pl.ANY),
                  pl.BlockSpec((tm, D_MODEL), lambda i: (rb + i, 0)),
                  pl.BlockSpec((None, HALO, D_MODEL), lambda i: (i, 0, 0))]
                 + _pool_const_specs(2),
        out_specs=[pl.BlockSpec((tm, D_MODEL), lambda i: (rb + i, 0)),
                   pl.BlockSpec((tm, D_MODEL), lambda i: (rb + i, 0))],
        out_shape=[jax.ShapeDtypeStruct(of.shape, of.dtype),
                   jax.ShapeDtypeStruct(ob.shape, ob.dtype)],
        input_output_aliases={0: 0, 1: 1},
        scratch_shapes=_pool_scratch(tm),
        compiler_params=_params(("arbitrary",), nbytes),
    )(of, ob, x, state, wp, ps, g, b)


def _rel_bias(table, qpos, kpos):
    idx = jnp.clip(qpos[:, None] - kpos[None, :], -REL_CLIP, REL_CLIP) + REL_CLIP
    return table[:, idx].astype(F32)


def _prompt_bias_block(table):
    r = jnp.arange(QB)
    c = jnp.arange(2 * QB)
    bias = _rel_bias(table, r, c - BAND_LEFT)
    off = c[None, :] - (r[:, None] // CHUNK) * CHUNK
    in_band = jnp.logical_and(off >= 0, off < BAND_KEYS)
    return jnp.where(in_band[None], bias, NEG_BIG)


def kernel(x_prompt, x_sample, cache_attn_k, cache_attn_v, state_pool, cache_mem_k, cache_mem_v,
           mem_prompt, w_qkv, w_attn_o, rel_bias, w_pool, pool_scale, w_mem_q, w_mem_kv, w_mem_o,
           w_ffn_in, w_ffn_out, ln_g, ln_b):
    D = D_MODEL
    past = cache_attn_k.shape[2]
    x_f = jnp.concatenate([x_prompt.reshape(M_PROMPT, D), x_sample.reshape(M_SAMPLE, D)], axis=0)
    x_b = x_f.astype(BF16)
    mem_b = mem_prompt.reshape(BATCH * N_MEM, D).astype(BF16)
    w_attn_o_b = w_attn_o.astype(BF16)
    w_mem_o_b = w_mem_o.astype(BF16)
    w_ffn_out_b = w_ffn_out.astype(BF16)
    w_pool_b = w_pool.astype(BF16)
    cache_ak = cache_attn_k.reshape(-1, D)
    cache_av = cache_attn_v.reshape(-1, D)
    cache_mk = cache_mem_k.reshape(-1, D)
    cache_mv = cache_mem_v.reshape(-1, D)
    state_pad = jnp.pad(state_pool, ((0, 0), (0, 0), (HALO - POOL_HIST, 0), (0, 0)))

    ak_p, av_p, pool_p, mk_p, mv_p, ak_s, av_s, pool_s = [], [], [], [], [], [], [], []
    for i in range(DEPTH):
        g = ln_g[i].reshape(3, 1, D)
        b = ln_b[i].reshape(3, 1, D)
        if i % 2 == 0:
            a = i // 2
            q = _matmul(x_b, w_qkv, a, 0, D, TM_MM, BF16, scale=HEAD_DIM ** -0.5)
            kv = _matmul(x_b, w_qkv, a, D, 2 * D, TM_MM, F32)
            bias_blk = _prompt_bias_block(rel_bias[a])
            kpos = jnp.concatenate([jnp.arange(past) - past, jnp.arange(DEC_SEQ)])
            bias_s = _rel_bias(rel_bias[a], jnp.arange(DEC_SEQ), kpos)
            o = _band_attn_prompt(q, kv, bias_blk)
            o = _band_attn_sample(o, q, kv, cache_ak, cache_av, bias_s[:, :, :past], bias_s[:, :, past:], a)
            kv_p = kv[:M_PROMPT].reshape(BATCH, SEQ, 2, N_HEADS, HEAD_DIM)[:, SEQ - min(BAND_LEFT, SEQ):]
            kv_s = kv[M_PROMPT:].reshape(DEC_BATCH, DEC_SEQ, 2, N_HEADS, HEAD_DIM)
            ak_p.append(kv_p[:, :, 0])
            av_p.append(kv_p[:, :, 1])
            ak_s.append(kv_s[:, :, 0])
            av_s.append(kv_s[:, :, 1])
            x_f, x_b = _mm_ln(o, w_attn_o_b[a], x_f, g[0], b[0], D)
        else:
            p = i // 2
            xs = x_f[M_PROMPT:].reshape(DEC_BATCH, DEC_SEQ, D)
            pool_p.append(x_f[:M_PROMPT].reshape(BATCH, SEQ, D)[:, SEQ - POOL_HIST:])
            pool_s.append(jnp.concatenate([state_pool[p], xs], axis=1)[:, DEC_SEQ:])
            ps = pool_scale[p].reshape(1, D)
            y_f, y_b = _pool_prompt(x_f, w_pool_b[p], ps, g[0], b[0])
            x_f, x_b = _pool_sample(y_f, y_b, x_f, state_pad[p], w_pool_b[p], ps, g[0], b[0])
        mem_kv = _matmul(mem_b, w_mem_kv, i, 0, 2 * D, BATCH * N_MEM, F32)
        mk_p.append(mem_kv[:, :D].reshape(BATCH, N_MEM, MEM_HEADS, MEM_HEAD_DIM))
        mv_p.append(mem_kv[:, D:].reshape(BATCH, N_MEM, MEM_HEADS, MEM_HEAD_DIM))
        q = _matmul(x_b, w_mem_q, i, 0, D, TM_MM, BF16, scale=MEM_HEAD_DIM ** -0.5)
        o = _mem_attn_prompt(q, mem_kv)
        o = _mem_attn_sample(o, q, cache_mk, cache_mv, i)
        x_f, x_b = _mm_ln(o, w_mem_o_b[i], x_f, g[1], b[1], D)
        h = _swiglu_in(x_b, w_ffn_in, i)
        x_f, x_b = _mm_ln(h, w_ffn_out_b[i], x_f, g[2], b[2], TK_FF)

    return (x_f[:M_PROMPT].reshape(BATCH, SEQ, D),
            x_f[M_PROMPT:].reshape(DEC_BATCH, DEC_SEQ, D),
            jnp.stack(ak_p), jnp.stack(av_p), jnp.stack(pool_p), jnp.stack(mk_p), jnp.stack(mv_p),
            jnp.stack(ak_s), jnp.stack(av_s), jnp.stack(pool_s))
```

```python
import functools

import numpy as np
import jax
import jax.numpy as jnp
from jax import lax
from jax.experimental import pallas as pl
from jax.experimental.pallas import tpu as pltpu

F32 = jnp.float32
BF16 = jnp.bfloat16

D_MODEL = 2048
BATCH = 4
SEQ = 4096
DEPTH = 4
DEC_BATCH = 8
DEC_SEQ = 32
CHUNK = 64
BAND_LEFT = 512
BAND_KEYS = BAND_LEFT + CHUNK
N_HEADS = 16
HEAD_DIM = D_MODEL // N_HEADS
REL_CLIP = 2 * CHUNK
POOL_WINDOWS = (2, 4, 8, 16)
POOL_GROUP = D_MODEL // len(POOL_WINDOWS)
POOL_HIST = max(POOL_WINDOWS) - 1
N_MEM = 256
MEM_HEADS = 4
MEM_HEAD_DIM = D_MODEL // MEM_HEADS
D_FF = 5632
ALPHA = (2.0 * DEPTH) ** 0.25
LN_EPS = 1e-5
NEG_BIG = -1e30

M_PROMPT = BATCH * SEQ
M_SAMPLE = DEC_BATCH * DEC_SEQ
M_ALL = M_PROMPT + M_SAMPLE

TM_MM = 1280
TN_MM = 1024
TN_FF = 512
TM_LN = 640
TM_LN_PROMPT = 512
QB = 512
PAIR = 2 * CHUNK
PAIR_KEYS = BAND_LEFT + PAIR
HG = 8
TM_MEM = 1024
TM_POOL = 512
HALO = 16

V7X_VMEM_BYTES = 64 * 1024 * 1024


def _vmem_limit(nbytes):
    return int(min(nbytes + (12 << 20), V7X_VMEM_BYTES - (6 << 20)))


def _params(sem, nbytes):
    return pltpu.CompilerParams(dimension_semantics=sem, vmem_limit_bytes=_vmem_limit(nbytes))


def _mm_kernel(a_ref, w_ref, *rest, scale):
    o_refs, wb_ref = rest[:-1], rest[-1]

    @pl.when(pl.program_id(1) == 0)
    def _():
        wb_ref[...] = w_ref[...].astype(BF16)

    acc = jnp.dot(a_ref[...], wb_ref[...], preferred_element_type=F32)
    if scale != 1.0:
        acc = acc * scale
    for o_ref in o_refs:
        o_ref[...] = acc.astype(o_ref.dtype)


def _matmul(a, w, layer, col0, n_out, tm, out_dtypes, scale=1.0):
    m, k = a.shape
    tn = TN_MM
    assert m % tm == 0 and n_out % tn == 0 and col0 % tn == 0
    cb = col0 // tn
    osz = sum(jnp.dtype(d).itemsize for d in out_dtypes)
    nbytes = 2 * tm * k * 2 + 2 * k * tn * 4 + k * tn * 2 + 2 * tm * tn * osz + tm * tn * 4
    return pl.pallas_call(
        functools.partial(_mm_kernel, scale=scale),
        grid=(n_out // tn, m // tm),
        in_specs=[pl.BlockSpec((tm, k), lambda j, i: (i, 0)),
                  pl.BlockSpec((None, k, tn), lambda j, i: (layer, 0, j + cb))],
        out_specs=[pl.BlockSpec((tm, tn), lambda j, i: (i, j)) for _ in out_dtypes],
        out_shape=[jax.ShapeDtypeStruct((m, n_out), d) for d in out_dtypes],
        scratch_shapes=[pltpu.VMEM((k, tn), BF16)],
        compiler_params=_params(("arbitrary", "arbitrary"), nbytes),
    )(a, w)


def _swiglu_kernel(a_ref, wg_ref, wu_ref, o_ref, wgb_ref, wub_ref):
    @pl.when(pl.program_id(1) == 0)
    def _():
        wgb_ref[...] = wg_ref[...].astype(BF16)
        wub_ref[...] = wu_ref[...].astype(BF16)

    a = a_ref[...]
    gate = jnp.dot(a, wgb_ref[...], preferred_element_type=F32)
    up = jnp.dot(a, wub_ref[...], preferred_element_type=F32)
    o_ref[...] = (gate / (1.0 + jnp.exp(-gate)) * up).astype(o_ref.dtype)


def _swiglu_in(a, w_in, layer):
    m, k = a.shape
    tm, tn = TM_MM, TN_FF
    nj = D_FF // tn
    nbytes = 2 * tm * k * 2 + 4 * k * tn * 4 + 2 * k * tn * 2 + 2 * tm * tn * 2 + 3 * tm * tn * 4
    return pl.pallas_call(
        _swiglu_kernel,
        grid=(nj, m // tm),
        in_specs=[pl.BlockSpec((tm, k), lambda j, i: (i, 0)),
                  pl.BlockSpec((None, k, tn), lambda j, i: (layer, 0, j)),
                  pl.BlockSpec((None, k, tn), lambda j, i: (layer, 0, j + nj))],
        out_specs=pl.BlockSpec((tm, tn), lambda j, i: (i, j)),
        out_shape=jax.ShapeDtypeStruct((m, D_FF), BF16),
        scratch_shapes=[pltpu.VMEM((k, tn), BF16), pltpu.VMEM((k, tn), BF16)],
        compiler_params=_params(("arbitrary", "arbitrary"), nbytes),
    )(a, w_in, w_in)


def _layer_norm(z, g, b):
    mu = jnp.mean(z, axis=-1, keepdims=True)
    zc = z - mu
    var = jnp.mean(zc * zc, axis=-1, keepdims=True)
    return zc * lax.rsqrt(var + LN_EPS) * g + b


def _mm_ln_kernel(*refs, nn, n_alias, n_out):
    a_ref, w_ref, x_ref, g_ref, b_ref = refs[n_alias:n_alias + 5]
    o_refs = refs[n_alias + 5:n_alias + 5 + n_out]
    z_ref = refs[-1]
    n = pl.program_id(1)
    tn = w_ref.shape[1]
    z_ref[n] = ALPHA * x_ref[...] + jnp.dot(a_ref[...], w_ref[...], preferred_element_type=F32)

    @pl.when(n == nn - 1)
    def _():
        d = nn * tn
        tot = z_ref[0].sum(axis=-1, keepdims=True)
        for c in range(1, nn):
            tot = tot + z_ref[c].sum(axis=-1, keepdims=True)
        mu = tot / d
        sq = None
        for c in range(nn):
            zc = z_ref[c] - mu
            part = (zc * zc).sum(axis=-1, keepdims=True)
            sq = part if sq is None else sq + part
        rstd = lax.rsqrt(sq / d + LN_EPS)
        for c in range(nn):
            cs = slice(c * tn, (c + 1) * tn)
            y = (z_ref[c] - mu) * rstd * g_ref[:, cs] + b_ref[:, cs]
            for o_ref in o_refs:
                o_ref[:, cs] = y.astype(o_ref.dtype)


def _mm_ln(a, w, x, g, b, *, tm, n_tiles, a_blk0, x_blk0, o_blk0, out_rows, out_dtypes, into=None):
    kdim = a.shape[1]
    tn = 512 if kdim <= D_MODEL else 256
    nn = D_MODEL // tn
    n_alias = 0 if into is None else len(into)
    osz = sum(jnp.dtype(d).itemsize for d in out_dtypes)
    nbytes = (2 * tm * kdim * 2 + 2 * kdim * tn * 2 + 2 * tm * tn * 4 + tm * D_MODEL * 4
              + 2 * tm * D_MODEL * osz + 2 * tm * tn * 4)
    in_specs = [pl.BlockSpec(memory_space=pl.ANY)] * n_alias + [
        pl.BlockSpec((tm, kdim), lambda i, n: (a_blk0 + i, 0)),
        pl.BlockSpec((kdim, tn), lambda i, n: (0, n)),
        pl.BlockSpec((tm, tn), lambda i, n: (x_blk0 + i, n)),
        pl.BlockSpec((1, D_MODEL), lambda i, n: (0, 0)),
        pl.BlockSpec((1, D_MODEL), lambda i, n: (0, 0))]
    return pl.pallas_call(
        functools.partial(_mm_ln_kernel, nn=nn, n_alias=n_alias, n_out=len(out_dtypes)),
        grid=(n_tiles, nn),
        in_specs=in_specs,
        out_specs=[pl.BlockSpec((tm, D_MODEL), lambda i, n: (o_blk0 + i, 0)) for _ in out_dtypes],
        out_shape=[jax.ShapeDtypeStruct((out_rows, D_MODEL), d) for d in out_dtypes],
        input_output_aliases={j: j for j in range(n_alias)},
        scratch_shapes=[pltpu.VMEM((nn, tm, tn), F32)],
        compiler_params=_params(("arbitrary", "arbitrary"), nbytes),
    )(*(into or ()), a, w, x, g, b)


def _mm_ln_all(a, w, x, g, b):
    return _mm_ln(a, w, x, g, b, tm=TM_LN, n_tiles=M_ALL // TM_LN, a_blk0=0, x_blk0=0, o_blk0=0,
                  out_rows=M_ALL, out_dtypes=(F32, BF16))


def _dot_nt(a, b):
    return lax.dot_general(a, b, (((1,), (1,)), ((), ())), preferred_element_type=F32)


def _band_attn_kernel(q_ref, kl_ref, kc_ref, vl_ref, vc_ref, bias_ref, o_ref, k_sc, v_sc):
    hg = pl.program_id(2)
    k_sc[0:QB, :] = kl_ref[...]
    k_sc[QB:2 * QB, :] = kc_ref[...]
    v_sc[0:QB, :] = vl_ref[...]
    v_sc[QB:2 * QB, :] = vc_ref[...]
    no_left = pl.program_id(1) == 0
    col = lax.broadcasted_iota(jnp.int32, (PAIR, PAIR_KEYS), 1)
    for h in range(HG):
        sl = slice(h * HEAD_DIM, (h + 1) * HEAD_DIM)
        bias = bias_ref[hg * HG + h]
        for pr in range(QB // PAIR):
            r0 = pr * PAIR
            s = _dot_nt(q_ref[r0:r0 + PAIR, sl], k_sc[r0:r0 + PAIR_KEYS, sl]) + bias
            s = jnp.where(jnp.logical_and(no_left, col < QB - r0), NEG_BIG, s)
            m = jnp.max(s, axis=-1, keepdims=True)
            p = jnp.exp(s - m)
            denom = jnp.sum(p, axis=-1, keepdims=True)
            o = jnp.dot(p.astype(BF16), v_sc[r0:r0 + PAIR_KEYS, sl], preferred_element_type=F32)
            o_ref[r0:r0 + PAIR, sl] = (o / denom).astype(o_ref.dtype)


def _band_attn_prompt(q, kv, bias):
    nb = SEQ // QB
    w = HG * HEAD_DIM
    vb = D_MODEL // w
    cur = lambda b, i, g: (b * nb + i, g)
    left = lambda b, i, g: (b * nb + jnp.maximum(i - 1, 0), g)
    cur_v = lambda b, i, g: (b * nb + i, vb + g)
    left_v = lambda b, i, g: (b * nb + jnp.maximum(i - 1, 0), vb + g)
    nbytes = 2 * 6 * QB * w * 2 + 2 * bias.size * 4 + 4 * QB * w * 2 + 8 * PAIR * PAIR_KEYS * 4
    return pl.pallas_call(
        _band_attn_kernel,
        grid=(BATCH, nb, N_HEADS // HG),
        in_specs=[pl.BlockSpec((QB, w), cur),
                  pl.BlockSpec((QB, w), left),
                  pl.BlockSpec((QB, w), cur),
                  pl.BlockSpec((QB, w), left_v),
                  pl.BlockSpec((QB, w), cur_v),
                  pl.BlockSpec((N_HEADS, PAIR, PAIR_KEYS), lambda b, i, g: (0, 0, 0))],
        out_specs=pl.BlockSpec((QB, w), cur),
        out_shape=jax.ShapeDtypeStruct((M_ALL, D_MODEL), BF16),
        scratch_shapes=[pltpu.VMEM((2 * QB, w), BF16), pltpu.VMEM((2 * QB, w), BF16)],
        compiler_params=_params(("arbitrary", "arbitrary", "arbitrary"), nbytes),
    )(q, kv, kv, kv, kv, bias)


def _band_attn_sample_kernel(o_any, q_ref, kn_ref, vn_ref, ck_ref, cv_ref, bc_ref, bn_ref, o_ref):
    del o_any
    hg = pl.program_id(1)
    for h in range(HG):
        sl = slice(h * HEAD_DIM, (h + 1) * HEAD_DIM)
        q = q_ref[:, sl]
        s_c = _dot_nt(q, ck_ref[:, sl].astype(BF16)) + bc_ref[hg * HG + h]
        s_n = _dot_nt(q, kn_ref[:, sl]) + bn_ref[hg * HG + h]
        m = jnp.maximum(jnp.max(s_c, axis=-1, keepdims=True), jnp.max(s_n, axis=-1, keepdims=True))
        p_c = jnp.exp(s_c - m)
        p_n = jnp.exp(s_n - m)
        denom = jnp.sum(p_c, axis=-1, keepdims=True) + jnp.sum(p_n, axis=-1, keepdims=True)
        o = (jnp.dot(p_c.astype(BF16), cv_ref[:, sl].astype(BF16), preferred_element_type=F32)
             + jnp.dot(p_n.astype(BF16), vn_ref[:, sl], preferred_element_type=F32))
        o_ref[:, sl] = (o / denom).astype(o_ref.dtype)


def _band_attn_sample(o, q, kv, cache_k, cache_v, bias_c, bias_n, a):
    t = DEC_SEQ
    p = bias_c.shape[-1]
    w = HG * HEAD_DIM
    vb = D_MODEL // w
    rb = M_PROMPT // t
    nbytes = (2 * (4 * t * w * 2 + 2 * p * w * 4 + N_HEADS * t * (p + t) * 4) + 2 * p * w * 2 + 8 * t * p * 4)
    return pl.pallas_call(
        _band_attn_sample_kernel,
        grid=(DEC_BATCH, N_HEADS // HG),
        in_specs=[pl.BlockSpec(memory_space=pl.ANY),
                  pl.BlockSpec((t, w), lambda b, g: (rb + b, g)),
                  pl.BlockSpec((t, w), lambda b, g: (rb + b, g)),
                  pl.BlockSpec((t, w), lambda b, g: (rb + b, vb + g)),
                  pl.BlockSpec((p, w), lambda b, g: (a * DEC_BATCH + b, g)),
                  pl.BlockSpec((p, w), lambda b, g: (a * DEC_BATCH + b, g)),
                  pl.BlockSpec((N_HEADS, t, p), lambda b, g: (0, 0, 0)),
                  pl.BlockSpec((N_HEADS, t, t), lambda b, g: (0, 0, 0))],
        out_specs=pl.BlockSpec((t, w), lambda b, g: (rb + b, g)),
        out_shape=jax.ShapeDtypeStruct(o.shape, o.dtype),
        input_output_aliases={0: 0},
        compiler_params=_params(("arbitrary", "arbitrary"), nbytes),
    )(o, q, kv, kv, cache_k, cache_v, bias_c, bias_n)


def _mem_attn_body(q_ref, k_ref, v_ref, o_ref):
    for h in range(MEM_HEADS):
        sl = slice(h * MEM_HEAD_DIM, (h + 1) * MEM_HEAD_DIM)
        s = _dot_nt(q_ref[:, sl], k_ref[:, sl].astype(BF16))
        m = jnp.max(s, axis=-1, keepdims=True)
        p = jnp.exp(s - m)
        denom = jnp.sum(p, axis=-1, keepdims=True)
        o = jnp.dot(p.astype(BF16), v_ref[:, sl].astype(BF16), preferred_element_type=F32)
        o_ref[:, sl] = (o / denom).astype(o_ref.dtype)


def _mem_attn_prompt_kernel(q_ref, k_ref, v_ref, o_ref):
    _mem_attn_body(q_ref, k_ref, v_ref, o_ref)


def _mem_attn_sample_kernel(o_any, q_ref, k_ref, v_ref, o_ref):
    del o_any
    _mem_attn_body(q_ref, k_ref, v_ref, o_ref)


def _mem_attn_prompt(q, mem_kv):
    tm = TM_MEM
    nt = SEQ // tm
    nbytes = 2 * (tm * D_MODEL * 2 + 2 * N_MEM * D_MODEL * 4 + tm * D_MODEL * 2) + 6 * tm * N_MEM * 4
    return pl.pallas_call(
        _mem_attn_prompt_kernel,
        grid=(BATCH, nt),
        in_specs=[pl.BlockSpec((tm, D_MODEL), lambda b, i: (b * nt + i, 0)),
                  pl.BlockSpec((N_MEM, D_MODEL), lambda b, i: (b, 0)),
                  pl.BlockSpec((N_MEM, D_MODEL), lambda b, i: (b, 1))],
        out_specs=pl.BlockSpec((tm, D_MODEL), lambda b, i: (b * nt + i, 0)),
        out_shape=jax.ShapeDtypeStruct((M_ALL, D_MODEL), BF16),
        compiler_params=_params(("arbitrary", "arbitrary"), nbytes),
    )(q, mem_kv, mem_kv)


def _mem_attn_sample(o, q, cache_k, cache_v, layer):
    t = DEC_SEQ
    rb = M_PROMPT // t
    nbytes = 2 * (2 * t * D_MODEL * 2 + 2 * N_MEM * D_MODEL * 4) + 6 * t * N_MEM * 4
    return pl.pallas_call(
        _mem_attn_sample_kernel,
        grid=(DEC_BATCH,),
        in_specs=[pl.BlockSpec(memory_space=pl.ANY),
                  pl.BlockSpec((t, D_MODEL), lambda b: (rb + b, 0)),
                  pl.BlockSpec((N_MEM, D_MODEL), lambda b: (layer * DEC_BATCH + b, 0)),
                  pl.BlockSpec((N_MEM, D_MODEL), lambda b: (layer * DEC_BATCH + b, 0))],
        out_specs=pl.BlockSpec((t, D_MODEL), lambda b: (rb + b, 0)),
        out_shape=jax.ShapeDtypeStruct(o.shape, o.dtype),
        input_output_aliases={0: 0},
        compiler_params=_params(("arbitrary",), nbytes),
    )(o, q, cache_k, cache_v)


def _pool_body(x_ref, halo, pos0, n_hist, wp_ref, ps_ref, g_ref, b_ref, of_ref, ob_ref,
               ext_ref, sa_ref, sb_ref):
    tm = x_ref.shape[0]
    top = 2 * HALO
    rows = tm + top
    zeros = jnp.zeros((HALO, D_MODEL), F32)
    ext_ref[0:HALO, :] = zeros
    ext_ref[HALO:top, :] = halo
    ext_ref[top:rows, :] = x_ref[...]
    sa_ref[0:HALO, :] = zeros[:, :POOL_GROUP]
    sb_ref[0:HALO, :] = zeros[:, :POOL_GROUP]
    seen = (pos0 + n_hist + 1 + lax.broadcasted_iota(jnp.int32, (tm, 1), 0)).astype(F32)
    bufs = (sa_ref, sb_ref)
    for gi, win in enumerate(POOL_WINDOWS):
        cs = slice(gi * POOL_GROUP, (gi + 1) * POOL_GROUP)
        src, d, lvl = None, 1, 0
        while 2 * d < win:
            if src is None:
                val = ext_ref[HALO:rows, cs] + ext_ref[HALO - d:rows - d, cs]
            else:
                val = src[HALO:rows, :] + src[HALO - d:rows - d, :]
            dst = bufs[lvl % 2]
            dst[HALO:rows, :] = val
            src, d, lvl = dst, 2 * d, lvl + 1
        if src is None:
            wsum = ext_ref[top:rows, cs] + ext_ref[top - d:rows - d, cs]
        else:
            wsum = src[top:rows, :] + src[top - d:rows - d, :]
        xg = x_ref[:, cs]
        u = wsum / jnp.minimum(seen, float(win)) - xg
        y = jnp.dot(u.astype(BF16), wp_ref[gi], preferred_element_type=F32) * ps_ref[:, cs]
        of_ref[:, cs] = ALPHA * xg + y
    out = _layer_norm(of_ref[...], g_ref[...], b_ref[...])
    of_ref[...] = out
    ob_ref[...] = out.astype(BF16)


def _pool_prompt_kernel(x_ref, halo_ref, wp_ref, ps_ref, g_ref, b_ref, of_ref, ob_ref,
                        ext_ref, sa_ref, sb_ref):
    it = pl.program_id(0) % (SEQ // TM_POOL)
    halo = jnp.where(it > 0, halo_ref[...], 0.0)
    _pool_body(x_ref, halo, it * TM_POOL, 0, wp_ref, ps_ref, g_ref, b_ref, of_ref, ob_ref,
               ext_ref, sa_ref, sb_ref)


def _pool_sample_kernel(of_any, ob_any, x_ref, halo_ref, wp_ref, ps_ref, g_ref, b_ref, of_ref, ob_ref,
                        ext_ref, sa_ref, sb_ref):
    del of_any, ob_any
    _pool_body(x_ref, halo_ref[...], 0, POOL_HIST, wp_ref, ps_ref, g_ref, b_ref, of_ref, ob_ref,
               ext_ref, sa_ref, sb_ref)


def _pool_scratch(tm):
    return [pltpu.VMEM((tm + 2 * HALO, D_MODEL), F32),
            pltpu.VMEM((tm + 2 * HALO, POOL_GROUP), F32),
            pltpu.VMEM((tm + 2 * HALO, POOL_GROUP), F32)]


def _pool_const_specs():
    return [pl.BlockSpec((len(POOL_WINDOWS), POOL_GROUP, POOL_GROUP), lambda i: (0, 0, 0)),
            pl.BlockSpec((1, D_MODEL), lambda i: (0, 0)),
            pl.BlockSpec((1, D_MODEL), lambda i: (0, 0)),
            pl.BlockSpec((1, D_MODEL), lambda i: (0, 0))]


def _pool_bytes(tm, wp):
    return (2 * tm * D_MODEL * 4 + 2 * HALO * D_MODEL * 4 + 2 * wp.size * 2 + 2 * tm * D_MODEL * 6
            + (tm + 2 * HALO) * (D_MODEL + 2 * POOL_GROUP) * 4 + 2 * tm * D_MODEL * 4)


def _pool_prompt(x, wp, ps, g, b):
    tm = TM_POOL
    hb = tm // HALO
    return pl.pallas_call(
        _pool_prompt_kernel,
        grid=(M_PROMPT // tm,),
        in_specs=[pl.BlockSpec((tm, D_MODEL), lambda i: (i, 0)),
                  pl.BlockSpec((HALO, D_MODEL), lambda i: (jnp.maximum(i * hb - 1, 0), 0))]
                 + _pool_const_specs(),
        out_specs=[pl.BlockSpec((tm, D_MODEL), lambda i: (i, 0)),
                   pl.BlockSpec((tm, D_MODEL), lambda i: (i, 0))],
        out_shape=[jax.ShapeDtypeStruct((M_ALL, D_MODEL), F32),
                   jax.ShapeDtypeStruct((M_ALL, D_MODEL), BF16)],
        scratch_shapes=_pool_scratch(tm),
        compiler_params=_params(("arbitrary",), _pool_bytes(tm, wp)),
    )(x, x, wp, ps, g, b)


def _pool_sample(of, ob, x, state, wp, ps, g, b):
    tm = DEC_SEQ
    rb = M_PROMPT // tm
    return pl.pallas_call(
        _pool_sample_kernel,
        grid=(DEC_BATCH,),
        in_specs=[pl.BlockSpec(memory_space=pl.ANY),
                  pl.BlockSpec(memory_space=pl.ANY),
                  pl.BlockSpec((tm, D_MODEL), lambda i: (rb + i, 0)),
                  pl.BlockSpec((None, HALO, D_MODEL), lambda i: (i, 0, 0))]
                 + _pool_const_specs(),
        out_specs=[pl.BlockSpec((tm, D_MODEL), lambda i: (rb + i, 0)),
                   pl.BlockSpec((tm, D_MODEL), lambda i: (rb + i, 0))],
        out_shape=[jax.ShapeDtypeStruct(of.shape, of.dtype),
                   jax.ShapeDtypeStruct(ob.shape, ob.dtype)],
        input_output_aliases={0: 0, 1: 1},
        scratch_shapes=_pool_scratch(tm),
        compiler_params=_params(("arbitrary",), _pool_bytes(tm, wp)),
    )(of, ob, x, state, wp, ps, g, b)


def _pair_bias(table):
    h = table.shape[0]
    period = PAIR + PAIR_KEYS
    far = table[:, 2 * REL_CLIP:]
    vec = jnp.concatenate([jnp.broadcast_to(far, (h, BAND_LEFT - REL_CLIP)),
                           jnp.flip(table[:, 1:], axis=1),
                           jnp.broadcast_to(far, (h, period - BAND_LEFT - REL_CLIP))], axis=1)
    flat = jnp.tile(vec, (1, PAIR))[:, :PAIR * (period - 1)]
    toe = flat.reshape(h, PAIR, period - 1)[:, :, :PAIR_KEYS]
    r = np.arange(PAIR)[:, None]
    c = np.arange(PAIR_KEYS)[None, :]
    off = c - (r // CHUNK) * CHUNK
    in_band = np.logical_and(off >= 0, off < BAND_KEYS)
    return jnp.where(in_band[None], toe, NEG_BIG).astype(F32)


def _tail_rows(x, n_tail, cols):
    return jnp.stack([x[(bi + 1) * SEQ - n_tail:(bi + 1) * SEQ, cols] for bi in range(BATCH)])


def kernel(x_prompt, x_sample, cache_attn_k, cache_attn_v, state_pool, cache_mem_k, cache_mem_v,
           mem_prompt, w_qkv, w_attn_o, rel_bias, w_pool, pool_scale, w_mem_q, w_mem_kv, w_mem_o,
           w_ffn_in, w_ffn_out, ln_g, ln_b):
    D = D_MODEL
    assert cache_attn_k.shape[2] == BAND_LEFT and DEPTH % 2 == 0
    xp = x_prompt.reshape(M_PROMPT, D)
    xs = x_sample.reshape(M_SAMPLE, D)
    x_f = None
    x_b = jnp.concatenate([xp.astype(BF16), xs.astype(BF16)], axis=0)
    mem_b = mem_prompt.reshape(BATCH * N_MEM, D).astype(BF16)
    w_attn_o_b = w_attn_o.astype(BF16)
    w_mem_o_b = w_mem_o.astype(BF16)
    w_ffn_out_b = w_ffn_out.astype(BF16)
    w_pool_b = w_pool.astype(BF16)
    cache_ak = cache_attn_k.reshape(-1, D)
    cache_av = cache_attn_v.reshape(-1, D)
    cache_mk = cache_mem_k.reshape(-1, D)
    cache_mv = cache_mem_v.reshape(-1, D)
    state_pad = jnp.pad(state_pool, ((0, 0), (0, 0), (HALO - POOL_HIST, 0), (0, 0)))
    k_cols, v_cols = slice(0, D), slice(D, 2 * D)

    ak_p, av_p, pool_p, mk_p, mv_p, ak_s, av_s, pool_s = [], [], [], [], [], [], [], []
    for i in range(DEPTH):
        g = ln_g[i].reshape(3, 1, D)
        b = ln_b[i].reshape(3, 1, D)
        if i % 2 == 0:
            a = i // 2
            q, = _matmul(x_b, w_qkv, a, 0, D, TM_MM, (BF16,), scale=HEAD_DIM ** -0.5)
            kv, kv_b = _matmul(x_b, w_qkv, a, D, 2 * D, TM_MM, (F32, BF16))
            bias = _pair_bias(rel_bias[a])
            o = _band_attn_prompt(q, kv_b, bias)
            o = _band_attn_sample(o, q, kv_b, cache_ak, cache_av,
                                  bias[:, :DEC_SEQ, :BAND_LEFT], bias[:, :DEC_SEQ, BAND_LEFT:BAND_LEFT + DEC_SEQ], a)
            ak_p.append(_tail_rows(kv, BAND_LEFT, k_cols).reshape(BATCH, BAND_LEFT, N_HEADS, HEAD_DIM))
            av_p.append(_tail_rows(kv, BAND_LEFT, v_cols).reshape(BATCH, BAND_LEFT, N_HEADS, HEAD_DIM))
            ak_s.append(kv[M_PROMPT:, k_cols].reshape(DEC_BATCH, DEC_SEQ, N_HEADS, HEAD_DIM))
            av_s.append(kv[M_PROMPT:, v_cols].reshape(DEC_BATCH, DEC_SEQ, N_HEADS, HEAD_DIM))
            if i == 0:
                outs = _mm_ln(o, w_attn_o_b[a], xp, g[0], b[0], tm=TM_LN_PROMPT, n_tiles=M_PROMPT // TM_LN_PROMPT,
                              a_blk0=0, x_blk0=0, o_blk0=0, out_rows=M_ALL, out_dtypes=(F32, BF16))
                x_f, x_b = _mm_ln(o, w_attn_o_b[a], xs, g[0], b[0], tm=M_SAMPLE, n_tiles=1,
                                  a_blk0=M_PROMPT // M_SAMPLE, x_blk0=0, o_blk0=M_PROMPT // M_SAMPLE,
                                  out_rows=M_ALL, out_dtypes=(F32, BF16), into=outs)
            else:
                x_f, x_b = _mm_ln_all(o, w_attn_o_b[a], x_f, g[0], b[0])
        else:
            p = i // 2
            pool_p.append(_tail_rows(x_f, POOL_HIST, slice(None)))
            ext = jnp.concatenate([state_pool[p], x_f[M_PROMPT:].reshape(DEC_BATCH, DEC_SEQ, D)], axis=1)
            pool_s.append(ext[:, DEC_SEQ:])
            ps = pool_scale[p].reshape(1, D)
            y_f, y_b = _pool_prompt(x_f, w_pool_b[p], ps, g[0], b[0])
            x_f, x_b = _pool_sample(y_f, y_b, x_f, state_pad[p], w_pool_b[p], ps, g[0], b[0])
        mem_kv, = _matmul(mem_b, w_mem_kv, i, 0, 2 * D, BATCH * N_MEM, (F32,))
        mk_p.append(mem_kv[:, k_cols].reshape(BATCH, N_MEM, MEM_HEADS, MEM_HEAD_DIM))
        mv_p.append(mem_kv[:, v_cols].reshape(BATCH, N_MEM, MEM_HEADS, MEM_HEAD_DIM))
        q, = _matmul(x_b, w_mem_q, i, 0, D, TM_MM, (BF16,), scale=MEM_HEAD_DIM ** -0.5)
        o = _mem_attn_prompt(q, mem_kv)
        o = _mem_attn_sample(o, q, cache_mk, cache_mv, i)
        x_f, x_b = _mm_ln_all(o, w_mem_o_b[i], x_f, g[1], b[1])
        h = _swiglu_in(x_b, w_ffn_in, i)
        if i < DEPTH - 1:
            x_f, x_b = _mm_ln_all(h, w_ffn_out_b[i], x_f, g[2], b[2])
        else:
            y_p, = _mm_ln(h, w_ffn_out_b[i], x_f, g[2], b[2], tm=TM_LN_PROMPT, n_tiles=M_PROMPT // TM_LN_PROMPT,
                          a_blk0=0, x_blk0=0, o_blk0=0, out_rows=M_PROMPT, out_dtypes=(F32,))
            y_s, = _mm_ln(h, w_ffn_out_b[i], x_f, g[2], b[2], tm=M_SAMPLE, n_tiles=1,
                          a_blk0=M_PROMPT // M_SAMPLE, x_blk0=M_PROMPT // M_SAMPLE, o_blk0=0,
                          out_rows=M_SAMPLE, out_dtypes=(F32,))

    return (y_p.reshape(BATCH, SEQ, D), y_s.reshape(DEC_BATCH, DEC_SEQ, D),
            jnp.stack(ak_p), jnp.stack(av_p), jnp.stack(pool_p), jnp.stack(mk_p), jnp.stack(mv_p),
            jnp.stack(ak_s), jnp.stack(av_s), jnp.stack(pool_s))
```

```python
import functools

import numpy as np
import jax
import jax.numpy as jnp
from jax import lax
from jax.experimental import pallas as pl
from jax.experimental.pallas import tpu as pltpu

F32 = jnp.float32
BF16 = jnp.bfloat16

D_MODEL = 2048
BATCH = 4
SEQ = 4096
DEPTH = 4
DEC_BATCH = 8
DEC_SEQ = 32
CHUNK = 64
BAND_LEFT = 512
BAND_KEYS = BAND_LEFT + CHUNK
N_HEADS = 16
HEAD_DIM = D_MODEL // N_HEADS
REL_CLIP = 2 * CHUNK
POOL_WINDOWS = (2, 4, 8, 16)
POOL_GROUP = D_MODEL // len(POOL_WINDOWS)
POOL_HIST = max(POOL_WINDOWS) - 1
N_MEM = 256
MEM_HEADS = 4
MEM_HEAD_DIM = D_MODEL // MEM_HEADS
D_FF = 5632
ALPHA = (2.0 * DEPTH) ** 0.25
LN_EPS = 1e-5
NEG_BIG = -1e30

M_PROMPT = BATCH * SEQ
M_SAMPLE = DEC_BATCH * DEC_SEQ
M_ALL = M_PROMPT + M_SAMPLE

TM_MM = 1280
TN_MM = 1024
TN_FF = 512
TM_LN = 640
TM_LN_PROMPT = 512
LN_SUB = 4
LN_PIECES = 2
QB = 512
PAIR = 2 * CHUNK
PAIR_KEYS = BAND_LEFT + PAIR
HG = 8
TM_MEM = 1024
TM_POOL = 512
HALO = 16

V7X_VMEM_BYTES = 64 * 1024 * 1024


def _vmem_limit(nbytes):
    return int(min(nbytes + (12 << 20), V7X_VMEM_BYTES - (6 << 20)))


def _params(sem, nbytes):
    return pltpu.CompilerParams(dimension_semantics=sem, vmem_limit_bytes=_vmem_limit(nbytes))


def _mm_kernel(a_ref, w_ref, *rest, scale):
    o_refs, wb_ref = rest[:-1], rest[-1]

    @pl.when(pl.program_id(1) == 0)
    def _():
        wb_ref[...] = w_ref[...].astype(BF16)

    acc = jnp.dot(a_ref[...], wb_ref[...], preferred_element_type=F32)
    if scale != 1.0:
        acc = acc * scale
    for o_ref in o_refs:
        o_ref[...] = acc.astype(o_ref.dtype)


def _matmul(a, w, layer, col0, n_out, tm, out_dtypes, scale=1.0):
    m, k = a.shape
    tn = TN_MM
    assert m % tm == 0 and n_out % tn == 0 and col0 % tn == 0
    cb = col0 // tn
    osz = sum(jnp.dtype(d).itemsize for d in out_dtypes)
    nbytes = 2 * tm * k * 2 + 2 * k * tn * 4 + k * tn * 2 + 2 * tm * tn * osz + tm * tn * 4
    return pl.pallas_call(
        functools.partial(_mm_kernel, scale=scale),
        name="mm",
        grid=(n_out // tn, m // tm),
        in_specs=[pl.BlockSpec((tm, k), lambda j, i: (i, 0)),
                  pl.BlockSpec((None, k, tn), lambda j, i: (layer, 0, j + cb))],
        out_specs=[pl.BlockSpec((tm, tn), lambda j, i: (i, j)) for _ in out_dtypes],
        out_shape=[jax.ShapeDtypeStruct((m, n_out), d) for d in out_dtypes],
        scratch_shapes=[pltpu.VMEM((k, tn), BF16)],
        compiler_params=_params(("arbitrary", "arbitrary"), nbytes),
    )(a, w)


def _swiglu_kernel(a_ref, wg_ref, wu_ref, o_ref, wgb_ref, wub_ref):
    @pl.when(pl.program_id(1) == 0)
    def _():
        wgb_ref[...] = wg_ref[...].astype(BF16)
        wub_ref[...] = wu_ref[...].astype(BF16)

    a = a_ref[...]
    gate = jnp.dot(a, wgb_ref[...], preferred_element_type=F32)
    up = jnp.dot(a, wub_ref[...], preferred_element_type=F32)
    o_ref[...] = (gate / (1.0 + jnp.exp(-gate)) * up).astype(o_ref.dtype)


def _swiglu_in(a, w_in, layer):
    m, k = a.shape
    tm, tn = TM_MM, TN_FF
    nj = D_FF // tn
    nbytes = 2 * tm * k * 2 + 4 * k * tn * 4 + 2 * k * tn * 2 + 2 * tm * tn * 2 + 3 * tm * tn * 4
    return pl.pallas_call(
        _swiglu_kernel,
        name="swiglu_in",
        grid=(nj, m // tm),
        in_specs=[pl.BlockSpec((tm, k), lambda j, i: (i, 0)),
                  pl.BlockSpec((None, k, tn), lambda j, i: (layer, 0, j)),
                  pl.BlockSpec((None, k, tn), lambda j, i: (layer, 0, j + nj))],
        out_specs=pl.BlockSpec((tm, tn), lambda j, i: (i, j)),
        out_shape=jax.ShapeDtypeStruct((m, D_FF), BF16),
        scratch_shapes=[pltpu.VMEM((k, tn), BF16), pltpu.VMEM((k, tn), BF16)],
        compiler_params=_params(("arbitrary", "arbitrary"), nbytes),
    )(a, w_in, w_in)


def _layer_norm(z, g, b):
    mu = jnp.mean(z, axis=-1, keepdims=True)
    zc = z - mu
    var = jnp.mean(zc * zc, axis=-1, keepdims=True)
    return zc * lax.rsqrt(var + LN_EPS) * g + b


def _mm_ln_kernel(*refs, nn, n_alias, n_out):
    a_ref, w_ref, x_ref, g_ref, b_ref = refs[n_alias:n_alias + 5]
    o_refs = refs[n_alias + 5:n_alias + 5 + n_out]
    z_ref = refs[-1]
    n = pl.program_id(1)
    tn = w_ref.shape[1]
    z_ref[n] = ALPHA * x_ref[...] + jnp.dot(a_ref[...], w_ref[...], preferred_element_type=F32)

    @pl.when(n == nn - 1)
    def _():
        d = nn * tn
        tot = z_ref[0].sum(axis=-1, keepdims=True)
        for c in range(1, nn):
            tot = tot + z_ref[c].sum(axis=-1, keepdims=True)
        mu = tot / d
        sq = None
        for c in range(nn):
            zc = z_ref[c] - mu
            part = (zc * zc).sum(axis=-1, keepdims=True)
            sq = part if sq is None else sq + part
        rstd = lax.rsqrt(sq / d + LN_EPS)
        for c in range(nn):
            cs = slice(c * tn, (c + 1) * tn)
            y = (z_ref[c] - mu) * rstd * g_ref[:, cs] + b_ref[:, cs]
            for o_ref in o_refs:
                o_ref[:, cs] = y.astype(o_ref.dtype)


def _mm_ln_resident_kernel(*refs, n_alias):
    a_ref, w_ref, x_ref, g_ref, b_ref = refs[n_alias:n_alias + 5]
    o_refs = refs[n_alias + 5:-1]
    z_ref = refs[-1]
    ts = z_ref.shape[1]
    tc = D_MODEL // LN_PIECES
    tr = ts // LN_PIECES

    def pre_norm(r, c):
        rs, cs = slice(r * ts, (r + 1) * ts), slice(c * tc, (c + 1) * tc)
        z_ref[r, :, cs] = ALPHA * x_ref[rs, cs] + jnp.dot(a_ref[rs, :], w_ref[:, cs],
                                                         preferred_element_type=F32)

    def finish(r, c):
        y = _layer_norm(z_ref[r, c * tr:(c + 1) * tr, :], g_ref[...], b_ref[...])
        for o_ref in o_refs:
            o_ref[r * ts + c * tr:r * ts + (c + 1) * tr, :] = y.astype(o_ref.dtype)

    for r in range(LN_SUB + 1):
        for c in range(LN_PIECES):
            if r < LN_SUB:
                pre_norm(r, c)
            if r > 0:
                finish(r - 1, c)


def _mm_ln_resident(a, w, x, g, b, *, tm, n_tiles, a_blk0, x_blk0, o_blk0, out_rows, out_dtypes, into):
    kdim = a.shape[1]
    n_alias = 0 if into is None else len(into)
    osz = sum(jnp.dtype(d).itemsize for d in out_dtypes)
    nbytes = (2 * tm * kdim * 2 + kdim * D_MODEL * 2 + 2 * tm * D_MODEL * 4 + 2 * tm * D_MODEL * osz
              + 2 * tm * D_MODEL * 4)
    in_specs = [pl.BlockSpec(memory_space=pl.ANY)] * n_alias + [
        pl.BlockSpec((tm, kdim), lambda i: (a_blk0 + i, 0)),
        pl.BlockSpec((kdim, D_MODEL), lambda i: (0, 0), pipeline_mode=pl.Buffered(1)),
        pl.BlockSpec((tm, D_MODEL), lambda i: (x_blk0 + i, 0)),
        pl.BlockSpec((1, D_MODEL), lambda i: (0, 0)),
        pl.BlockSpec((1, D_MODEL), lambda i: (0, 0))]
    return pl.pallas_call(
        functools.partial(_mm_ln_resident_kernel, n_alias=n_alias),
        name="mm_ln_resident",
        grid=(n_tiles,),
        in_specs=in_specs,
        out_specs=[pl.BlockSpec((tm, D_MODEL), lambda i: (o_blk0 + i, 0)) for _ in out_dtypes],
        out_shape=[jax.ShapeDtypeStruct((out_rows, D_MODEL), d) for d in out_dtypes],
        input_output_aliases={j: j for j in range(n_alias)},
        scratch_shapes=[pltpu.VMEM((LN_SUB, tm // LN_SUB, D_MODEL), F32)],
        compiler_params=_params(("arbitrary",), nbytes),
    )(*(into or ()), a, w, x, g, b)


def _mm_ln(a, w, x, g, b, *, tm, n_tiles, a_blk0, x_blk0, o_blk0, out_rows, out_dtypes, into=None):
    kdim = a.shape[1]
    if kdim <= D_MODEL:
        return _mm_ln_resident(a, w, x, g, b, tm=tm, n_tiles=n_tiles, a_blk0=a_blk0, x_blk0=x_blk0,
                               o_blk0=o_blk0, out_rows=out_rows, out_dtypes=out_dtypes, into=into)
    tn = 256
    nn = D_MODEL // tn
    n_alias = 0 if into is None else len(into)
    osz = sum(jnp.dtype(d).itemsize for d in out_dtypes)
    nbytes = (2 * tm * kdim * 2 + 2 * kdim * tn * 2 + 2 * tm * tn * 4 + tm * D_MODEL * 4
              + 2 * tm * D_MODEL * osz + 2 * tm * tn * 4)
    in_specs = [pl.BlockSpec(memory_space=pl.ANY)] * n_alias + [
        pl.BlockSpec((tm, kdim), lambda i, n: (a_blk0 + i, 0)),
        pl.BlockSpec((kdim, tn), lambda i, n: (0, n)),
        pl.BlockSpec((tm, tn), lambda i, n: (x_blk0 + i, n)),
        pl.BlockSpec((1, D_MODEL), lambda i, n: (0, 0)),
        pl.BlockSpec((1, D_MODEL), lambda i, n: (0, 0))]
    return pl.pallas_call(
        functools.partial(_mm_ln_kernel, nn=nn, n_alias=n_alias, n_out=len(out_dtypes)),
        name="mm_ln_stream",
        grid=(n_tiles, nn),
        in_specs=in_specs,
        out_specs=[pl.BlockSpec((tm, D_MODEL), lambda i, n: (o_blk0 + i, 0)) for _ in out_dtypes],
        out_shape=[jax.ShapeDtypeStruct((out_rows, D_MODEL), d) for d in out_dtypes],
        input_output_aliases={j: j for j in range(n_alias)},
        scratch_shapes=[pltpu.VMEM((nn, tm, tn), F32)],
        compiler_params=_params(("arbitrary", "arbitrary"), nbytes),
    )(*(into or ()), a, w, x, g, b)


def _mm_ln_all(a, w, x, g, b):
    return _mm_ln(a, w, x, g, b, tm=TM_LN, n_tiles=M_ALL // TM_LN, a_blk0=0, x_blk0=0, o_blk0=0,
                  out_rows=M_ALL, out_dtypes=(F32, BF16))


def _dot_nt(a, b):
    return lax.dot_general(a, b, (((1,), (1,)), ((), ())), preferred_element_type=F32)


def _band_attn_kernel(q_ref, kl_ref, kc_ref, vl_ref, vc_ref, bias_ref, o_ref, k_sc, v_sc):
    hg = pl.program_id(2)
    k_sc[0:QB, :] = kl_ref[...]
    k_sc[QB:2 * QB, :] = kc_ref[...]
    v_sc[0:QB, :] = vl_ref[...]
    v_sc[QB:2 * QB, :] = vc_ref[...]
    no_left = pl.program_id(1) == 0
    col = lax.broadcasted_iota(jnp.int32, (PAIR, PAIR_KEYS), 1)

    def scores(h, r0):
        sl = slice(h * HEAD_DIM, (h + 1) * HEAD_DIM)
        s = _dot_nt(q_ref[r0:r0 + PAIR, sl], k_sc[r0:r0 + PAIR_KEYS, sl]) + bias_ref[hg * HG + h]
        return jnp.where(jnp.logical_and(no_left, col < QB - r0), NEG_BIG, s)

    def finish(h, r0, s):
        sl = slice(h * HEAD_DIM, (h + 1) * HEAD_DIM)
        m = jnp.max(s, axis=-1, keepdims=True)
        p = jnp.exp(s - m)
        denom = jnp.sum(p, axis=-1, keepdims=True)
        o = jnp.dot(p.astype(BF16), v_sc[r0:r0 + PAIR_KEYS, sl], preferred_element_type=F32)
        o_ref[r0:r0 + PAIR, sl] = (o / denom).astype(o_ref.dtype)

    units = [(h, pr * PAIR) for h in range(HG) for pr in range(QB // PAIR)]
    pending = None
    for h, r0 in units:
        s = scores(h, r0)
        if pending is not None:
            finish(*pending)
        pending = (h, r0, s)
    finish(*pending)


def _band_attn_prompt(q, kv, bias):
    nb = SEQ // QB
    w = HG * HEAD_DIM
    vb = D_MODEL // w
    cur = lambda b, i, g: (b * nb + i, g)
    left = lambda b, i, g: (b * nb + jnp.maximum(i - 1, 0), g)
    cur_v = lambda b, i, g: (b * nb + i, vb + g)
    left_v = lambda b, i, g: (b * nb + jnp.maximum(i - 1, 0), vb + g)
    nbytes = 2 * 6 * QB * w * 2 + 2 * bias.size * 4 + 4 * QB * w * 2 + 8 * PAIR * PAIR_KEYS * 4
    return pl.pallas_call(
        _band_attn_kernel,
        name="band_attn_prompt",
        grid=(BATCH, nb, N_HEADS // HG),
        in_specs=[pl.BlockSpec((QB, w), cur),
                  pl.BlockSpec((QB, w), left),
                  pl.BlockSpec((QB, w), cur),
                  pl.BlockSpec((QB, w), left_v),
                  pl.BlockSpec((QB, w), cur_v),
                  pl.BlockSpec((N_HEADS, PAIR, PAIR_KEYS), lambda b, i, g: (0, 0, 0))],
        out_specs=pl.BlockSpec((QB, w), cur),
        out_shape=jax.ShapeDtypeStruct((M_ALL, D_MODEL), BF16),
        scratch_shapes=[pltpu.VMEM((2 * QB, w), BF16), pltpu.VMEM((2 * QB, w), BF16)],
        compiler_params=_params(("arbitrary", "arbitrary", "arbitrary"), nbytes),
    )(q, kv, kv, kv, kv, bias)


def _band_attn_sample_kernel(o_any, q_ref, kn_ref, vn_ref, ck_ref, cv_ref, bc_ref, bn_ref, o_ref):
    del o_any
    hg = pl.program_id(1)
    for h in range(HG):
        sl = slice(h * HEAD_DIM, (h + 1) * HEAD_DIM)
        q = q_ref[:, sl]
        s_c = _dot_nt(q, ck_ref[:, sl].astype(BF16)) + bc_ref[hg * HG + h]
        s_n = _dot_nt(q, kn_ref[:, sl]) + bn_ref[hg * HG + h]
        m = jnp.maximum(jnp.max(s_c, axis=-1, keepdims=True), jnp.max(s_n, axis=-1, keepdims=True))
        p_c = jnp.exp(s_c - m)
        p_n = jnp.exp(s_n - m)
        denom = jnp.sum(p_c, axis=-1, keepdims=True) + jnp.sum(p_n, axis=-1, keepdims=True)
        o = (jnp.dot(p_c.astype(BF16), cv_ref[:, sl].astype(BF16), preferred_element_type=F32)
             + jnp.dot(p_n.astype(BF16), vn_ref[:, sl], preferred_element_type=F32))
        o_ref[:, sl] = (o / denom).astype(o_ref.dtype)


def _band_attn_sample(o, q, kv, cache_k, cache_v, bias_c, bias_n, a):
    t = DEC_SEQ
    p = bias_c.shape[-1]
    w = HG * HEAD_DIM
    vb = D_MODEL // w
    rb = M_PROMPT // t
    nbytes = (2 * (4 * t * w * 2 + 2 * p * w * 4 + N_HEADS * t * (p + t) * 4) + 2 * p * w * 2 + 8 * t * p * 4)
    return pl.pallas_call(
        _band_attn_sample_kernel,
        name="band_attn_sample",
        grid=(DEC_BATCH, N_HEADS // HG),
        in_specs=[pl.BlockSpec(memory_space=pl.ANY),
                  pl.BlockSpec((t, w), lambda b, g: (rb + b, g)),
                  pl.BlockSpec((t, w), lambda b, g: (rb + b, g)),
                  pl.BlockSpec((t, w), lambda b, g: (rb + b, vb + g)),
                  pl.BlockSpec((p, w), lambda b, g: (a * DEC_BATCH + b, g)),
                  pl.BlockSpec((p, w), lambda b, g: (a * DEC_BATCH + b, g)),
                  pl.BlockSpec((N_HEADS, t, p), lambda b, g: (0, 0, 0)),
                  pl.BlockSpec((N_HEADS, t, t), lambda b, g: (0, 0, 0))],
        out_specs=pl.BlockSpec((t, w), lambda b, g: (rb + b, g)),
        out_shape=jax.ShapeDtypeStruct(o.shape, o.dtype),
        input_output_aliases={0: 0},
        compiler_params=_params(("arbitrary", "arbitrary"), nbytes),
    )(o, q, kv, kv, cache_k, cache_v, bias_c, bias_n)


def _mem_attn_body(q_ref, k_ref, v_ref, o_ref):
    for h in range(MEM_HEADS):
        sl = slice(h * MEM_HEAD_DIM, (h + 1) * MEM_HEAD_DIM)
        s = _dot_nt(q_ref[:, sl], k_ref[:, sl].astype(BF16))
        m = jnp.max(s, axis=-1, keepdims=True)
        p = jnp.exp(s - m)
        denom = jnp.sum(p, axis=-1, keepdims=True)
        o = jnp.dot(p.astype(BF16), v_ref[:, sl].astype(BF16), preferred_element_type=F32)
        o_ref[:, sl] = (o / denom).astype(o_ref.dtype)


def _mem_attn_prompt_kernel(q_ref, k_ref, v_ref, o_ref):
    _mem_attn_body(q_ref, k_ref, v_ref, o_ref)


def _mem_attn_sample_kernel(o_any, q_ref, k_ref, v_ref, o_ref):
    del o_any
    _mem_attn_body(q_ref, k_ref, v_ref, o_ref)


def _mem_attn_prompt(q, mem_kv):
    tm = TM_MEM
    nt = SEQ // tm
    nbytes = 2 * (tm * D_MODEL * 2 + 2 * N_MEM * D_MODEL * 4 + tm * D_MODEL * 2) + 6 * tm * N_MEM * 4
    return pl.pallas_call(
        _mem_attn_prompt_kernel,
        name="mem_attn_prompt",
        grid=(BATCH, nt),
        in_specs=[pl.BlockSpec((tm, D_MODEL), lambda b, i: (b * nt + i, 0)),
                  pl.BlockSpec((N_MEM, D_MODEL), lambda b, i: (b, 0)),
                  pl.BlockSpec((N_MEM, D_MODEL), lambda b, i: (b, 1))],
        out_specs=pl.BlockSpec((tm, D_MODEL), lambda b, i: (b * nt + i, 0)),
        out_shape=jax.ShapeDtypeStruct((M_ALL, D_MODEL), BF16),
        compiler_params=_params(("arbitrary", "arbitrary"), nbytes),
    )(q, mem_kv, mem_kv)


def _mem_attn_sample(o, q, cache_k, cache_v, layer):
    t = DEC_SEQ
    rb = M_PROMPT // t
    nbytes = 2 * (2 * t * D_MODEL * 2 + 2 * N_MEM * D_MODEL * 4) + 6 * t * N_MEM * 4
    return pl.pallas_call(
        _mem_attn_sample_kernel,
        name="mem_attn_sample",
        grid=(DEC_BATCH,),
        in_specs=[pl.BlockSpec(memory_space=pl.ANY),
                  pl.BlockSpec((t, D_MODEL), lambda b: (rb + b, 0)),
                  pl.BlockSpec((N_MEM, D_MODEL), lambda b: (layer * DEC_BATCH + b, 0)),
                  pl.BlockSpec((N_MEM, D_MODEL), lambda b: (layer * DEC_BATCH + b, 0))],
        out_specs=pl.BlockSpec((t, D_MODEL), lambda b: (rb + b, 0)),
        out_shape=jax.ShapeDtypeStruct(o.shape, o.dtype),
        input_output_aliases={0: 0},
        compiler_params=_params(("arbitrary",), nbytes),
    )(o, q, cache_k, cache_v)


def _pool_body(x_ref, halo, pos0, n_hist, wp_ref, ps_ref, g_ref, b_ref, of_ref, ob_ref,
               ext_ref, sa_ref, sb_ref):
    tm = x_ref.shape[0]
    top = 2 * HALO
    rows = tm + top
    zeros = jnp.zeros((HALO, D_MODEL), F32)
    ext_ref[0:HALO, :] = zeros
    ext_ref[HALO:top, :] = halo
    ext_ref[top:rows, :] = x_ref[...]
    sa_ref[0:HALO, :] = zeros[:, :POOL_GROUP]
    sb_ref[0:HALO, :] = zeros[:, :POOL_GROUP]
    seen = (pos0 + n_hist + 1 + lax.broadcasted_iota(jnp.int32, (tm, 1), 0)).astype(F32)
    bufs = (sa_ref, sb_ref)
    for gi, win in enumerate(POOL_WINDOWS):
        cs = slice(gi * POOL_GROUP, (gi + 1) * POOL_GROUP)
        src, d, lvl = None, 1, 0
        while 2 * d < win:
            if src is None:
                val = ext_ref[HALO:rows, cs] + ext_ref[HALO - d:rows - d, cs]
            else:
                val = src[HALO:rows, :] + src[HALO - d:rows - d, :]
            dst = bufs[lvl % 2]
            dst[HALO:rows, :] = val
            src, d, lvl = dst, 2 * d, lvl + 1
        if src is None:
            wsum = ext_ref[top:rows, cs] + ext_ref[top - d:rows - d, cs]
        else:
            wsum = src[top:rows, :] + src[top - d:rows - d, :]
        xg = x_ref[:, cs]
        u = wsum / jnp.minimum(seen, float(win)) - xg
        y = jnp.dot(u.astype(BF16), wp_ref[gi], preferred_element_type=F32) * ps_ref[:, cs]
        of_ref[:, cs] = ALPHA * xg + y
    out = _layer_norm(of_ref[...], g_ref[...], b_ref[...])
    of_ref[...] = out
    ob_ref[...] = out.astype(BF16)


def _pool_prompt_kernel(x_ref, halo_ref, wp_ref, ps_ref, g_ref, b_ref, of_ref, ob_ref,
                        ext_ref, sa_ref, sb_ref):
    it = pl.program_id(0) % (SEQ // TM_POOL)
    halo = jnp.where(it > 0, halo_ref[...], 0.0)
    _pool_body(x_ref, halo, it * TM_POOL, 0, wp_ref, ps_ref, g_ref, b_ref, of_ref, ob_ref,
               ext_ref, sa_ref, sb_ref)


def _pool_sample_kernel(of_any, ob_any, x_ref, halo_ref, wp_ref, ps_ref, g_ref, b_ref, of_ref, ob_ref,
                        ext_ref, sa_ref, sb_ref):
    del of_any, ob_any
    _pool_body(x_ref, halo_ref[...], 0, POOL_HIST, wp_ref, ps_ref, g_ref, b_ref, of_ref, ob_ref,
               ext_ref, sa_ref, sb_ref)


def _pool_scratch(tm):
    return [pltpu.VMEM((tm + 2 * HALO, D_MODEL), F32),
            pltpu.VMEM((tm + 2 * HALO, POOL_GROUP), F32),
            pltpu.VMEM((tm + 2 * HALO, POOL_GROUP), F32)]


def _pool_const_specs():
    return [pl.BlockSpec((len(POOL_WINDOWS), POOL_GROUP, POOL_GROUP), lambda i: (0, 0, 0)),
            pl.BlockSpec((1, D_MODEL), lambda i: (0, 0)),
            pl.BlockSpec((1, D_MODEL), lambda i: (0, 0)),
            pl.BlockSpec((1, D_MODEL), lambda i: (0, 0))]


def _pool_bytes(tm, wp):
    return (2 * tm * D_MODEL * 4 + 2 * HALO * D_MODEL * 4 + 2 * wp.size * 2 + 2 * tm * D_MODEL * 6
            + (tm + 2 * HALO) * (D_MODEL + 2 * POOL_GROUP) * 4 + 2 * tm * D_MODEL * 4)


def _pool_prompt(x, wp, ps, g, b):
    tm = TM_POOL
    hb = tm // HALO
    return pl.pallas_call(
        _pool_prompt_kernel,
        name="pool_prompt",
        grid=(M_PROMPT // tm,),
        in_specs=[pl.BlockSpec((tm, D_MODEL), lambda i: (i, 0)),
                  pl.BlockSpec((HALO, D_MODEL), lambda i: (jnp.maximum(i * hb - 1, 0), 0))]
                 + _pool_const_specs(),
        out_specs=[pl.BlockSpec((tm, D_MODEL), lambda i: (i, 0)),
                   pl.BlockSpec((tm, D_MODEL), lambda i: (i, 0))],
        out_shape=[jax.ShapeDtypeStruct((M_ALL, D_MODEL), F32),
                   jax.ShapeDtypeStruct((M_ALL, D_MODEL), BF16)],
        scratch_shapes=_pool_scratch(tm),
        compiler_params=_params(("arbitrary",), _pool_bytes(tm, wp)),
    )(x, x, wp, ps, g, b)


def _pool_sample(of, ob, x, state, wp, ps, g, b):
    tm = DEC_SEQ
    rb = M_PROMPT // tm
    return pl.pallas_call(
        _pool_sample_kernel,
        name="pool_sample",
        grid=(DEC_BATCH,),
        in_specs=[pl.BlockSpec(memory_space=pl.ANY),
                  pl.BlockSpec(memory_space=pl.ANY),
                  pl.BlockSpec((tm, D_MODEL), lambda i: (rb + i, 0)),
                  pl.BlockSpec((None, HALO, D_MODEL), lambda i: (i, 0, 0))]
                 + _pool_const_specs(),
        out_specs=[pl.BlockSpec((tm, D_MODEL), lambda i: (rb + i, 0)),
                   pl.BlockSpec((tm, D_MODEL), lambda i: (rb + i, 0))],
        out_shape=[jax.ShapeDtypeStruct(of.shape, of.dtype),
                   jax.ShapeDtypeStruct(ob.shape, ob.dtype)],
        input_output_aliases={0: 0, 1: 1},
        scratch_shapes=_pool_scratch(tm),
        compiler_params=_params(("arbitrary",), _pool_bytes(tm, wp)),
    )(of, ob, x, state, wp, ps, g, b)


def _pair_bias(table):
    h = table.shape[0]
    period = PAIR + PAIR_KEYS
    far = table[:, 2 * REL_CLIP:]
    vec = jnp.concatenate([jnp.broadcast_to(far, (h, BAND_LEFT - REL_CLIP)),
                           jnp.flip(table[:, 1:], axis=1),
                           jnp.broadcast_to(far, (h, period - BAND_LEFT - REL_CLIP))], axis=1)
    flat = jnp.tile(vec, (1, PAIR))[:, :PAIR * (period - 1)]
    toe = flat.reshape(h, PAIR, period - 1)[:, :, :PAIR_KEYS]
    r = np.arange(PAIR)[:, None]
    c = np.arange(PAIR_KEYS)[None, :]
    off = c - (r // CHUNK) * CHUNK
    in_band = np.logical_and(off >= 0, off < BAND_KEYS)
    return jnp.where(in_band[None], toe, NEG_BIG).astype(F32)


def _tail_rows(x, n_tail, cols):
    return jnp.stack([x[(bi + 1) * SEQ - n_tail:(bi + 1) * SEQ, cols] for bi in range(BATCH)])


def kernel(x_prompt, x_sample, cache_attn_k, cache_attn_v, state_pool, cache_mem_k, cache_mem_v,
           mem_prompt, w_qkv, w_attn_o, rel_bias, w_pool, pool_scale, w_mem_q, w_mem_kv, w_mem_o,
           w_ffn_in, w_ffn_out, ln_g, ln_b):
    D = D_MODEL
    assert cache_attn_k.shape[2] == BAND_LEFT and DEPTH % 2 == 0
    xp = x_prompt.reshape(M_PROMPT, D)
    xs = x_sample.reshape(M_SAMPLE, D)
    x_f = None
    x_b = jnp.concatenate([xp.astype(BF16), xs.astype(BF16)], axis=0)
    mem_b = mem_prompt.reshape(BATCH * N_MEM, D).astype(BF16)
    w_attn_o_b = w_attn_o.astype(BF16)
    w_mem_o_b = w_mem_o.astype(BF16)
    w_ffn_out_b = w_ffn_out.astype(BF16)
    w_pool_b = w_pool.astype(BF16)
    cache_ak = cache_attn_k.reshape(-1, D)
    cache_av = cache_attn_v.reshape(-1, D)
    cache_mk = cache_mem_k.reshape(-1, D)
    cache_mv = cache_mem_v.reshape(-1, D)
    state_pad = jnp.pad(state_pool, ((0, 0), (0, 0), (HALO - POOL_HIST, 0), (0, 0)))
    k_cols, v_cols = slice(0, D), slice(D, 2 * D)

    ak_p, av_p, pool_p, mk_p, mv_p, ak_s, av_s, pool_s = [], [], [], [], [], [], [], []
    for i in range(DEPTH):
        g = ln_g[i].reshape(3, 1, D)
        b = ln_b[i].reshape(3, 1, D)
        if i % 2 == 0:
            a = i // 2
            q, = _matmul(x_b, w_qkv, a, 0, D, TM_MM, (BF16,), scale=HEAD_DIM ** -0.5)
            kv, kv_b = _matmul(x_b, w_qkv, a, D, 2 * D, TM_MM, (F32, BF16))
            bias = _pair_bias(rel_bias[a])
            o = _band_attn_prompt(q, kv_b, bias)
            o = _band_attn_sample(o, q, kv_b, cache_ak, cache_av,
                                  bias[:, :DEC_SEQ, :BAND_LEFT], bias[:, :DEC_SEQ, BAND_LEFT:BAND_LEFT + DEC_SEQ], a)
            ak_p.append(_tail_rows(kv, BAND_LEFT, k_cols).reshape(BATCH, BAND_LEFT, N_HEADS, HEAD_DIM))
            av_p.append(_tail_rows(kv, BAND_LEFT, v_cols).reshape(BATCH, BAND_LEFT, N_HEADS, HEAD_DIM))
            ak_s.append(kv[M_PROMPT:, k_cols].reshape(DEC_BATCH, DEC_SEQ, N_HEADS, HEAD_DIM))
            av_s.append(kv[M_PROMPT:, v_cols].reshape(DEC_BATCH, DEC_SEQ, N_HEADS, HEAD_DIM))
            if i == 0:
                outs = _mm_ln(o, w_attn_o_b[a], xp, g[0], b[0], tm=TM_LN_PROMPT, n_tiles=M_PROMPT // TM_LN_PROMPT,
                              a_blk0=0, x_blk0=0, o_blk0=0, out_rows=M_ALL, out_dtypes=(F32, BF16))
                x_f, x_b = _mm_ln(o, w_attn_o_b[a], xs, g[0], b[0], tm=M_SAMPLE, n_tiles=1,
                                  a_blk0=M_PROMPT // M_SAMPLE, x_blk0=0, o_blk0=M_PROMPT // M_SAMPLE,
                                  out_rows=M_ALL, out_dtypes=(F32, BF16), into=outs)
            else:
                x_f, x_b = _mm_ln_all(o, w_attn_o_b[a], x_f, g[0], b[0])
        else:
            p = i // 2
            pool_p.append(_tail_rows(x_f, POOL_HIST, slice(None)))
            ext = jnp.concatenate([state_pool[p], x_f[M_PROMPT:].reshape(DEC_BATCH, DEC_SEQ, D)], axis=1)
            pool_s.append(ext[:, DEC_SEQ:])
            ps = pool_scale[p].reshape(1, D)
            y_f, y_b = _pool_prompt(x_f, w_pool_b[p], ps, g[0], b[0])
            x_f, x_b = _pool_sample(y_f, y_b, x_f, state_pad[p], w_pool_b[p], ps, g[0], b[0])
        mem_kv, = _matmul(mem_b, w_mem_kv, i, 0, 2 * D, BATCH * N_MEM, (F32,))
        mk_p.append(mem_kv[:, k_cols].reshape(BATCH, N_MEM, MEM_HEADS, MEM_HEAD_DIM))
        mv_p.append(mem_kv[:, v_cols].reshape(BATCH, N_MEM, MEM_HEADS, MEM_HEAD_DIM))
        q, = _matmul(x_b, w_mem_q, i, 0, D, TM_MM, (BF16,), scale=MEM_HEAD_DIM ** -0.5)
        o = _mem_attn_prompt(q, mem_kv)
        o = _mem_attn_sample(o, q, cache_mk, cache_mv, i)
        x_f, x_b = _mm_ln_all(o, w_mem_o_b[i], x_f, g[1], b[1])
        h = _swiglu_in(x_b, w_ffn_in, i)
        if i < DEPTH - 1:
            x_f, x_b = _mm_ln_all(h, w_ffn_out_b[i], x_f, g[2], b[2])
        else:
            y_p, = _mm_ln(h, w_ffn_out_b[i], x_f, g[2], b[2], tm=TM_LN_PROMPT, n_tiles=M_PROMPT // TM_LN_PROMPT,
                          a_blk0=0, x_blk0=0, o_blk0=0, out_rows=M_PROMPT, out_dtypes=(F32,))
            y_s, = _mm_ln(h, w_ffn_out_b[i], x_f, g[2], b[2], tm=M_SAMPLE, n_tiles=1,
                          a_blk0=M_PROMPT // M_SAMPLE, x_blk0=M_PROMPT // M_SAMPLE, o_blk0=0,
                          out_rows=M_SAMPLE, out_dtypes=(F32,))

    return (y_p.reshape(BATCH, SEQ, D), y_s.reshape(DEC_BATCH, DEC_SEQ, D),
            jnp.stack(ak_p), jnp.stack(av_p), jnp.stack(pool_p), jnp.stack(mk_p), jnp.stack(mv_p),
            jnp.stack(ak_s), jnp.stack(av_s), jnp.stack(pool_s))
```

```python
import functools

import numpy as np
import jax
import jax.numpy as jnp
from jax import lax
from jax.experimental import pallas as pl
from jax.experimental.pallas import tpu as pltpu

F32 = jnp.float32
BF16 = jnp.bfloat16

D_MODEL = 2048
BATCH = 4
SEQ = 4096
DEPTH = 4
DEC_BATCH = 8
DEC_SEQ = 32
CHUNK = 64
BAND_LEFT = 512
BAND_KEYS = BAND_LEFT + CHUNK
N_HEADS = 16
HEAD_DIM = D_MODEL // N_HEADS
REL_CLIP = 2 * CHUNK
POOL_WINDOWS = (2, 4, 8, 16)
POOL_GROUP = D_MODEL // len(POOL_WINDOWS)
POOL_HIST = max(POOL_WINDOWS) - 1
N_MEM = 256
MEM_HEADS = 4
MEM_HEAD_DIM = D_MODEL // MEM_HEADS
D_FF = 5632
ALPHA = (2.0 * DEPTH) ** 0.25
LN_EPS = 1e-5
NEG_BIG = -1e30

M_PROMPT = BATCH * SEQ
M_SAMPLE = DEC_BATCH * DEC_SEQ
M_ALL = M_PROMPT + M_SAMPLE

TM_MM = 1280
TN_MM = 1024
TN_FF = 512
TM_LN = 640
TM_LN_PROMPT = 512
TN_LN_STREAM = 512
LN_SUB = 4
LN_PIECES = 2
QB = 512
PAIR = 2 * CHUNK
PAIR_KEYS = BAND_LEFT + PAIR
HG = 8
TM_MEM = 1024
TM_POOL = 512
HALO = 16

V7X_VMEM_BYTES = 64 * 1024 * 1024


def _vmem_limit(nbytes):
    return int(min(nbytes + (12 << 20), V7X_VMEM_BYTES - (6 << 20)))


def _params(sem, nbytes):
    return pltpu.CompilerParams(dimension_semantics=sem, vmem_limit_bytes=_vmem_limit(nbytes))


def _mm_kernel(a_ref, w_ref, *rest, scale):
    o_refs, wb_ref = rest[:-1], rest[-1]

    @pl.when(pl.program_id(1) == 0)
    def _():
        wb_ref[...] = w_ref[...].astype(BF16)

    acc = jnp.dot(a_ref[...], wb_ref[...], preferred_element_type=F32)
    if scale != 1.0:
        acc = acc * scale
    for o_ref in o_refs:
        o_ref[...] = acc.astype(o_ref.dtype)


def _matmul(a, w, layer, col0, n_out, tm, out_dtypes, scale=1.0):
    m, k = a.shape
    tn = TN_MM
    assert m % tm == 0 and n_out % tn == 0 and col0 % tn == 0
    cb = col0 // tn
    osz = sum(jnp.dtype(d).itemsize for d in out_dtypes)
    nbytes = 2 * tm * k * 2 + 2 * k * tn * 4 + k * tn * 2 + 2 * tm * tn * osz + tm * tn * 4
    return pl.pallas_call(
        functools.partial(_mm_kernel, scale=scale),
        name="mm",
        grid=(n_out // tn, m // tm),
        in_specs=[pl.BlockSpec((tm, k), lambda j, i: (i, 0)),
                  pl.BlockSpec((None, k, tn), lambda j, i: (layer, 0, j + cb))],
        out_specs=[pl.BlockSpec((tm, tn), lambda j, i: (i, j)) for _ in out_dtypes],
        out_shape=[jax.ShapeDtypeStruct((m, n_out), d) for d in out_dtypes],
        scratch_shapes=[pltpu.VMEM((k, tn), BF16)],
        compiler_params=_params(("arbitrary", "arbitrary"), nbytes),
    )(a, w)


def _mem_kv_kernel(a_ref, w_ref, o_ref):
    o_ref[...] = jnp.dot(a_ref[...], w_ref[...].astype(BF16), preferred_element_type=F32)


def _mem_kv_all(mem, w):
    m, k = mem.shape
    n = w.shape[2]
    tn = TN_MM
    nbytes = 2 * m * k * 2 + 2 * k * tn * 4 + k * tn * 2 + 2 * m * tn * 4 + m * tn * 4
    return pl.pallas_call(
        _mem_kv_kernel,
        name="mem_kv",
        grid=(w.shape[0], n // tn),
        in_specs=[pl.BlockSpec((m, k), lambda l, j: (0, 0)),
                  pl.BlockSpec((None, k, tn), lambda l, j: (l, 0, j))],
        out_specs=pl.BlockSpec((None, m, tn), lambda l, j: (l, 0, j)),
        out_shape=jax.ShapeDtypeStruct((w.shape[0], m, n), F32),
        compiler_params=_params(("arbitrary", "arbitrary"), nbytes),
    )(mem, w)


def _swiglu_kernel(a_ref, wg_ref, wu_ref, o_ref, wgb_ref, wub_ref):
    @pl.when(pl.program_id(1) == 0)
    def _():
        wgb_ref[...] = wg_ref[...].astype(BF16)
        wub_ref[...] = wu_ref[...].astype(BF16)

    a = a_ref[...]
    gate = jnp.dot(a, wgb_ref[...], preferred_element_type=F32)
    up = jnp.dot(a, wub_ref[...], preferred_element_type=F32)
    o_ref[...] = (gate / (1.0 + jnp.exp(-gate)) * up).astype(o_ref.dtype)


def _swiglu_in(a, w_in, layer):
    m, k = a.shape
    tm, tn = TM_MM, TN_FF
    nj = D_FF // tn
    nbytes = 2 * tm * k * 2 + 4 * k * tn * 4 + 2 * k * tn * 2 + 2 * tm * tn * 2 + 3 * tm * tn * 4
    return pl.pallas_call(
        _swiglu_kernel,
        name="swiglu_in",
        grid=(nj, m // tm),
        in_specs=[pl.BlockSpec((tm, k), lambda j, i: (i, 0)),
                  pl.BlockSpec((None, k, tn), lambda j, i: (layer, 0, j)),
                  pl.BlockSpec((None, k, tn), lambda j, i: (layer, 0, j + nj))],
        out_specs=pl.BlockSpec((tm, tn), lambda j, i: (i, j)),
        out_shape=jax.ShapeDtypeStruct((m, D_FF), BF16),
        scratch_shapes=[pltpu.VMEM((k, tn), BF16), pltpu.VMEM((k, tn), BF16)],
        compiler_params=_params(("arbitrary", "arbitrary"), nbytes),
    )(a, w_in, w_in)


def _layer_norm(z, g, b):
    mu = jnp.mean(z, axis=-1, keepdims=True)
    zc = z - mu
    var = jnp.mean(zc * zc, axis=-1, keepdims=True)
    return zc * lax.rsqrt(var + LN_EPS) * g + b


def _mm_ln_stream_kernel(*refs, nn, n_alias, n_out):
    a_ref, w_ref, x_ref, g_ref, b_ref = refs[n_alias:n_alias + 5]
    o_refs = refs[n_alias + 5:n_alias + 5 + n_out]
    z_ref = refs[-1]
    n = pl.program_id(1)
    tn = w_ref.shape[1]
    z_ref[n] = ALPHA * x_ref[...] + jnp.dot(a_ref[...], w_ref[...], preferred_element_type=F32)

    @pl.when(n == nn - 1)
    def _():
        d = nn * tn
        tot = z_ref[0].sum(axis=-1, keepdims=True)
        for c in range(1, nn):
            tot = tot + z_ref[c].sum(axis=-1, keepdims=True)
        mu = tot / d
        sq = None
        for c in range(nn):
            zc = z_ref[c] - mu
            part = (zc * zc).sum(axis=-1, keepdims=True)
            sq = part if sq is None else sq + part
        rstd = lax.rsqrt(sq / d + LN_EPS)
        for c in range(nn):
            cs = slice(c * tn, (c + 1) * tn)
            y = (z_ref[c] - mu) * rstd * g_ref[:, cs] + b_ref[:, cs]
            for o_ref in o_refs:
                o_ref[:, cs] = y.astype(o_ref.dtype)


def _mm_ln_resident_kernel(*refs, n_alias):
    a_ref, w_ref, x_ref, g_ref, b_ref = refs[n_alias:n_alias + 5]
    o_refs = refs[n_alias + 5:-1]
    z_ref = refs[-1]
    ts = z_ref.shape[1]
    tc = D_MODEL // LN_PIECES
    tr = ts // LN_PIECES

    def pre_norm(r, c):
        rs, cs = slice(r * ts, (r + 1) * ts), slice(c * tc, (c + 1) * tc)
        z_ref[r, :, cs] = ALPHA * x_ref[rs, cs] + jnp.dot(a_ref[rs, :], w_ref[:, cs],
                                                         preferred_element_type=F32)

    def finish(r, c):
        y = _layer_norm(z_ref[r, c * tr:(c + 1) * tr, :], g_ref[...], b_ref[...])
        for o_ref in o_refs:
            o_ref[r * ts + c * tr:r * ts + (c + 1) * tr, :] = y.astype(o_ref.dtype)

    for r in range(LN_SUB + 1):
        for c in range(LN_PIECES):
            if r < LN_SUB:
                pre_norm(r, c)
            if r > 0:
                finish(r - 1, c)


def _ln_param_specs(gi):
    return [pl.BlockSpec((None, 1, D_MODEL), lambda *_: (gi, 0, 0)),
            pl.BlockSpec((None, 1, D_MODEL), lambda *_: (gi, 0, 0))]


def _mm_ln(a, w, layer, x, gb, gi, *, tm, n_tiles, a_blk0, x_blk0, o_blk0, out_rows, out_dtypes, into=None):
    kdim = a.shape[1]
    n_alias = 0 if into is None else len(into)
    osz = sum(jnp.dtype(d).itemsize for d in out_dtypes)
    any_specs = [pl.BlockSpec(memory_space=pl.ANY)] * n_alias
    common = dict(out_shape=[jax.ShapeDtypeStruct((out_rows, D_MODEL), d) for d in out_dtypes],
                  input_output_aliases={j: j for j in range(n_alias)})
    args = (*(into or ()), a, w, x, *gb)
    if kdim <= D_MODEL:
        nbytes = (2 * tm * kdim * 2 + kdim * D_MODEL * 2 + 2 * tm * D_MODEL * 4 + 2 * tm * D_MODEL * osz
                  + 2 * tm * D_MODEL * 4)
        return pl.pallas_call(
            functools.partial(_mm_ln_resident_kernel, n_alias=n_alias),
            name="mm_ln_resident",
            grid=(n_tiles,),
            in_specs=any_specs + [
                pl.BlockSpec((tm, kdim), lambda i: (a_blk0 + i, 0)),
                pl.BlockSpec((None, kdim, D_MODEL), lambda i: (layer, 0, 0), pipeline_mode=pl.Buffered(1)),
                pl.BlockSpec((tm, D_MODEL), lambda i: (x_blk0 + i, 0))] + _ln_param_specs(gi),
            out_specs=[pl.BlockSpec((tm, D_MODEL), lambda i: (o_blk0 + i, 0)) for _ in out_dtypes],
            scratch_shapes=[pltpu.VMEM((LN_SUB, tm // LN_SUB, D_MODEL), F32)],
            compiler_params=_params(("arbitrary",), nbytes),
            **common,
        )(*args)
    tn = TN_LN_STREAM
    nn = D_MODEL // tn
    nbytes = (2 * tm * kdim * 2 + 2 * kdim * tn * 2 + 2 * tm * tn * 4 + tm * D_MODEL * 4
              + 2 * tm * D_MODEL * osz + tm * tn * 4)
    return pl.pallas_call(
        functools.partial(_mm_ln_stream_kernel, nn=nn, n_alias=n_alias, n_out=len(out_dtypes)),
        name="mm_ln_stream",
        grid=(n_tiles, nn),
        in_specs=any_specs + [
            pl.BlockSpec((tm, kdim), lambda i, n: (a_blk0 + i, 0)),
            pl.BlockSpec((None, kdim, tn), lambda i, n: (layer, 0, n)),
            pl.BlockSpec((tm, tn), lambda i, n: (x_blk0 + i, n))] + _ln_param_specs(gi),
        out_specs=[pl.BlockSpec((tm, D_MODEL), lambda i, n: (o_blk0 + i, 0)) for _ in out_dtypes],
        scratch_shapes=[pltpu.VMEM((nn, tm, tn), F32)],
        compiler_params=_params(("arbitrary", "arbitrary"), nbytes),
        **common,
    )(*args)


def _mm_ln_all(a, w, layer, x, gb, gi):
    return _mm_ln(a, w, layer, x, gb, gi, tm=TM_LN, n_tiles=M_ALL // TM_LN, a_blk0=0, x_blk0=0, o_blk0=0,
                  out_rows=M_ALL, out_dtypes=(F32, BF16))


def _dot_nt(a, b):
    return lax.dot_general(a, b, (((1,), (1,)), ((), ())), preferred_element_type=F32)


def _band_attn_kernel(q_ref, kl_ref, kc_ref, vl_ref, vc_ref, bias_ref, o_ref, k_sc, v_sc):
    hg = pl.program_id(2)
    k_sc[0:QB, :] = kl_ref[...]
    k_sc[QB:2 * QB, :] = kc_ref[...]
    v_sc[0:QB, :] = vl_ref[...]
    v_sc[QB:2 * QB, :] = vc_ref[...]
    no_left = pl.program_id(1) == 0
    col = lax.broadcasted_iota(jnp.int32, (PAIR, PAIR_KEYS), 1)

    def scores(h, r0):
        sl = slice(h * HEAD_DIM, (h + 1) * HEAD_DIM)
        s = _dot_nt(q_ref[r0:r0 + PAIR, sl], k_sc[r0:r0 + PAIR_KEYS, sl]) + bias_ref[hg * HG + h]
        return jnp.where(jnp.logical_and(no_left, col < QB - r0), NEG_BIG, s)

    def finish(h, r0, s):
        sl = slice(h * HEAD_DIM, (h + 1) * HEAD_DIM)
        m = jnp.max(s, axis=-1, keepdims=True)
        p = jnp.exp(s - m)
        denom = jnp.sum(p, axis=-1, keepdims=True)
        o = jnp.dot(p.astype(BF16), v_sc[r0:r0 + PAIR_KEYS, sl], preferred_element_type=F32)
        o_ref[r0:r0 + PAIR, sl] = (o / denom).astype(o_ref.dtype)

    units = [(h, pr * PAIR) for h in range(HG) for pr in range(QB // PAIR)]
    pending = None
    for h, r0 in units:
        s = scores(h, r0)
        if pending is not None:
            finish(*pending)
        pending = (h, r0, s)
    finish(*pending)


def _band_attn_prompt(q, kv, bias, a):
    nb = SEQ // QB
    w = HG * HEAD_DIM
    vb = D_MODEL // w
    cur = lambda b, i, g: (b * nb + i, g)
    left = lambda b, i, g: (b * nb + jnp.maximum(i - 1, 0), g)
    cur_v = lambda b, i, g: (b * nb + i, vb + g)
    left_v = lambda b, i, g: (b * nb + jnp.maximum(i - 1, 0), vb + g)
    nbytes = (2 * 6 * QB * w * 2 + 2 * N_HEADS * PAIR * PAIR_KEYS * 4 + 4 * QB * w * 2
              + 8 * PAIR * PAIR_KEYS * 4)
    return pl.pallas_call(
        _band_attn_kernel,
        name="band_attn_prompt",
        grid=(BATCH, nb, N_HEADS // HG),
        in_specs=[pl.BlockSpec((QB, w), cur),
                  pl.BlockSpec((QB, w), left),
                  pl.BlockSpec((QB, w), cur),
                  pl.BlockSpec((QB, w), left_v),
                  pl.BlockSpec((QB, w), cur_v),
                  pl.BlockSpec((None, N_HEADS, PAIR, PAIR_KEYS), lambda b, i, g: (a, 0, 0, 0))],
        out_specs=pl.BlockSpec((QB, w), cur),
        out_shape=jax.ShapeDtypeStruct((M_ALL, D_MODEL), BF16),
        scratch_shapes=[pltpu.VMEM((2 * QB, w), BF16), pltpu.VMEM((2 * QB, w), BF16)],
        compiler_params=_params(("arbitrary", "arbitrary", "arbitrary"), nbytes),
    )(q, kv, kv, kv, kv, bias)


def _band_attn_sample_kernel(o_any, q_ref, kn_ref, vn_ref, ck_ref, cv_ref, bias_ref, o_ref):
    del o_any
    past = ck_ref.shape[0] // N_HEADS
    for h in range(N_HEADS):
        sl = slice(h * HEAD_DIM, (h + 1) * HEAD_DIM)
        rows = pl.ds(h, past, stride=N_HEADS)
        q = q_ref[:, sl]
        s_c = _dot_nt(q, ck_ref[rows, :].astype(BF16)) + bias_ref[h, 0:DEC_SEQ, 0:past]
        s_n = _dot_nt(q, kn_ref[:, sl]) + bias_ref[h, 0:DEC_SEQ, past:past + DEC_SEQ]
        m = jnp.maximum(jnp.max(s_c, axis=-1, keepdims=True), jnp.max(s_n, axis=-1, keepdims=True))
        p_c = jnp.exp(s_c - m)
        p_n = jnp.exp(s_n - m)
        denom = jnp.sum(p_c, axis=-1, keepdims=True) + jnp.sum(p_n, axis=-1, keepdims=True)
        o = (jnp.dot(p_c.astype(BF16), cv_ref[rows, :].astype(BF16), preferred_element_type=F32)
             + jnp.dot(p_n.astype(BF16), vn_ref[:, sl], preferred_element_type=F32))
        o_ref[:, sl] = (o / denom).astype(o_ref.dtype)


def _band_attn_sample(o, q, kv, cache_k, cache_v, bias, a):
    t = DEC_SEQ
    rb = M_PROMPT // t
    cr = BAND_LEFT * N_HEADS
    nbytes = (2 * (4 * t * D_MODEL * 2 + 2 * cr * HEAD_DIM * 4 + N_HEADS * PAIR * PAIR_KEYS * 4)
              + 4 * BAND_LEFT * HEAD_DIM * 4)
    return pl.pallas_call(
        _band_attn_sample_kernel,
        name="band_attn_sample",
        grid=(DEC_BATCH,),
        in_specs=[pl.BlockSpec(memory_space=pl.ANY),
                  pl.BlockSpec((t, D_MODEL), lambda b: (rb + b, 0)),
                  pl.BlockSpec((t, D_MODEL), lambda b: (rb + b, 0)),
                  pl.BlockSpec((t, D_MODEL), lambda b: (rb + b, 1)),
                  pl.BlockSpec((cr, HEAD_DIM), lambda b: (a * DEC_BATCH + b, 0)),
                  pl.BlockSpec((cr, HEAD_DIM), lambda b: (a * DEC_BATCH + b, 0)),
                  pl.BlockSpec((None, N_HEADS, PAIR, PAIR_KEYS), lambda b: (a, 0, 0, 0))],
        out_specs=pl.BlockSpec((t, D_MODEL), lambda b: (rb + b, 0)),
        out_shape=jax.ShapeDtypeStruct(o.shape, o.dtype),
        input_output_aliases={0: 0},
        compiler_params=_params(("arbitrary",), nbytes),
    )(o, q, kv, kv, cache_k, cache_v, bias)


def _mem_attn_body(q_ref, k_ref, v_ref, o_ref):
    for h in range(MEM_HEADS):
        sl = slice(h * MEM_HEAD_DIM, (h + 1) * MEM_HEAD_DIM)
        s = _dot_nt(q_ref[:, sl], k_ref[:, sl].astype(BF16))
        m = jnp.max(s, axis=-1, keepdims=True)
        p = jnp.exp(s - m)
        denom = jnp.sum(p, axis=-1, keepdims=True)
        o = jnp.dot(p.astype(BF16), v_ref[:, sl].astype(BF16), preferred_element_type=F32)
        o_ref[:, sl] = (o / denom).astype(o_ref.dtype)


def _mem_attn_prompt_kernel(q_ref, k_ref, v_ref, o_ref):
    _mem_attn_body(q_ref, k_ref, v_ref, o_ref)


def _mem_attn_sample_kernel(o_any, q_ref, k_ref, v_ref, o_ref):
    del o_any
    _mem_attn_body(q_ref, k_ref, v_ref, o_ref)


def _mem_attn_prompt(q, mem_kv, layer):
    tm = TM_MEM
    nt = SEQ // tm
    nbytes = 2 * (tm * D_MODEL * 2 + 2 * N_MEM * D_MODEL * 4 + tm * D_MODEL * 2) + 6 * tm * N_MEM * 4
    return pl.pallas_call(
        _mem_attn_prompt_kernel,
        name="mem_attn_prompt",
        grid=(BATCH, nt),
        in_specs=[pl.BlockSpec((tm, D_MODEL), lambda b, i: (b * nt + i, 0)),
                  pl.BlockSpec((None, N_MEM, D_MODEL), lambda b, i: (layer, b, 0)),
                  pl.BlockSpec((None, N_MEM, D_MODEL), lambda b, i: (layer, b, 1))],
        out_specs=pl.BlockSpec((tm, D_MODEL), lambda b, i: (b * nt + i, 0)),
        out_shape=jax.ShapeDtypeStruct((M_ALL, D_MODEL), BF16),
        compiler_params=_params(("arbitrary", "arbitrary"), nbytes),
    )(q, mem_kv, mem_kv)


def _mem_attn_sample(o, q, cache_k, cache_v, layer):
    t = DEC_SEQ
    rb = M_PROMPT // t
    nbytes = 2 * (2 * t * D_MODEL * 2 + 2 * N_MEM * D_MODEL * 4) + 6 * t * N_MEM * 4
    return pl.pallas_call(
        _mem_attn_sample_kernel,
        name="mem_attn_sample",
        grid=(DEC_BATCH,),
        in_specs=[pl.BlockSpec(memory_space=pl.ANY),
                  pl.BlockSpec((t, D_MODEL), lambda b: (rb + b, 0)),
                  pl.BlockSpec((N_MEM, D_MODEL), lambda b: (layer * DEC_BATCH + b, 0)),
                  pl.BlockSpec((N_MEM, D_MODEL), lambda b: (layer * DEC_BATCH + b, 0))],
        out_specs=pl.BlockSpec((t, D_MODEL), lambda b: (rb + b, 0)),
        out_shape=jax.ShapeDtypeStruct(o.shape, o.dtype),
        input_output_aliases={0: 0},
        compiler_params=_params(("arbitrary",), nbytes),
    )(o, q, cache_k, cache_v)


def _pool_body(x_ref, halo, pos0, n_hist, wp_ref, ps_ref, g_ref, b_ref, of_ref, ob_ref,
               ext_ref, sa_ref, sb_ref):
    tm = x_ref.shape[0]
    top = 2 * HALO
    rows = tm + top
    zeros = jnp.zeros((HALO, D_MODEL), F32)
    ext_ref[0:HALO, :] = zeros
    ext_ref[HALO:top, :] = halo
    ext_ref[top:rows, :] = x_ref[...]
    sa_ref[0:HALO, :] = zeros[:, :POOL_GROUP]
    sb_ref[0:HALO, :] = zeros[:, :POOL_GROUP]
    seen = (pos0 + n_hist + 1 + lax.broadcasted_iota(jnp.int32, (tm, 1), 0)).astype(F32)
    bufs = (sa_ref, sb_ref)
    for gi, win in enumerate(POOL_WINDOWS):
        cs = slice(gi * POOL_GROUP, (gi + 1) * POOL_GROUP)
        src, d, lvl = None, 1, 0
        while 2 * d < win:
            if src is None:
                val = ext_ref[HALO:rows, cs] + ext_ref[HALO - d:rows - d, cs]
            else:
                val = src[HALO:rows, :] + src[HALO - d:rows - d, :]
            dst = bufs[lvl % 2]
            dst[HALO:rows, :] = val
            src, d, lvl = dst, 2 * d, lvl + 1
        if src is None:
            wsum = ext_ref[top:rows, cs] + ext_ref[top - d:rows - d, cs]
        else:
            wsum = src[top:rows, :] + src[top - d:rows - d, :]
        xg = x_ref[:, cs]
        u = wsum / jnp.minimum(seen, float(win)) - xg
        y = jnp.dot(u.astype(BF16), wp_ref[gi], preferred_element_type=F32) * ps_ref[:, cs]
        of_ref[:, cs] = ALPHA * xg + y
    out = _layer_norm(of_ref[...], g_ref[...], b_ref[...])
    of_ref[...] = out
    ob_ref[...] = out.astype(BF16)


def _pool_prompt_kernel(x_ref, halo_ref, wp_ref, ps_ref, g_ref, b_ref, of_ref, ob_ref,
                        ext_ref, sa_ref, sb_ref):
    it = pl.program_id(0) % (SEQ // TM_POOL)
    halo = jnp.where(it > 0, halo_ref[...], 0.0)
    _pool_body(x_ref, halo, it * TM_POOL, 0, wp_ref, ps_ref, g_ref, b_ref, of_ref, ob_ref,
               ext_ref, sa_ref, sb_ref)


def _pool_sample_kernel(of_any, ob_any, x_ref, halo_ref, wp_ref, ps_ref, g_ref, b_ref, of_ref, ob_ref,
                        ext_ref, sa_ref, sb_ref):
    del of_any, ob_any
    _pool_body(x_ref, halo_ref[...], 0, POOL_HIST, wp_ref, ps_ref, g_ref, b_ref, of_ref, ob_ref,
               ext_ref, sa_ref, sb_ref)


def _pool_scratch(tm):
    return [pltpu.VMEM((tm + 2 * HALO, D_MODEL), F32),
            pltpu.VMEM((tm + 2 * HALO, POOL_GROUP), F32),
            pltpu.VMEM((tm + 2 * HALO, POOL_GROUP), F32)]


def _pool_const_specs(p, gi):
    return [pl.BlockSpec((None, len(POOL_WINDOWS), POOL_GROUP, POOL_GROUP), lambda i: (p, 0, 0, 0)),
            pl.BlockSpec((None, 1, D_MODEL), lambda i: (p, 0, 0))] + _ln_param_specs(gi)


def _pool_bytes(tm):
    return (2 * tm * D_MODEL * 4 + 2 * HALO * D_MODEL * 4 + 2 * D_MODEL * POOL_GROUP * 2 + 2 * tm * D_MODEL * 6
            + (tm + 2 * HALO) * (D_MODEL + 2 * POOL_GROUP) * 4 + 2 * tm * D_MODEL * 4)


def _pool_prompt(x, wp, ps, p, gb, gi):
    tm = TM_POOL
    hb = tm // HALO
    return pl.pallas_call(
        _pool_prompt_kernel,
        name="pool_prompt",
        grid=(M_PROMPT // tm,),
        in_specs=[pl.BlockSpec((tm, D_MODEL), lambda i: (i, 0)),
                  pl.BlockSpec((HALO, D_MODEL), lambda i: (jnp.maximum(i * hb - 1, 0), 0))]
                 + _pool_const_specs(p, gi),
        out_specs=[pl.BlockSpec((tm, D_MODEL), lambda i: (i, 0)),
                   pl.BlockSpec((tm, D_MODEL), lambda i: (i, 0))],
        out_shape=[jax.ShapeDtypeStruct((M_ALL, D_MODEL), F32),
                   jax.ShapeDtypeStruct((M_ALL, D_MODEL), BF16)],
        scratch_shapes=_pool_scratch(tm),
        compiler_params=_params(("arbitrary",), _pool_bytes(tm)),
    )(x, x, wp, ps, *gb)


def _pool_sample(of, ob, x, state, wp, ps, p, gb, gi):
    tm = DEC_SEQ
    rb = M_PROMPT // tm
    return pl.pallas_call(
        _pool_sample_kernel,
        name="pool_sample",
        grid=(DEC_BATCH,),
        in_specs=[pl.BlockSpec(memory_space=pl.ANY),
                  pl.BlockSpec(memory_space=pl.ANY),
                  pl.BlockSpec((tm, D_MODEL), lambda i: (rb + i, 0)),
                  pl.BlockSpec((None, HALO, D_MODEL), lambda i: (p * DEC_BATCH + i, 0, 0))]
                 + _pool_const_specs(p, gi),
        out_specs=[pl.BlockSpec((tm, D_MODEL), lambda i: (rb + i, 0)),
                   pl.BlockSpec((tm, D_MODEL), lambda i: (rb + i, 0))],
        out_shape=[jax.ShapeDtypeStruct(of.shape, of.dtype),
                   jax.ShapeDtypeStruct(ob.shape, ob.dtype)],
        input_output_aliases={0: 0, 1: 1},
        scratch_shapes=_pool_scratch(tm),
        compiler_params=_params(("arbitrary",), _pool_bytes(tm)),
    )(of, ob, x, state, wp, ps, *gb)


def _pair_bias(table):
    lead = table.shape[:-1]
    period = PAIR + PAIR_KEYS
    far = table[..., 2 * REL_CLIP:]
    vec = jnp.concatenate([jnp.broadcast_to(far, lead + (BAND_LEFT - REL_CLIP,)),
                           jnp.flip(table[..., 1:], axis=-1),
                           jnp.broadcast_to(far, lead + (period - BAND_LEFT - REL_CLIP,))], axis=-1)
    flat = jnp.tile(vec, (1,) * len(lead) + (PAIR,))[..., :PAIR * (period - 1)]
    toe = flat.reshape(lead + (PAIR, period - 1))[..., :PAIR_KEYS]
    r = np.arange(PAIR)[:, None]
    c = np.arange(PAIR_KEYS)[None, :]
    off = c - (r // CHUNK) * CHUNK
    in_band = np.logical_and(off >= 0, off < BAND_KEYS)
    return jnp.where(in_band, toe, NEG_BIG).astype(F32)


def _tail_rows(x, n_tail, cols):
    return jnp.stack([x[(bi + 1) * SEQ - n_tail:(bi + 1) * SEQ, cols] for bi in range(BATCH)])


def kernel(x_prompt, x_sample, cache_attn_k, cache_attn_v, state_pool, cache_mem_k, cache_mem_v,
           mem_prompt, w_qkv, w_attn_o, rel_bias, w_pool, pool_scale, w_mem_q, w_mem_kv, w_mem_o,
           w_ffn_in, w_ffn_out, ln_g, ln_b):
    D = D_MODEL
    assert cache_attn_k.shape[2] == BAND_LEFT and DEPTH % 2 == 0
    xp = x_prompt.reshape(M_PROMPT, D)
    xs = x_sample.reshape(M_SAMPLE, D)
    x_f = None
    x_b = jnp.concatenate([xp.astype(BF16), xs.astype(BF16)], axis=0)
    mem_b = mem_prompt.reshape(BATCH * N_MEM, D).astype(BF16)
    w_attn_o_b = w_attn_o.astype(BF16)
    w_mem_o_b = w_mem_o.astype(BF16)
    w_ffn_out_b = w_ffn_out.astype(BF16)
    w_pool_b = w_pool.astype(BF16)
    gb = (ln_g.reshape(DEPTH * 3, 1, D), ln_b.reshape(DEPTH * 3, 1, D))
    pscale = pool_scale.reshape(-1, 1, D)
    cache_ak = cache_attn_k.reshape(-1, HEAD_DIM)
    cache_av = cache_attn_v.reshape(-1, HEAD_DIM)
    cache_mk = cache_mem_k.reshape(-1, D)
    cache_mv = cache_mem_v.reshape(-1, D)
    state_pad = jnp.pad(state_pool, ((0, 0), (0, 0), (HALO - POOL_HIST, 0), (0, 0))).reshape(-1, HALO, D)
    bias = _pair_bias(rel_bias)
    mem_kv = _mem_kv_all(mem_b, w_mem_kv)
    k_cols, v_cols = slice(0, D), slice(D, 2 * D)

    ak_p, av_p, pool_p, ak_s, av_s, pool_s = [], [], [], [], [], []
    for i in range(DEPTH):
        if i % 2 == 0:
            a = i // 2
            q, = _matmul(x_b, w_qkv, a, 0, D, TM_MM, (BF16,), scale=HEAD_DIM ** -0.5)
            kv, kv_b = _matmul(x_b, w_qkv, a, D, 2 * D, TM_MM, (F32, BF16))
            o = _band_attn_prompt(q, kv_b, bias, a)
            o = _band_attn_sample(o, q, kv_b, cache_ak, cache_av, bias, a)
            ak_p.append(_tail_rows(kv, BAND_LEFT, k_cols).reshape(BATCH, BAND_LEFT, N_HEADS, HEAD_DIM))
            av_p.append(_tail_rows(kv, BAND_LEFT, v_cols).reshape(BATCH, BAND_LEFT, N_HEADS, HEAD_DIM))
            ak_s.append(kv[M_PROMPT:, k_cols].reshape(DEC_BATCH, DEC_SEQ, N_HEADS, HEAD_DIM))
            av_s.append(kv[M_PROMPT:, v_cols].reshape(DEC_BATCH, DEC_SEQ, N_HEADS, HEAD_DIM))
            if i == 0:
                outs = _mm_ln(o, w_attn_o_b, a, xp, gb, 0, tm=TM_LN_PROMPT, n_tiles=M_PROMPT // TM_LN_PROMPT,
                              a_blk0=0, x_blk0=0, o_blk0=0, out_rows=M_ALL, out_dtypes=(F32, BF16))
                x_f, x_b = _mm_ln(o, w_attn_o_b, a, xs, gb, 0, tm=M_SAMPLE, n_tiles=1,
                                  a_blk0=M_PROMPT // M_SAMPLE, x_blk0=0, o_blk0=M_PROMPT // M_SAMPLE,
                                  out_rows=M_ALL, out_dtypes=(F32, BF16), into=outs)
            else:
                x_f, x_b = _mm_ln_all(o, w_attn_o_b, a, x_f, gb, 3 * i)
        else:
            p = i // 2
            pool_p.append(_tail_rows(x_f, POOL_HIST, slice(None)))
            ext = jnp.concatenate([state_pool[p], x_f[M_PROMPT:].reshape(DEC_BATCH, DEC_SEQ, D)], axis=1)
            pool_s.append(ext[:, DEC_SEQ:])
            y_f, y_b = _pool_prompt(x_f, w_pool_b, pscale, p, gb, 3 * i)
            x_f, x_b = _pool_sample(y_f, y_b, x_f, state_pad, w_pool_b, pscale, p, gb, 3 * i)
        q, = _matmul(x_b, w_mem_q, i, 0, D, TM_MM, (BF16,), scale=MEM_HEAD_DIM ** -0.5)
        o = _mem_attn_prompt(q, mem_kv, i)
        o = _mem_attn_sample(o, q, cache_mk, cache_mv, i)
        x_f, x_b = _mm_ln_all(o, w_mem_o_b, i, x_f, gb, 3 * i + 1)
        h = _swiglu_in(x_b, w_ffn_in, i)
        if i < DEPTH - 1:
            x_f, x_b = _mm_ln_all(h, w_ffn_out_b, i, x_f, gb, 3 * i + 2)
        else:
            y_p, = _mm_ln(h, w_ffn_out_b, i, x_f, gb, 3 * i + 2, tm=TM_LN_PROMPT,
                          n_tiles=M_PROMPT // TM_LN_PROMPT, a_blk0=0, x_blk0=0, o_blk0=0,
                          out_rows=M_PROMPT, out_dtypes=(F32,))
            y_s, = _mm_ln(h, w_ffn_out_b, i, x_f, gb, 3 * i + 2, tm=M_SAMPLE, n_tiles=1,
                          a_blk0=M_PROMPT // M_SAMPLE, x_blk0=M_PROMPT // M_SAMPLE, o_blk0=0,
                          out_rows=M_SAMPLE, out_dtypes=(F32,))

    mem_shape = (DEPTH, BATCH, N_MEM, MEM_HEADS, MEM_HEAD_DIM)
    return (y_p.reshape(BATCH, SEQ, D), y_s.reshape(DEC_BATCH, DEC_SEQ, D),
            jnp.stack(ak_p), jnp.stack(av_p), jnp.stack(pool_p),
            mem_kv[:, :, k_cols].reshape(mem_shape), mem_kv[:, :, v_cols].reshape(mem_shape),
            jnp.stack(ak_s), jnp.stack(av_s), jnp.stack(pool_s))
```

```python
import functools

import numpy as np
import jax
import jax.numpy as jnp
from jax import lax
from jax.experimental import pallas as pl
from jax.experimental.pallas import tpu as pltpu

F32 = jnp.float32
BF16 = jnp.bfloat16

D_MODEL = 2048
BATCH = 4
SEQ = 4096
DEPTH = 4
DEC_BATCH = 8
DEC_SEQ = 32
CHUNK = 64
BAND_LEFT = 512
BAND_KEYS = BAND_LEFT + CHUNK
N_HEADS = 16
HEAD_DIM = D_MODEL // N_HEADS
REL_CLIP = 2 * CHUNK
POOL_WINDOWS = (2, 4, 8, 16)
POOL_GROUP = D_MODEL // len(POOL_WINDOWS)
POOL_HIST = max(POOL_WINDOWS) - 1
N_MEM = 256
MEM_HEADS = 4
MEM_HEAD_DIM = D_MODEL // MEM_HEADS
D_FF = 5632
ALPHA = (2.0 * DEPTH) ** 0.25
LN_EPS = 1e-5
NEG_BIG = -1e30

M_PROMPT = BATCH * SEQ
M_SAMPLE = DEC_BATCH * DEC_SEQ
M_ALL = M_PROMPT + M_SAMPLE

TM_MM = 1664
TM_KV = 1280
TN_MM = 1024
TN_FF = 512
TM_LN = 640
TM_LN_PROMPT = 512
TM_LN_WIDE = 256
LN_SLAB = 128
LN_PIECES = 2
QB = 512
PAIR = 2 * CHUNK
PAIR_KEYS = BAND_LEFT + PAIR
HG = 8
TM_MEM = 1024
TM_POOL = 512
HALO = 16

V7X_VMEM_BYTES = 64 * 1024 * 1024


def _vmem_limit(nbytes):
    return int(min(nbytes + (12 << 20), V7X_VMEM_BYTES - (6 << 20)))


def _params(sem, nbytes):
    return pltpu.CompilerParams(dimension_semantics=sem, vmem_limit_bytes=_vmem_limit(nbytes))


def _mm_kernel(a_ref, w_ref, *rest, scale):
    o_refs, wb_ref = rest[:-1], rest[-1]

    @pl.when(pl.program_id(1) == 0)
    def _():
        wb_ref[...] = w_ref[...].astype(BF16)

    acc = jnp.dot(a_ref[...], wb_ref[...], preferred_element_type=F32)
    if scale != 1.0:
        acc = acc * scale
    for o_ref in o_refs:
        o_ref[...] = acc.astype(o_ref.dtype)


def _matmul(a, w, layer, col0, n_out, tm, out_dtypes, scale=1.0):
    m, k = a.shape
    tn = TN_MM
    assert m % tm == 0 and n_out % tn == 0 and col0 % tn == 0
    cb = col0 // tn
    osz = sum(jnp.dtype(d).itemsize for d in out_dtypes)
    nbytes = 2 * tm * k * 2 + 2 * k * tn * 4 + k * tn * 2 + 2 * tm * tn * osz + tm * tn * 4
    return pl.pallas_call(
        functools.partial(_mm_kernel, scale=scale),
        name="mm",
        grid=(n_out // tn, m // tm),
        in_specs=[pl.BlockSpec((tm, k), lambda j, i: (i, 0)),
                  pl.BlockSpec((None, k, tn), lambda j, i: (layer, 0, j + cb))],
        out_specs=[pl.BlockSpec((tm, tn), lambda j, i: (i, j)) for _ in out_dtypes],
        out_shape=[jax.ShapeDtypeStruct((m, n_out), d) for d in out_dtypes],
        scratch_shapes=[pltpu.VMEM((k, tn), BF16)],
        compiler_params=_params(("arbitrary", "arbitrary"), nbytes),
    )(a, w)


def _mem_kv_kernel(a_ref, w_ref, o_ref):
    o_ref[...] = jnp.dot(a_ref[...], w_ref[...].astype(BF16), preferred_element_type=F32)


def _mem_kv_all(mem, w):
    m, k = mem.shape
    n = w.shape[2]
    tn = TN_MM
    nbytes = 2 * m * k * 2 + 2 * k * tn * 4 + k * tn * 2 + 2 * m * tn * 4 + m * tn * 4
    return pl.pallas_call(
        _mem_kv_kernel,
        name="mem_kv",
        grid=(w.shape[0], n // tn),
        in_specs=[pl.BlockSpec((m, k), lambda l, j: (0, 0)),
                  pl.BlockSpec((None, k, tn), lambda l, j: (l, 0, j))],
        out_specs=pl.BlockSpec((None, m, tn), lambda l, j: (l, 0, j)),
        out_shape=jax.ShapeDtypeStruct((w.shape[0], m, n), F32),
        compiler_params=_params(("arbitrary", "arbitrary"), nbytes),
    )(mem, w)


def _swiglu_kernel(a_ref, wg_ref, wu_ref, o_ref, wgb_ref, wub_ref):
    @pl.when(pl.program_id(1) == 0)
    def _():
        wgb_ref[...] = wg_ref[...].astype(BF16)
        wub_ref[...] = wu_ref[...].astype(BF16)

    a = a_ref[...]
    gate = jnp.dot(a, wgb_ref[...], preferred_element_type=F32)
    up = jnp.dot(a, wub_ref[...], preferred_element_type=F32)
    half = 0.5 * gate
    o_ref[...] = ((half + half * jnp.tanh(half)) * up).astype(o_ref.dtype)


def _swiglu_in(a, w_in, layer):
    m, k = a.shape
    tm, tn = TM_MM, TN_FF
    nj = D_FF // tn
    nbytes = 2 * tm * k * 2 + 4 * k * tn * 4 + 2 * k * tn * 2 + 2 * tm * tn * 2 + 3 * tm * tn * 4
    return pl.pallas_call(
        _swiglu_kernel,
        name="swiglu_in",
        grid=(nj, m // tm),
        in_specs=[pl.BlockSpec((tm, k), lambda j, i: (i, 0)),
                  pl.BlockSpec((None, k, tn), lambda j, i: (layer, 0, j)),
                  pl.BlockSpec((None, k, tn), lambda j, i: (layer, 0, j + nj))],
        out_specs=pl.BlockSpec((tm, tn), lambda j, i: (i, j)),
        out_shape=jax.ShapeDtypeStruct((m, D_FF), BF16),
        scratch_shapes=[pltpu.VMEM((k, tn), BF16), pltpu.VMEM((k, tn), BF16)],
        compiler_params=_params(("arbitrary", "arbitrary"), nbytes),
    )(a, w_in, w_in)


def _layer_norm(z, g, b):
    mu = jnp.mean(z, axis=-1, keepdims=True)
    zc = z - mu
    var = jnp.mean(zc * zc, axis=-1, keepdims=True)
    return zc * lax.rsqrt(var + LN_EPS) * g + b


def _mm_ln_kernel(*refs, n_alias, n_sub):
    a_ref, w_ref, x_ref, g_ref, b_ref = refs[n_alias:n_alias + 5]
    o_refs = refs[n_alias + 5:-1]
    z_ref = refs[-1]
    ts = z_ref.shape[1]
    tc = D_MODEL // LN_PIECES
    tr = ts // LN_PIECES

    def pre_norm(r, c):
        rs, cs = slice(r * ts, (r + 1) * ts), slice(c * tc, (c + 1) * tc)
        z_ref[r, :, cs] = ALPHA * x_ref[rs, cs] + jnp.dot(a_ref[rs, :], w_ref[:, cs],
                                                         preferred_element_type=F32)

    def finish(r, c):
        y = _layer_norm(z_ref[r, c * tr:(c + 1) * tr, :], g_ref[...], b_ref[...])
        for o_ref in o_refs:
            o_ref[r * ts + c * tr:r * ts + (c + 1) * tr, :] = y.astype(o_ref.dtype)

    for r in range(n_sub + 1):
        for c in range(LN_PIECES):
            if r < n_sub:
                pre_norm(r, c)
            if r > 0:
                finish(r - 1, c)


def _ln_param_specs(gi):
    return [pl.BlockSpec((None, 1, D_MODEL), lambda *_: (gi, 0, 0)),
            pl.BlockSpec((None, 1, D_MODEL), lambda *_: (gi, 0, 0))]


def _mm_ln(a, w, layer, x, gb, gi, *, tm, n_tiles, a_blk0, x_blk0, o_blk0, out_rows, out_dtypes, into=None):
    kdim = a.shape[1]
    assert tm % LN_SLAB == 0
    n_sub = tm // LN_SLAB
    n_alias = 0 if into is None else len(into)
    osz = sum(jnp.dtype(d).itemsize for d in out_dtypes)
    nbytes = (2 * tm * kdim * 2 + kdim * D_MODEL * 2 + 2 * tm * D_MODEL * 4 + 2 * tm * D_MODEL * osz
              + tm * D_MODEL * 4 + 2 * LN_SLAB * D_MODEL * 4)
    return pl.pallas_call(
        functools.partial(_mm_ln_kernel, n_alias=n_alias, n_sub=n_sub),
        name="mm_ln",
        grid=(n_tiles,),
        in_specs=[pl.BlockSpec(memory_space=pl.ANY)] * n_alias + [
            pl.BlockSpec((tm, kdim), lambda i: (a_blk0 + i, 0)),
            pl.BlockSpec((None, kdim, D_MODEL), lambda i: (layer, 0, 0), pipeline_mode=pl.Buffered(1)),
            pl.BlockSpec((tm, D_MODEL), lambda i: (x_blk0 + i, 0))] + _ln_param_specs(gi),
        out_specs=[pl.BlockSpec((tm, D_MODEL), lambda i: (o_blk0 + i, 0)) for _ in out_dtypes],
        out_shape=[jax.ShapeDtypeStruct((out_rows, D_MODEL), d) for d in out_dtypes],
        input_output_aliases={j: j for j in range(n_alias)},
        scratch_shapes=[pltpu.VMEM((n_sub, LN_SLAB, D_MODEL), F32)],
        compiler_params=_params(("arbitrary",), nbytes),
    )(*(into or ()), a, w, x, *gb)


def _mm_ln_all(a, w, layer, x, gb, gi):
    tm = TM_LN if a.shape[1] <= D_MODEL else TM_LN_WIDE
    return _mm_ln(a, w, layer, x, gb, gi, tm=tm, n_tiles=M_ALL // tm, a_blk0=0, x_blk0=0, o_blk0=0,
                  out_rows=M_ALL, out_dtypes=(F32, BF16))


def _dot_nt(a, b):
    return lax.dot_general(a, b, (((1,), (1,)), ((), ())), preferred_element_type=F32)


def _band_attn_kernel(q_ref, kl_ref, kc_ref, vl_ref, vc_ref, bias_ref, o_ref, k_sc, v_sc):
    hg = pl.program_id(2)
    k_sc[0:QB, :] = kl_ref[...]
    k_sc[QB:2 * QB, :] = kc_ref[...]
    v_sc[0:QB, :] = vl_ref[...]
    v_sc[QB:2 * QB, :] = vc_ref[...]
    no_left = pl.program_id(1) == 0
    col = lax.broadcasted_iota(jnp.int32, (PAIR, PAIR_KEYS), 1)

    def scores(h, r0):
        sl = slice(h * HEAD_DIM, (h + 1) * HEAD_DIM)
        s = _dot_nt(q_ref[r0:r0 + PAIR, sl], k_sc[r0:r0 + PAIR_KEYS, sl]) + bias_ref[hg * HG + h]
        return jnp.where(jnp.logical_and(no_left, col < QB - r0), NEG_BIG, s)

    def finish(h, r0, s):
        sl = slice(h * HEAD_DIM, (h + 1) * HEAD_DIM)
        m = jnp.max(s, axis=-1, keepdims=True)
        p = jnp.exp(s - m)
        denom = jnp.sum(p, axis=-1, keepdims=True)
        o = jnp.dot(p.astype(BF16), v_sc[r0:r0 + PAIR_KEYS, sl], preferred_element_type=F32)
        o_ref[r0:r0 + PAIR, sl] = (o / denom).astype(o_ref.dtype)

    units = [(h, pr * PAIR) for h in range(HG) for pr in range(QB // PAIR)]
    pending = None
    for h, r0 in units:
        s = scores(h, r0)
        if pending is not None:
            finish(*pending)
        pending = (h, r0, s)
    finish(*pending)


def _band_attn_prompt(q, kv, bias, a):
    nb = SEQ // QB
    w = HG * HEAD_DIM
    vb = D_MODEL // w
    cur = lambda b, i, g: (b * nb + i, g)
    left = lambda b, i, g: (b * nb + jnp.maximum(i - 1, 0), g)
    cur_v = lambda b, i, g: (b * nb + i, vb + g)
    left_v = lambda b, i, g: (b * nb + jnp.maximum(i - 1, 0), vb + g)
    nbytes = (2 * 6 * QB * w * 2 + 2 * N_HEADS * PAIR * PAIR_KEYS * 4 + 4 * QB * w * 2
              + 8 * PAIR * PAIR_KEYS * 4)
    return pl.pallas_call(
        _band_attn_kernel,
        name="band_attn_prompt",
        grid=(BATCH, nb, N_HEADS // HG),
        in_specs=[pl.BlockSpec((QB, w), cur),
                  pl.BlockSpec((QB, w), left),
                  pl.BlockSpec((QB, w), cur),
                  pl.BlockSpec((QB, w), left_v),
                  pl.BlockSpec((QB, w), cur_v),
                  pl.BlockSpec((None, N_HEADS, PAIR, PAIR_KEYS), lambda b, i, g: (a, 0, 0, 0))],
        out_specs=pl.BlockSpec((QB, w), cur),
        out_shape=jax.ShapeDtypeStruct((M_ALL, D_MODEL), BF16),
        scratch_shapes=[pltpu.VMEM((2 * QB, w), BF16), pltpu.VMEM((2 * QB, w), BF16)],
        compiler_params=_params(("arbitrary", "arbitrary", "arbitrary"), nbytes),
    )(q, kv, kv, kv, kv, bias)


def _band_attn_sample_kernel(o_any, q_ref, kn_ref, vn_ref, ck_ref, cv_ref, bias_ref, o_ref):
    del o_any
    past = ck_ref.shape[0] // N_HEADS
    for h in range(N_HEADS):
        sl = slice(h * HEAD_DIM, (h + 1) * HEAD_DIM)
        rows = pl.ds(h, past, stride=N_HEADS)
        q = q_ref[:, sl]
        s_c = _dot_nt(q, ck_ref[rows, :].astype(BF16)) + bias_ref[h, 0:DEC_SEQ, 0:past]
        s_n = _dot_nt(q, kn_ref[:, sl]) + bias_ref[h, 0:DEC_SEQ, past:past + DEC_SEQ]
        m = jnp.maximum(jnp.max(s_c, axis=-1, keepdims=True), jnp.max(s_n, axis=-1, keepdims=True))
        p_c = jnp.exp(s_c - m)
        p_n = jnp.exp(s_n - m)
        denom = jnp.sum(p_c, axis=-1, keepdims=True) + jnp.sum(p_n, axis=-1, keepdims=True)
        o = (jnp.dot(p_c.astype(BF16), cv_ref[rows, :].astype(BF16), preferred_element_type=F32)
             + jnp.dot(p_n.astype(BF16), vn_ref[:, sl], preferred_element_type=F32))
        o_ref[:, sl] = (o / denom).astype(o_ref.dtype)


def _band_attn_sample(o, q, kv, cache_k, cache_v, bias, a):
    t = DEC_SEQ
    rb = M_PROMPT // t
    cr = BAND_LEFT * N_HEADS
    nbytes = (2 * (4 * t * D_MODEL * 2 + 2 * cr * HEAD_DIM * 4 + N_HEADS * PAIR * PAIR_KEYS * 4)
              + 4 * BAND_LEFT * HEAD_DIM * 4)
    return pl.pallas_call(
        _band_attn_sample_kernel,
        name="band_attn_sample",
        grid=(DEC_BATCH,),
        in_specs=[pl.BlockSpec(memory_space=pl.ANY),
                  pl.BlockSpec((t, D_MODEL), lambda b: (rb + b, 0)),
                  pl.BlockSpec((t, D_MODEL), lambda b: (rb + b, 0)),
                  pl.BlockSpec((t, D_MODEL), lambda b: (rb + b, 1)),
                  pl.BlockSpec((cr, HEAD_DIM), lambda b: (a * DEC_BATCH + b, 0)),
                  pl.BlockSpec((cr, HEAD_DIM), lambda b: (a * DEC_BATCH + b, 0)),
                  pl.BlockSpec((None, N_HEADS, PAIR, PAIR_KEYS), lambda b: (a, 0, 0, 0))],
        out_specs=pl.BlockSpec((t, D_MODEL), lambda b: (rb + b, 0)),
        out_shape=jax.ShapeDtypeStruct(o.shape, o.dtype),
        input_output_aliases={0: 0},
        compiler_params=_params(("arbitrary",), nbytes),
    )(o, q, kv, kv, cache_k, cache_v, bias)


def _mem_attn_body(q_ref, k_ref, v_ref, o_ref):
    for h in range(MEM_HEADS):
        sl = slice(h * MEM_HEAD_DIM, (h + 1) * MEM_HEAD_DIM)
        s = _dot_nt(q_ref[:, sl], k_ref[:, sl].astype(BF16))
        m = jnp.max(s, axis=-1, keepdims=True)
        p = jnp.exp(s - m)
        denom = jnp.sum(p, axis=-1, keepdims=True)
        o = jnp.dot(p.astype(BF16), v_ref[:, sl].astype(BF16), preferred_element_type=F32)
        o_ref[:, sl] = (o / denom).astype(o_ref.dtype)


def _mem_attn_prompt_kernel(q_ref, k_ref, v_ref, o_ref):
    _mem_attn_body(q_ref, k_ref, v_ref, o_ref)


def _mem_attn_sample_kernel(o_any, q_ref, k_ref, v_ref, o_ref):
    del o_any
    _mem_attn_body(q_ref, k_ref, v_ref, o_ref)


def _mem_attn_prompt(q, mem_kv, layer):
    tm = TM_MEM
    nt = SEQ // tm
    nbytes = 2 * (tm * D_MODEL * 2 + 2 * N_MEM * D_MODEL * 4 + tm * D_MODEL * 2) + 6 * tm * N_MEM * 4
    return pl.pallas_call(
        _mem_attn_prompt_kernel,
        name="mem_attn_prompt",
        grid=(BATCH, nt),
        in_specs=[pl.BlockSpec((tm, D_MODEL), lambda b, i: (b * nt + i, 0)),
                  pl.BlockSpec((None, N_MEM, D_MODEL), lambda b, i: (layer, b, 0)),
                  pl.BlockSpec((None, N_MEM, D_MODEL), lambda b, i: (layer, b, 1))],
        out_specs=pl.BlockSpec((tm, D_MODEL), lambda b, i: (b * nt + i, 0)),
        out_shape=jax.ShapeDtypeStruct((M_ALL, D_MODEL), BF16),
        compiler_params=_params(("arbitrary", "arbitrary"), nbytes),
    )(q, mem_kv, mem_kv)


def _mem_attn_sample(o, q, cache_k, cache_v, layer):
    t = DEC_SEQ
    rb = M_PROMPT // t
    nbytes = 2 * (2 * t * D_MODEL * 2 + 2 * N_MEM * D_MODEL * 4) + 6 * t * N_MEM * 4
    return pl.pallas_call(
        _mem_attn_sample_kernel,
        name="mem_attn_sample",
        grid=(DEC_BATCH,),
        in_specs=[pl.BlockSpec(memory_space=pl.ANY),
                  pl.BlockSpec((t, D_MODEL), lambda b: (rb + b, 0)),
                  pl.BlockSpec((N_MEM, D_MODEL), lambda b: (layer * DEC_BATCH + b, 0)),
                  pl.BlockSpec((N_MEM, D_MODEL), lambda b: (layer * DEC_BATCH + b, 0))],
        out_specs=pl.BlockSpec((t, D_MODEL), lambda b: (rb + b, 0)),
        out_shape=jax.ShapeDtypeStruct(o.shape, o.dtype),
        input_output_aliases={0: 0},
        compiler_params=_params(("arbitrary",), nbytes),
    )(o, q, cache_k, cache_v)


def _pool_body(x_ref, halo, pos0, n_hist, wp_ref, ps_ref, g_ref, b_ref, of_ref, ob_ref,
               ext_ref, sa_ref, sb_ref):
    tm = x_ref.shape[0]
    top = 2 * HALO
    rows = tm + top
    zeros = jnp.zeros((HALO, D_MODEL), F32)
    ext_ref[0:HALO, :] = zeros
    ext_ref[HALO:top, :] = halo
    ext_ref[top:rows, :] = x_ref[...]
    sa_ref[0:HALO, :] = zeros[:, :POOL_GROUP]
    sb_ref[0:HALO, :] = zeros[:, :POOL_GROUP]
    seen = (pos0 + n_hist + 1 + lax.broadcasted_iota(jnp.int32, (tm, 1), 0)).astype(F32)
    bufs = (sa_ref, sb_ref)
    for gi, win in enumerate(POOL_WINDOWS):
        cs = slice(gi * POOL_GROUP, (gi + 1) * POOL_GROUP)
        src, d, lvl = None, 1, 0
        while 2 * d < win:
            if src is None:
                val = ext_ref[HALO:rows, cs] + ext_ref[HALO - d:rows - d, cs]
            else:
                val = src[HALO:rows, :] + src[HALO - d:rows - d, :]
            dst = bufs[lvl % 2]
            dst[HALO:rows, :] = val
            src, d, lvl = dst, 2 * d, lvl + 1
        if src is None:
            wsum = ext_ref[top:rows, cs] + ext_ref[top - d:rows - d, cs]
        else:
            wsum = src[top:rows, :] + src[top - d:rows - d, :]
        xg = x_ref[:, cs]
        u = wsum / jnp.minimum(seen, float(win)) - xg
        y = jnp.dot(u.astype(BF16), wp_ref[gi], preferred_element_type=F32) * ps_ref[:, cs]
        of_ref[:, cs] = ALPHA * xg + y
    out = _layer_norm(of_ref[...], g_ref[...], b_ref[...])
    of_ref[...] = out
    ob_ref[...] = out.astype(BF16)


def _pool_prompt_kernel(x_ref, halo_ref, wp_ref, ps_ref, g_ref, b_ref, of_ref, ob_ref,
                        ext_ref, sa_ref, sb_ref):
    it = pl.program_id(0) % (SEQ // TM_POOL)
    halo = jnp.where(it > 0, halo_ref[...], 0.0)
    _pool_body(x_ref, halo, it * TM_POOL, 0, wp_ref, ps_ref, g_ref, b_ref, of_ref, ob_ref,
               ext_ref, sa_ref, sb_ref)


def _pool_sample_kernel(of_any, ob_any, x_ref, halo_ref, wp_ref, ps_ref, g_ref, b_ref, of_ref, ob_ref,
                        ext_ref, sa_ref, sb_ref):
    del of_any, ob_any
    _pool_body(x_ref, halo_ref[...], 0, POOL_HIST, wp_ref, ps_ref, g_ref, b_ref, of_ref, ob_ref,
               ext_ref, sa_ref, sb_ref)


def _pool_scratch(tm):
    return [pltpu.VMEM((tm + 2 * HALO, D_MODEL), F32),
            pltpu.VMEM((tm + 2 * HALO, POOL_GROUP), F32),
            pltpu.VMEM((tm + 2 * HALO, POOL_GROUP), F32)]


def _pool_const_specs(p, gi):
    return [pl.BlockSpec((None, len(POOL_WINDOWS), POOL_GROUP, POOL_GROUP), lambda i: (p, 0, 0, 0)),
            pl.BlockSpec((None, 1, D_MODEL), lambda i: (p, 0, 0))] + _ln_param_specs(gi)


def _pool_bytes(tm):
    return (2 * tm * D_MODEL * 4 + 2 * HALO * D_MODEL * 4 + 2 * D_MODEL * POOL_GROUP * 2 + 2 * tm * D_MODEL * 6
            + (tm + 2 * HALO) * (D_MODEL + 2 * POOL_GROUP) * 4 + 2 * tm * D_MODEL * 4)


def _pool_prompt(x, wp, ps, p, gb, gi):
    tm = TM_POOL
    hb = tm // HALO
    return pl.pallas_call(
        _pool_prompt_kernel,
        name="pool_prompt",
        grid=(M_PROMPT // tm,),
        in_specs=[pl.BlockSpec((tm, D_MODEL), lambda i: (i, 0)),
                  pl.BlockSpec((HALO, D_MODEL), lambda i: (jnp.maximum(i * hb - 1, 0), 0))]
                 + _pool_const_specs(p, gi),
        out_specs=[pl.BlockSpec((tm, D_MODEL), lambda i: (i, 0)),
                   pl.BlockSpec((tm, D_MODEL), lambda i: (i, 0))],
        out_shape=[jax.ShapeDtypeStruct((M_ALL, D_MODEL), F32),
                   jax.ShapeDtypeStruct((M_ALL, D_MODEL), BF16)],
        scratch_shapes=_pool_scratch(tm),
        compiler_params=_params(("arbitrary",), _pool_bytes(tm)),
    )(x, x, wp, ps, *gb)


def _pool_sample(of, ob, x, state, wp, ps, p, gb, gi):
    tm = DEC_SEQ
    rb = M_PROMPT // tm
    return pl.pallas_call(
        _pool_sample_kernel,
        name="pool_sample",
        grid=(DEC_BATCH,),
        in_specs=[pl.BlockSpec(memory_space=pl.ANY),
                  pl.BlockSpec(memory_space=pl.ANY),
                  pl.BlockSpec((tm, D_MODEL), lambda i: (rb + i, 0)),
                  pl.BlockSpec((None, HALO, D_MODEL), lambda i: (p * DEC_BATCH + i, 0, 0))]
                 + _pool_const_specs(p, gi),
        out_specs=[pl.BlockSpec((tm, D_MODEL), lambda i: (rb + i, 0)),
                   pl.BlockSpec((tm, D_MODEL), lambda i: (rb + i, 0))],
        out_shape=[jax.ShapeDtypeStruct(of.shape, of.dtype),
                   jax.ShapeDtypeStruct(ob.shape, ob.dtype)],
        input_output_aliases={0: 0, 1: 1},
        scratch_shapes=_pool_scratch(tm),
        compiler_params=_params(("arbitrary",), _pool_bytes(tm)),
    )(of, ob, x, state, wp, ps, *gb)


def _pair_bias(table):
    lead = table.shape[:-1]
    period = PAIR + PAIR_KEYS
    far = table[..., 2 * REL_CLIP:]
    vec = jnp.concatenate([jnp.broadcast_to(far, lead + (BAND_LEFT - REL_CLIP,)),
                           jnp.flip(table[..., 1:], axis=-1),
                           jnp.broadcast_to(far, lead + (period - BAND_LEFT - REL_CLIP,))], axis=-1)
    flat = jnp.tile(vec, (1,) * len(lead) + (PAIR,))[..., :PAIR * (period - 1)]
    toe = flat.reshape(lead + (PAIR, period - 1))[..., :PAIR_KEYS]
    r = np.arange(PAIR)[:, None]
    c = np.arange(PAIR_KEYS)[None, :]
    off = c - (r // CHUNK) * CHUNK
    in_band = np.logical_and(off >= 0, off < BAND_KEYS)
    return jnp.where(in_band, toe, NEG_BIG).astype(F32)


def _tail_rows(x, n_tail, cols):
    return jnp.stack([x[(bi + 1) * SEQ - n_tail:(bi + 1) * SEQ, cols] for bi in range(BATCH)])


def kernel(x_prompt, x_sample, cache_attn_k, cache_attn_v, state_pool, cache_mem_k, cache_mem_v,
           mem_prompt, w_qkv, w_attn_o, rel_bias, w_pool, pool_scale, w_mem_q, w_mem_kv, w_mem_o,
           w_ffn_in, w_ffn_out, ln_g, ln_b):
    D = D_MODEL
    assert cache_attn_k.shape[2] == BAND_LEFT and DEPTH % 2 == 0
    xp = x_prompt.reshape(M_PROMPT, D)
    xs = x_sample.reshape(M_SAMPLE, D)
    x_f = None
    x_b = jnp.concatenate([xp.astype(BF16), xs.astype(BF16)], axis=0)
    mem_b = mem_prompt.reshape(BATCH * N_MEM, D).astype(BF16)
    w_attn_o_b = w_attn_o.astype(BF16)
    w_mem_o_b = w_mem_o.astype(BF16)
    w_ffn_out_b = w_ffn_out.astype(BF16)
    w_pool_b = w_pool.astype(BF16)
    gb = (ln_g.reshape(DEPTH * 3, 1, D), ln_b.reshape(DEPTH * 3, 1, D))
    pscale = pool_scale.reshape(-1, 1, D)
    cache_ak = cache_attn_k.reshape(-1, HEAD_DIM)
    cache_av = cache_attn_v.reshape(-1, HEAD_DIM)
    cache_mk = cache_mem_k.reshape(-1, D)
    cache_mv = cache_mem_v.reshape(-1, D)
    state_pad = jnp.pad(state_pool, ((0, 0), (0, 0), (HALO - POOL_HIST, 0), (0, 0))).reshape(-1, HALO, D)
    bias = _pair_bias(rel_bias)
    mem_kv = _mem_kv_all(mem_b, w_mem_kv)
    k_cols, v_cols = slice(0, D), slice(D, 2 * D)

    ak_p, av_p, pool_p, ak_s, av_s, pool_s = [], [], [], [], [], []
    for i in range(DEPTH):
        if i % 2 == 0:
            a = i // 2
            q, = _matmul(x_b, w_qkv, a, 0, D, TM_MM, (BF16,), scale=HEAD_DIM ** -0.5)
            kv, kv_b = _matmul(x_b, w_qkv, a, D, 2 * D, TM_KV, (F32, BF16))
            o = _band_attn_prompt(q, kv_b, bias, a)
            o = _band_attn_sample(o, q, kv_b, cache_ak, cache_av, bias, a)
            ak_p.append(_tail_rows(kv, BAND_LEFT, k_cols).reshape(BATCH, BAND_LEFT, N_HEADS, HEAD_DIM))
            av_p.append(_tail_rows(kv, BAND_LEFT, v_cols).reshape(BATCH, BAND_LEFT, N_HEADS, HEAD_DIM))
            ak_s.append(kv[M_PROMPT:, k_cols].reshape(DEC_BATCH, DEC_SEQ, N_HEADS, HEAD_DIM))
            av_s.append(kv[M_PROMPT:, v_cols].reshape(DEC_BATCH, DEC_SEQ, N_HEADS, HEAD_DIM))
            if i == 0:
                outs = _mm_ln(o, w_attn_o_b, a, xp, gb, 0, tm=TM_LN_PROMPT, n_tiles=M_PROMPT // TM_LN_PROMPT,
                              a_blk0=0, x_blk0=0, o_blk0=0, out_rows=M_ALL, out_dtypes=(F32, BF16))
                x_f, x_b = _mm_ln(o, w_attn_o_b, a, xs, gb, 0, tm=M_SAMPLE, n_tiles=1,
                                  a_blk0=M_PROMPT // M_SAMPLE, x_blk0=0, o_blk0=M_PROMPT // M_SAMPLE,
                                  out_rows=M_ALL, out_dtypes=(F32, BF16), into=outs)
            else:
                x_f, x_b = _mm_ln_all(o, w_attn_o_b, a, x_f, gb, 3 * i)
        else:
            p = i // 2
            pool_p.append(_tail_rows(x_f, POOL_HIST, slice(None)))
            ext = jnp.concatenate([state_pool[p], x_f[M_PROMPT:].reshape(DEC_BATCH, DEC_SEQ, D)], axis=1)
            pool_s.append(ext[:, DEC_SEQ:])
            y_f, y_b = _pool_prompt(x_f, w_pool_b, pscale, p, gb, 3 * i)
            x_f, x_b = _pool_sample(y_f, y_b, x_f, state_pad, w_pool_b, pscale, p, gb, 3 * i)
        q, = _matmul(x_b, w_mem_q, i, 0, D, TM_MM, (BF16,), scale=MEM_HEAD_DIM ** -0.5)
        o = _mem_attn_prompt(q, mem_kv, i)
        o = _mem_attn_sample(o, q, cache_mk, cache_mv, i)
        x_f, x_b = _mm_ln_all(o, w_mem_o_b, i, x_f, gb, 3 * i + 1)
        h = _swiglu_in(x_b, w_ffn_in, i)
        if i < DEPTH - 1:
            x_f, x_b = _mm_ln_all(h, w_ffn_out_b, i, x_f, gb, 3 * i + 2)
        else:
            y_p, = _mm_ln(h, w_ffn_out_b, i, x_f, gb, 3 * i + 2, tm=TM_LN_WIDE,
                          n_tiles=M_PROMPT // TM_LN_WIDE, a_blk0=0, x_blk0=0, o_blk0=0,
                          out_rows=M_PROMPT, out_dtypes=(F32,))
            y_s, = _mm_ln(h, w_ffn_out_b, i, x_f, gb, 3 * i + 2, tm=M_SAMPLE, n_tiles=1,
                          a_blk0=M_PROMPT // M_SAMPLE, x_blk0=M_PROMPT // M_SAMPLE, o_blk0=0,
                          out_rows=M_SAMPLE, out_dtypes=(F32,))

    mem_shape = (DEPTH, BATCH, N_MEM, MEM_HEADS, MEM_HEAD_DIM)
    return (y_p.reshape(BATCH, SEQ, D), y_s.reshape(DEC_BATCH, DEC_SEQ, D),
            jnp.stack(ak_p), jnp.stack(av_p), jnp.stack(pool_p),
            mem_kv[:, :, k_cols].reshape(mem_shape), mem_kv[:, :, v_cols].reshape(mem_shape),
            jnp.stack(ak_s), jnp.stack(av_s), jnp.stack(pool_s))
```

```python
import functools

import numpy as np
import jax
import jax.numpy as jnp
from jax import lax
from jax.experimental import pallas as pl
from jax.experimental.pallas import tpu as pltpu

F32 = jnp.float32
BF16 = jnp.bfloat16

D_MODEL = 2048
BATCH = 4
SEQ = 4096
DEPTH = 4
DEC_BATCH = 8
DEC_SEQ = 32
CHUNK = 64
BAND_LEFT = 512
BAND_KEYS = BAND_LEFT + CHUNK
N_HEADS = 16
HEAD_DIM = D_MODEL // N_HEADS
REL_CLIP = 2 * CHUNK
POOL_WINDOWS = (2, 4, 8, 16)
POOL_GROUP = D_MODEL // len(POOL_WINDOWS)
POOL_HIST = max(POOL_WINDOWS) - 1
N_MEM = 256
MEM_HEADS = 4
MEM_HEAD_DIM = D_MODEL // MEM_HEADS
D_FF = 5632
ALPHA = (2.0 * DEPTH) ** 0.25
LN_EPS = 1e-5
NEG_BIG = -1e30

M_PROMPT = BATCH * SEQ
M_SAMPLE = DEC_BATCH * DEC_SEQ
M_ALL = M_PROMPT + M_SAMPLE

TM_MM = 1280
TN_MM = 1024
TN_FF = 512
FF_SUB = 2
TM_LN = 640
TM_LN_PROMPT = 512
TM_LN_WIDE = 256
LN_SLAB = 128
LN_PIECES = 2
QB = 512
PAIR = 2 * CHUNK
PAIR_KEYS = BAND_LEFT + PAIR
HG = 8
TM_MEM = 1024
MEM_SLAB = 256
TM_POOL = 512
HALO = 16

V7X_VMEM_BYTES = 64 * 1024 * 1024


def _vmem_limit(nbytes):
    return int(min(nbytes + (12 << 20), V7X_VMEM_BYTES - (6 << 20)))


def _params(sem, nbytes):
    return pltpu.CompilerParams(dimension_semantics=sem, vmem_limit_bytes=_vmem_limit(nbytes))


def _ingest_kernel(xp_ref, xs_ref, o_ref):
    is_prompt = pl.program_id(0) < M_PROMPT // M_SAMPLE

    @pl.when(is_prompt)
    def _():
        o_ref[...] = xp_ref[...].astype(BF16)

    @pl.when(jnp.logical_not(is_prompt))
    def _():
        o_ref[...] = xs_ref[...].astype(BF16)


def _ingest(xp, xs):
    tm = M_SAMPLE
    last = M_PROMPT // tm - 1
    nbytes = 2 * tm * D_MODEL * (4 + 4 + 2)
    return pl.pallas_call(
        _ingest_kernel,
        name="ingest",
        grid=(M_ALL // tm,),
        in_specs=[pl.BlockSpec((tm, D_MODEL), lambda i: (jnp.minimum(i, last), 0)),
                  pl.BlockSpec((tm, D_MODEL), lambda i: (0, 0))],
        out_specs=pl.BlockSpec((tm, D_MODEL), lambda i: (i, 0)),
        out_shape=jax.ShapeDtypeStruct((M_ALL, D_MODEL), BF16),
        compiler_params=_params(("arbitrary",), nbytes),
    )(xp, xs)


def _mm_kernel(a_ref, w_ref, *rest, scale):
    o_refs, wb_ref = rest[:-1], rest[-1]

    @pl.when(pl.program_id(1) == 0)
    def _():
        wb_ref[...] = w_ref[...].astype(BF16)

    acc = jnp.dot(a_ref[...], wb_ref[...], preferred_element_type=F32)
    if scale != 1.0:
        acc = acc * scale
    for o_ref in o_refs:
        o_ref[...] = acc.astype(o_ref.dtype)


def _matmul(a, w, layer, col0, n_out, tm, out_dtypes, scale=1.0):
    m, k = a.shape
    tn = TN_MM
    assert m % tm == 0 and n_out % tn == 0 and col0 % tn == 0
    cb = col0 // tn
    osz = sum(jnp.dtype(d).itemsize for d in out_dtypes)
    nbytes = 2 * tm * k * 2 + 2 * k * tn * 4 + k * tn * 2 + 2 * tm * tn * osz + tm * tn * 4
    return pl.pallas_call(
        functools.partial(_mm_kernel, scale=scale),
        name="mm",
        grid=(n_out // tn, m // tm),
        in_specs=[pl.BlockSpec((tm, k), lambda j, i: (i, 0)),
                  pl.BlockSpec((None, k, tn), lambda j, i: (layer, 0, j + cb))],
        out_specs=[pl.BlockSpec((tm, tn), lambda j, i: (i, j)) for _ in out_dtypes],
        out_shape=[jax.ShapeDtypeStruct((m, n_out), d) for d in out_dtypes],
        scratch_shapes=[pltpu.VMEM((k, tn), BF16)],
        compiler_params=_params(("arbitrary", "arbitrary"), nbytes),
    )(a, w)


def _mem_kv_kernel(a_ref, w_ref, o_ref):
    o_ref[...] = jnp.dot(a_ref[...], w_ref[...].astype(BF16), preferred_element_type=F32)


def _mem_kv_all(mem, w):
    m, k = mem.shape
    n = w.shape[2]
    tn = TN_MM
    nbytes = 2 * m * k * 2 + 2 * k * tn * 4 + k * tn * 2 + 2 * m * tn * 4 + m * tn * 4
    return pl.pallas_call(
        _mem_kv_kernel,
        name="mem_kv",
        grid=(w.shape[0], n // tn),
        in_specs=[pl.BlockSpec((m, k), lambda l, j: (0, 0)),
                  pl.BlockSpec((None, k, tn), lambda l, j: (l, 0, j))],
        out_specs=pl.BlockSpec((None, m, tn), lambda l, j: (l, 0, j)),
        out_shape=jax.ShapeDtypeStruct((w.shape[0], m, n), F32),
        compiler_params=_params(("arbitrary", "arbitrary"), nbytes),
    )(mem, w)


def _swiglu_kernel(a_ref, wg_ref, wu_ref, o_ref, wgb_ref, wub_ref):
    @pl.when(pl.program_id(1) == 0)
    def _():
        wgb_ref[...] = wg_ref[...].astype(BF16)
        wub_ref[...] = wu_ref[...].astype(BF16)

    ts = a_ref.shape[0] // FF_SUB

    def project(r):
        a = a_ref[r * ts:(r + 1) * ts, :]
        return (jnp.dot(a, wgb_ref[...], preferred_element_type=F32),
                jnp.dot(a, wub_ref[...], preferred_element_type=F32))

    def finish(r, gate, up):
        o_ref[r * ts:(r + 1) * ts, :] = (gate / (1.0 + jnp.exp(-gate)) * up).astype(o_ref.dtype)

    pending = None
    for r in range(FF_SUB):
        gate_up = project(r)
        if pending is not None:
            finish(*pending)
        pending = (r, *gate_up)
    finish(*pending)


def _swiglu_in(a, w_in, layer):
    m, k = a.shape
    tm, tn = TM_MM, TN_FF
    nj = D_FF // tn
    nbytes = 2 * tm * k * 2 + 4 * k * tn * 4 + 2 * k * tn * 2 + 2 * tm * tn * 2 + 3 * tm * tn * 4
    return pl.pallas_call(
        _swiglu_kernel,
        name="swiglu_in",
        grid=(nj, m // tm),
        in_specs=[pl.BlockSpec((tm, k), lambda j, i: (i, 0)),
                  pl.BlockSpec((None, k, tn), lambda j, i: (layer, 0, j)),
                  pl.BlockSpec((None, k, tn), lambda j, i: (layer, 0, j + nj))],
        out_specs=pl.BlockSpec((tm, tn), lambda j, i: (i, j)),
        out_shape=jax.ShapeDtypeStruct((m, D_FF), BF16),
        scratch_shapes=[pltpu.VMEM((k, tn), BF16), pltpu.VMEM((k, tn), BF16)],
        compiler_params=_params(("arbitrary", "arbitrary"), nbytes),
    )(a, w_in, w_in)


def _layer_norm(z, g, b):
    mu = jnp.mean(z, axis=-1, keepdims=True)
    zc = z - mu
    var = jnp.mean(zc * zc, axis=-1, keepdims=True)
    return zc * lax.rsqrt(var + LN_EPS) * g + b


def _mm_ln_kernel(*refs, n_alias, n_sub):
    a_ref, w_ref, x_ref, g_ref, b_ref = refs[n_alias:n_alias + 5]
    o_refs = refs[n_alias + 5:-1]
    z_ref = refs[-1]
    ts = z_ref.shape[1]
    tc = D_MODEL // LN_PIECES
    tr = ts // LN_PIECES

    def pre_norm(r, c):
        rs, cs = slice(r * ts, (r + 1) * ts), slice(c * tc, (c + 1) * tc)
        z_ref[r, :, cs] = ALPHA * x_ref[rs, cs] + jnp.dot(a_ref[rs, :], w_ref[:, cs],
                                                         preferred_element_type=F32)

    def finish(r, c):
        y = _layer_norm(z_ref[r, c * tr:(c + 1) * tr, :], g_ref[...], b_ref[...])
        for o_ref in o_refs:
            o_ref[r * ts + c * tr:r * ts + (c + 1) * tr, :] = y.astype(o_ref.dtype)

    for r in range(n_sub + 1):
        for c in range(LN_PIECES):
            if r < n_sub:
                pre_norm(r, c)
            if r > 0:
                finish(r - 1, c)


def _ln_param_specs(gi):
    return [pl.BlockSpec((None, 1, D_MODEL), lambda *_: (gi, 0, 0)),
            pl.BlockSpec((None, 1, D_MODEL), lambda *_: (gi, 0, 0))]


def _mm_ln(a, w, layer, x, gb, gi, *, tm, n_tiles, a_blk0, x_blk0, o_blk0, out_rows, out_dtypes, into=None):
    kdim = a.shape[1]
    assert tm % LN_SLAB == 0
    n_sub = tm // LN_SLAB
    n_alias = 0 if into is None else len(into)
    osz = sum(jnp.dtype(d).itemsize for d in out_dtypes)
    nbytes = (2 * tm * kdim * 2 + kdim * D_MODEL * 2 + 2 * tm * D_MODEL * 4 + 2 * tm * D_MODEL * osz
              + tm * D_MODEL * 4 + 2 * LN_SLAB * D_MODEL * 4)
    return pl.pallas_call(
        functools.partial(_mm_ln_kernel, n_alias=n_alias, n_sub=n_sub),
        name="mm_ln",
        grid=(n_tiles,),
        in_specs=[pl.BlockSpec(memory_space=pl.ANY)] * n_alias + [
            pl.BlockSpec((tm, kdim), lambda i: (a_blk0 + i, 0)),
            pl.BlockSpec((None, kdim, D_MODEL), lambda i: (layer, 0, 0), pipeline_mode=pl.Buffered(1)),
            pl.BlockSpec((tm, D_MODEL), lambda i: (x_blk0 + i, 0))] + _ln_param_specs(gi),
        out_specs=[pl.BlockSpec((tm, D_MODEL), lambda i: (o_blk0 + i, 0)) for _ in out_dtypes],
        out_shape=[jax.ShapeDtypeStruct((out_rows, D_MODEL), d) for d in out_dtypes],
        input_output_aliases={j: j for j in range(n_alias)},
        scratch_shapes=[pltpu.VMEM((n_sub, LN_SLAB, D_MODEL), F32)],
        compiler_params=_params(("arbitrary",), nbytes),
    )(*(into or ()), a, w, x, *gb)


def _mm_ln_all(a, w, layer, x, gb, gi):
    tm = TM_LN if a.shape[1] <= D_MODEL else TM_LN_WIDE
    return _mm_ln(a, w, layer, x, gb, gi, tm=tm, n_tiles=M_ALL // tm, a_blk0=0, x_blk0=0, o_blk0=0,
                  out_rows=M_ALL, out_dtypes=(F32, BF16))


def _dot_nt(a, b):
    return lax.dot_general(a, b, (((1,), (1,)), ((), ())), preferred_element_type=F32)


def _band_attn_kernel(q_ref, kl_ref, kc_ref, vl_ref, vc_ref, bias_ref, o_ref, k_sc, v_sc):
    hg = pl.program_id(2)
    k_sc[0:QB, :] = kl_ref[...]
    k_sc[QB:2 * QB, :] = kc_ref[...]
    v_sc[0:QB, :] = vl_ref[...]
    v_sc[QB:2 * QB, :] = vc_ref[...]
    no_left = pl.program_id(1) == 0
    col = lax.broadcasted_iota(jnp.int32, (PAIR, PAIR_KEYS), 1)

    def scores(h, r0):
        sl = slice(h * HEAD_DIM, (h + 1) * HEAD_DIM)
        s = _dot_nt(q_ref[r0:r0 + PAIR, sl], k_sc[r0:r0 + PAIR_KEYS, sl]) + bias_ref[hg * HG + h]
        return jnp.where(jnp.logical_and(no_left, col < QB - r0), NEG_BIG, s)

    def finish(h, r0, s):
        sl = slice(h * HEAD_DIM, (h + 1) * HEAD_DIM)
        m = jnp.max(s, axis=-1, keepdims=True)
        p = jnp.exp(s - m)
        denom = jnp.sum(p, axis=-1, keepdims=True)
        o = jnp.dot(p.astype(BF16), v_sc[r0:r0 + PAIR_KEYS, sl], preferred_element_type=F32)
        o_ref[r0:r0 + PAIR, sl] = (o / denom).astype(o_ref.dtype)

    units = [(h, pr * PAIR) for h in range(HG) for pr in range(QB // PAIR)]
    pending = None
    for h, r0 in units:
        s = scores(h, r0)
        if pending is not None:
            finish(*pending)
        pending = (h, r0, s)
    finish(*pending)


def _band_attn_prompt(q, kv, bias, a):
    nb = SEQ // QB
    w = HG * HEAD_DIM
    vb = D_MODEL // w
    cur = lambda b, i, g: (b * nb + i, g)
    left = lambda b, i, g: (b * nb + jnp.maximum(i - 1, 0), g)
    cur_v = lambda b, i, g: (b * nb + i, vb + g)
    left_v = lambda b, i, g: (b * nb + jnp.maximum(i - 1, 0), vb + g)
    nbytes = (2 * 6 * QB * w * 2 + 2 * N_HEADS * PAIR * PAIR_KEYS * 4 + 4 * QB * w * 2
              + 8 * PAIR * PAIR_KEYS * 4)
    return pl.pallas_call(
        _band_attn_kernel,
        name="band_attn_prompt",
        grid=(BATCH, nb, N_HEADS // HG),
        in_specs=[pl.BlockSpec((QB, w), cur),
                  pl.BlockSpec((QB, w), left),
                  pl.BlockSpec((QB, w), cur),
                  pl.BlockSpec((QB, w), left_v),
                  pl.BlockSpec((QB, w), cur_v),
                  pl.BlockSpec((None, N_HEADS, PAIR, PAIR_KEYS), lambda b, i, g: (a, 0, 0, 0))],
        out_specs=pl.BlockSpec((QB, w), cur),
        out_shape=jax.ShapeDtypeStruct((M_ALL, D_MODEL), BF16),
        scratch_shapes=[pltpu.VMEM((2 * QB, w), BF16), pltpu.VMEM((2 * QB, w), BF16)],
        compiler_params=_params(("arbitrary", "arbitrary", "arbitrary"), nbytes),
    )(q, kv, kv, kv, kv, bias)


def _band_attn_sample_kernel(o_any, q_ref, kn_ref, vn_ref, ck_ref, cv_ref, bias_ref, o_ref):
    del o_any
    past = ck_ref.shape[0] // N_HEADS
    for h in range(N_HEADS):
        sl = slice(h * HEAD_DIM, (h + 1) * HEAD_DIM)
        rows = pl.ds(h, past, stride=N_HEADS)
        q = q_ref[:, sl]
        s_c = _dot_nt(q, ck_ref[rows, :].astype(BF16)) + bias_ref[h, 0:DEC_SEQ, 0:past]
        s_n = _dot_nt(q, kn_ref[:, sl]) + bias_ref[h, 0:DEC_SEQ, past:past + DEC_SEQ]
        m = jnp.maximum(jnp.max(s_c, axis=-1, keepdims=True), jnp.max(s_n, axis=-1, keepdims=True))
        p_c = jnp.exp(s_c - m)
        p_n = jnp.exp(s_n - m)
        denom = jnp.sum(p_c, axis=-1, keepdims=True) + jnp.sum(p_n, axis=-1, keepdims=True)
        o = (jnp.dot(p_c.astype(BF16), cv_ref[rows, :].astype(BF16), preferred_element_type=F32)
             + jnp.dot(p_n.astype(BF16), vn_ref[:, sl], preferred_element_type=F32))
        o_ref[:, sl] = (o / denom).astype(o_ref.dtype)


def _band_attn_sample(o, q, kv, cache_k, cache_v, bias, a):
    t = DEC_SEQ
    rb = M_PROMPT // t
    cr = BAND_LEFT * N_HEADS
    nbytes = (2 * (4 * t * D_MODEL * 2 + 2 * cr * HEAD_DIM * 4 + N_HEADS * PAIR * PAIR_KEYS * 4)
              + 4 * BAND_LEFT * HEAD_DIM * 4)
    return pl.pallas_call(
        _band_attn_sample_kernel,
        name="band_attn_sample",
        grid=(DEC_BATCH,),
        in_specs=[pl.BlockSpec(memory_space=pl.ANY),
                  pl.BlockSpec((t, D_MODEL), lambda b: (rb + b, 0)),
                  pl.BlockSpec((t, D_MODEL), lambda b: (rb + b, 0)),
                  pl.BlockSpec((t, D_MODEL), lambda b: (rb + b, 1)),
                  pl.BlockSpec((cr, HEAD_DIM), lambda b: (a * DEC_BATCH + b, 0)),
                  pl.BlockSpec((cr, HEAD_DIM), lambda b: (a * DEC_BATCH + b, 0)),
                  pl.BlockSpec((None, N_HEADS, PAIR, PAIR_KEYS), lambda b: (a, 0, 0, 0))],
        out_specs=pl.BlockSpec((t, D_MODEL), lambda b: (rb + b, 0)),
        out_shape=jax.ShapeDtypeStruct(o.shape, o.dtype),
        input_output_aliases={0: 0},
        compiler_params=_params(("arbitrary",), nbytes),
    )(o, q, kv, kv, cache_k, cache_v, bias)


def _mem_attn_body(q_ref, k_ref, v_ref, o_ref):
    ts = min(q_ref.shape[0], MEM_SLAB)
    head = lambda h: slice(h * MEM_HEAD_DIM, (h + 1) * MEM_HEAD_DIM)
    kb = [k_ref[:, head(h)].astype(BF16) for h in range(MEM_HEADS)]
    vb = [v_ref[:, head(h)].astype(BF16) for h in range(MEM_HEADS)]

    def scores(h, r0):
        return _dot_nt(q_ref[r0:r0 + ts, head(h)], kb[h])

    def finish(h, r0, s):
        m = jnp.max(s, axis=-1, keepdims=True)
        p = jnp.exp(s - m)
        denom = jnp.sum(p, axis=-1, keepdims=True)
        o = jnp.dot(p.astype(BF16), vb[h], preferred_element_type=F32)
        o_ref[r0:r0 + ts, head(h)] = (o / denom).astype(o_ref.dtype)

    units = [(h, r0) for h in range(MEM_HEADS) for r0 in range(0, q_ref.shape[0], ts)]
    pending = None
    for h, r0 in units:
        s = scores(h, r0)
        if pending is not None:
            finish(*pending)
        pending = (h, r0, s)
    finish(*pending)


def _mem_attn_prompt_kernel(q_ref, k_ref, v_ref, o_ref):
    _mem_attn_body(q_ref, k_ref, v_ref, o_ref)


def _mem_attn_sample_kernel(o_any, q_ref, k_ref, v_ref, o_ref):
    del o_any
    _mem_attn_body(q_ref, k_ref, v_ref, o_ref)


def _mem_attn_prompt(q, mem_kv, layer):
    tm = TM_MEM
    nt = SEQ // tm
    nbytes = 2 * (tm * D_MODEL * 2 + 2 * N_MEM * D_MODEL * 4 + tm * D_MODEL * 2) + 6 * tm * N_MEM * 4
    return pl.pallas_call(
        _mem_attn_prompt_kernel,
        name="mem_attn_prompt",
        grid=(BATCH, nt),
        in_specs=[pl.BlockSpec((tm, D_MODEL), lambda b, i: (b * nt + i, 0)),
                  pl.BlockSpec((None, N_MEM, D_MODEL), lambda b, i: (layer, b, 0)),
                  pl.BlockSpec((None, N_MEM, D_MODEL), lambda b, i: (layer, b, 1))],
        out_specs=pl.BlockSpec((tm, D_MODEL), lambda b, i: (b * nt + i, 0)),
        out_shape=jax.ShapeDtypeStruct((M_ALL, D_MODEL), BF16),
        compiler_params=_params(("arbitrary", "arbitrary"), nbytes),
    )(q, mem_kv, mem_kv)


def _mem_attn_sample(o, q, cache_k, cache_v, layer):
    t = DEC_SEQ
    rb = M_PROMPT // t
    nbytes = 2 * (2 * t * D_MODEL * 2 + 2 * N_MEM * D_MODEL * 4) + 6 * t * N_MEM * 4
    return pl.pallas_call(
        _mem_attn_sample_kernel,
        name="mem_attn_sample",
        grid=(DEC_BATCH,),
        in_specs=[pl.BlockSpec(memory_space=pl.ANY),
                  pl.BlockSpec((t, D_MODEL), lambda b: (rb + b, 0)),
                  pl.BlockSpec((N_MEM, D_MODEL), lambda b: (layer * DEC_BATCH + b, 0)),
                  pl.BlockSpec((N_MEM, D_MODEL), lambda b: (layer * DEC_BATCH + b, 0))],
        out_specs=pl.BlockSpec((t, D_MODEL), lambda b: (rb + b, 0)),
        out_shape=jax.ShapeDtypeStruct(o.shape, o.dtype),
        input_output_aliases={0: 0},
        compiler_params=_params(("arbitrary",), nbytes),
    )(o, q, cache_k, cache_v)


def _pool_body(x_ref, halo, pos0, n_hist, wp_ref, ps_ref, g_ref, b_ref, of_ref, ob_ref,
               ext_ref, sa_ref, sb_ref):
    tm = x_ref.shape[0]
    top = 2 * HALO
    rows = tm + top
    zeros = jnp.zeros((HALO, D_MODEL), F32)
    ext_ref[0:HALO, :] = zeros
    ext_ref[HALO:top, :] = halo
    ext_ref[top:rows, :] = x_ref[...]
    sa_ref[0:HALO, :] = zeros[:, :POOL_GROUP]
    sb_ref[0:HALO, :] = zeros[:, :POOL_GROUP]
    seen = (pos0 + n_hist + 1 + lax.broadcasted_iota(jnp.int32, (tm, 1), 0)).astype(F32)
    bufs = (sa_ref, sb_ref)
    for gi, win in enumerate(POOL_WINDOWS):
        cs = slice(gi * POOL_GROUP, (gi + 1) * POOL_GROUP)
        src, d, lvl = None, 1, 0
        while 2 * d < win:
            if src is None:
                val = ext_ref[HALO:rows, cs] + ext_ref[HALO - d:rows - d, cs]
            else:
                val = src[HALO:rows, :] + src[HALO - d:rows - d, :]
            dst = bufs[lvl % 2]
            dst[HALO:rows, :] = val
            src, d, lvl = dst, 2 * d, lvl + 1
        if src is None:
            wsum = ext_ref[top:rows, cs] + ext_ref[top - d:rows - d, cs]
        else:
            wsum = src[top:rows, :] + src[top - d:rows - d, :]
        xg = x_ref[:, cs]
        u = wsum / jnp.minimum(seen, float(win)) - xg
        y = jnp.dot(u.astype(BF16), wp_ref[gi], preferred_element_type=F32) * ps_ref[:, cs]
        of_ref[:, cs] = ALPHA * xg + y
    out = _layer_norm(of_ref[...], g_ref[...], b_ref[...])
    of_ref[...] = out
    ob_ref[...] = out.astype(BF16)


def _pool_prompt_kernel(x_ref, halo_ref, wp_ref, ps_ref, g_ref, b_ref, of_ref, ob_ref,
                        ext_ref, sa_ref, sb_ref):
    it = pl.program_id(0) % (SEQ // TM_POOL)
    halo = jnp.where(it > 0, halo_ref[...], 0.0)
    _pool_body(x_ref, halo, it * TM_POOL, 0, wp_ref, ps_ref, g_ref, b_ref, of_ref, ob_ref,
               ext_ref, sa_ref, sb_ref)


def _pool_sample_kernel(of_any, ob_any, x_ref, halo_ref, wp_ref, ps_ref, g_ref, b_ref, of_ref, ob_ref,
                        ext_ref, sa_ref, sb_ref):
    del of_any, ob_any
    _pool_body(x_ref, halo_ref[...], 0, POOL_HIST, wp_ref, ps_ref, g_ref, b_ref, of_ref, ob_ref,
               ext_ref, sa_ref, sb_ref)


def _pool_scratch(tm):
    return [pltpu.VMEM((tm + 2 * HALO, D_MODEL), F32),
            pltpu.VMEM((tm + 2 * HALO, POOL_GROUP), F32),
            pltpu.VMEM((tm + 2 * HALO, POOL_GROUP), F32)]


def _pool_const_specs(p, gi):
    return [pl.BlockSpec((None, len(POOL_WINDOWS), POOL_GROUP, POOL_GROUP), lambda i: (p, 0, 0, 0)),
            pl.BlockSpec((None, 1, D_MODEL), lambda i: (p, 0, 0))] + _ln_param_specs(gi)


def _pool_bytes(tm):
    return (2 * tm * D_MODEL * 4 + 2 * HALO * D_MODEL * 4 + 2 * D_MODEL * POOL_GROUP * 2 + 2 * tm * D_MODEL * 6
            + (tm + 2 * HALO) * (D_MODEL + 2 * POOL_GROUP) * 4 + 2 * tm * D_MODEL * 4)


def _pool_prompt(x, wp, ps, p, gb, gi):
    tm = TM_POOL
    hb = tm // HALO
    return pl.pallas_call(
        _pool_prompt_kernel,
        name="pool_prompt",
        grid=(M_PROMPT // tm,),
        in_specs=[pl.BlockSpec((tm, D_MODEL), lambda i: (i, 0)),
                  pl.BlockSpec((HALO, D_MODEL), lambda i: (jnp.maximum(i * hb - 1, 0), 0))]
                 + _pool_const_specs(p, gi),
        out_specs=[pl.BlockSpec((tm, D_MODEL), lambda i: (i, 0)),
                   pl.BlockSpec((tm, D_MODEL), lambda i: (i, 0))],
        out_shape=[jax.ShapeDtypeStruct((M_ALL, D_MODEL), F32),
                   jax.ShapeDtypeStruct((M_ALL, D_MODEL), BF16)],
        scratch_shapes=_pool_scratch(tm),
        compiler_params=_params(("arbitrary",), _pool_bytes(tm)),
    )(x, x, wp, ps, *gb)


def _pool_sample(of, ob, x, state, wp, ps, p, gb, gi):
    tm = DEC_SEQ
    rb = M_PROMPT // tm
    return pl.pallas_call(
        _pool_sample_kernel,
        name="pool_sample",
        grid=(DEC_BATCH,),
        in_specs=[pl.BlockSpec(memory_space=pl.ANY),
                  pl.BlockSpec(memory_space=pl.ANY),
                  pl.BlockSpec((tm, D_MODEL), lambda i: (rb + i, 0)),
                  pl.BlockSpec((None, HALO, D_MODEL), lambda i: (p * DEC_BATCH + i, 0, 0))]
                 + _pool_const_specs(p, gi),
        out_specs=[pl.BlockSpec((tm, D_MODEL), lambda i: (rb + i, 0)),
                   pl.BlockSpec((tm, D_MODEL), lambda i: (rb + i, 0))],
        out_shape=[jax.ShapeDtypeStruct(of.shape, of.dtype),
                   jax.ShapeDtypeStruct(ob.shape, ob.dtype)],
        input_output_aliases={0: 0, 1: 1},
        scratch_shapes=_pool_scratch(tm),
        compiler_params=_params(("arbitrary",), _pool_bytes(tm)),
    )(of, ob, x, state, wp, ps, *gb)


def _pair_bias(table):
    lead = table.shape[:-1]
    period = PAIR + PAIR_KEYS
    far = table[..., 2 * REL_CLIP:]
    vec = jnp.concatenate([jnp.broadcast_to(far, lead + (BAND_LEFT - REL_CLIP,)),
                           jnp.flip(table[..., 1:], axis=-1),
                           jnp.broadcast_to(far, lead + (period - BAND_LEFT - REL_CLIP,))], axis=-1)
    flat = jnp.tile(vec, (1,) * len(lead) + (PAIR,))[..., :PAIR * (period - 1)]
    toe = flat.reshape(lead + (PAIR, period - 1))[..., :PAIR_KEYS]
    r = np.arange(PAIR)[:, None]
    c = np.arange(PAIR_KEYS)[None, :]
    off = c - (r // CHUNK) * CHUNK
    in_band = np.logical_and(off >= 0, off < BAND_KEYS)
    return jnp.where(in_band, toe, NEG_BIG).astype(F32)


def _tail_rows(x, n_tail, cols):
    return jnp.stack([x[(bi + 1) * SEQ - n_tail:(bi + 1) * SEQ, cols] for bi in range(BATCH)])


def kernel(x_prompt, x_sample, cache_attn_k, cache_attn_v, state_pool, cache_mem_k, cache_mem_v,
           mem_prompt, w_qkv, w_attn_o, rel_bias, w_pool, pool_scale, w_mem_q, w_mem_kv, w_mem_o,
           w_ffn_in, w_ffn_out, ln_g, ln_b):
    D = D_MODEL
    assert cache_attn_k.shape[2] == BAND_LEFT and DEPTH % 2 == 0
    xp = x_prompt.reshape(M_PROMPT, D)
    xs = x_sample.reshape(M_SAMPLE, D)
    x_f = None
    x_b = _ingest(xp, xs)
    mem_b = mem_prompt.reshape(BATCH * N_MEM, D).astype(BF16)
    w_attn_o_b = w_attn_o.astype(BF16)
    w_mem_o_b = w_mem_o.astype(BF16)
    w_ffn_out_b = w_ffn_out.astype(BF16)
    w_pool_b = w_pool.astype(BF16)
    gb = (ln_g.reshape(DEPTH * 3, 1, D), ln_b.reshape(DEPTH * 3, 1, D))
    pscale = pool_scale.reshape(-1, 1, D)
    cache_ak = cache_attn_k.reshape(-1, HEAD_DIM)
    cache_av = cache_attn_v.reshape(-1, HEAD_DIM)
    cache_mk = cache_mem_k.reshape(-1, D)
    cache_mv = cache_mem_v.reshape(-1, D)
    state_pad = jnp.pad(state_pool, ((0, 0), (0, 0), (HALO - POOL_HIST, 0), (0, 0))).reshape(-1, HALO, D)
    bias = _pair_bias(rel_bias)
    mem_kv = _mem_kv_all(mem_b, w_mem_kv)
    k_cols, v_cols = slice(0, D), slice(D, 2 * D)

    ak_p, av_p, pool_p, ak_s, av_s, pool_s = [], [], [], [], [], []
    for i in range(DEPTH):
        if i % 2 == 0:
            a = i // 2
            q, = _matmul(x_b, w_qkv, a, 0, D, TM_MM, (BF16,), scale=HEAD_DIM ** -0.5)
            kv, kv_b = _matmul(x_b, w_qkv, a, D, 2 * D, TM_MM, (F32, BF16))
            o = _band_attn_prompt(q, kv_b, bias, a)
            o = _band_attn_sample(o, q, kv_b, cache_ak, cache_av, bias, a)
            ak_p.append(_tail_rows(kv, BAND_LEFT, k_cols).reshape(BATCH, BAND_LEFT, N_HEADS, HEAD_DIM))
            av_p.append(_tail_rows(kv, BAND_LEFT, v_cols).reshape(BATCH, BAND_LEFT, N_HEADS, HEAD_DIM))
            ak_s.append(kv[M_PROMPT:, k_cols].reshape(DEC_BATCH, DEC_SEQ, N_HEADS, HEAD_DIM))
            av_s.append(kv[M_PROMPT:, v_cols].reshape(DEC_BATCH, DEC_SEQ, N_HEADS, HEAD_DIM))
            if i == 0:
                outs = _mm_ln(o, w_attn_o_b, a, xp, gb, 0, tm=TM_LN_PROMPT, n_tiles=M_PROMPT // TM_LN_PROMPT,
                              a_blk0=0, x_blk0=0, o_blk0=0, out_rows=M_ALL, out_dtypes=(F32, BF16))
                x_f, x_b = _mm_ln(o, w_attn_o_b, a, xs, gb, 0, tm=M_SAMPLE, n_tiles=1,
                                  a_blk0=M_PROMPT // M_SAMPLE, x_blk0=0, o_blk0=M_PROMPT // M_SAMPLE,
                                  out_rows=M_ALL, out_dtypes=(F32, BF16), into=outs)
            else:
                x_f, x_b = _mm_ln_all(o, w_attn_o_b, a, x_f, gb, 3 * i)
        else:
            p = i // 2
            pool_p.append(_tail_rows(x_f, POOL_HIST, slice(None)))
            ext = jnp.concatenate([state_pool[p], x_f[M_PROMPT:].reshape(DEC_BATCH, DEC_SEQ, D)], axis=1)
            pool_s.append(ext[:, DEC_SEQ:])
            y_f, y_b = _pool_prompt(x_f, w_pool_b, pscale, p, gb, 3 * i)
            x_f, x_b = _pool_sample(y_f, y_b, x_f, state_pad, w_pool_b, pscale, p, gb, 3 * i)
        q, = _matmul(x_b, w_mem_q, i, 0, D, TM_MM, (BF16,), scale=MEM_HEAD_DIM ** -0.5)
        o = _mem_attn_prompt(q, mem_kv, i)
        o = _mem_attn_sample(o, q, cache_mk, cache_mv, i)
        x_f, x_b = _mm_ln_all(o, w_mem_o_b, i, x_f, gb, 3 * i + 1)
        h = _swiglu_in(x_b, w_ffn_in, i)
        if i < DEPTH - 1:
            x_f, x_b = _mm_ln_all(h, w_ffn_out_b, i, x_f, gb, 3 * i + 2)
        else:
            y_p, = _mm_ln(h, w_ffn_out_b, i, x_f, gb, 3 * i + 2, tm=TM_LN_WIDE,
                          n_tiles=M_PROMPT // TM_LN_WIDE, a_blk0=0, x_blk0=0, o_blk0=0,
                          out_rows=M_PROMPT, out_dtypes=(F32,))
            y_s, = _mm_ln(h, w_ffn_out_b, i, x_f, gb, 3 * i + 2, tm=M_SAMPLE, n_tiles=1,
                          a_blk0=M_PROMPT // M_SAMPLE, x_blk0=M_PROMPT // M_SAMPLE, o_blk0=0,
                          out_rows=M_SAMPLE, out_dtypes=(F32,))

    mem_shape = (DEPTH, BATCH, N_MEM, MEM_HEADS, MEM_HEAD_DIM)
    return (y_p.reshape(BATCH, SEQ, D), y_s.reshape(DEC_BATCH, DEC_SEQ, D),
            jnp.stack(ak_p), jnp.stack(av_p), jnp.stack(pool_p),
            mem_kv[:, :, k_cols].reshape(mem_shape), mem_kv[:, :, v_cols].reshape(mem_shape),
            jnp.stack(ak_s), jnp.stack(av_s), jnp.stack(pool_s))
```

```python
import functools

import numpy as np
import jax
import jax.numpy as jnp
from jax import lax
from jax.experimental import pallas as pl
from jax.experimental.pallas import tpu as pltpu

F32 = jnp.float32
BF16 = jnp.bfloat16

D_MODEL = 2048
BATCH = 4
SEQ = 4096
DEPTH = 4
DEC_BATCH = 8
DEC_SEQ = 32
CHUNK = 64
BAND_LEFT = 512
BAND_KEYS = BAND_LEFT + CHUNK
N_HEADS = 16
HEAD_DIM = D_MODEL // N_HEADS
REL_CLIP = 2 * CHUNK
POOL_WINDOWS = (2, 4, 8, 16)
POOL_GROUP = D_MODEL // len(POOL_WINDOWS)
POOL_HIST = max(POOL_WINDOWS) - 1
N_MEM = 256
MEM_HEADS = 4
MEM_HEAD_DIM = D_MODEL // MEM_HEADS
D_FF = 5632
ALPHA = (2.0 * DEPTH) ** 0.25
LN_EPS = 1e-5
NEG_BIG = -1e30

M_PROMPT = BATCH * SEQ
M_SAMPLE = DEC_BATCH * DEC_SEQ
M_ALL = M_PROMPT + M_SAMPLE

TM_INGEST = 1024
TM_MM = 1280
TN_MM = 1024
TN_FF = 512
FF_SUB = 2
TM_LN = 640
TM_LN_PROMPT = 512
TM_LN_WIDE = 256
LN_SLAB = 128
LN_PIECES = 2
QB = 512
PAIR = 2 * CHUNK
PAIR_KEYS = BAND_LEFT + PAIR
HG = 16
TM_MEM = 1024
MEM_SLAB = 256
TM_POOL = 512
HALO = 16

V7X_VMEM_BYTES = 64 * 1024 * 1024


def _vmem_limit(nbytes):
    return int(min(nbytes + (12 << 20), V7X_VMEM_BYTES - (6 << 20)))


def _params(sem, nbytes):
    return pltpu.CompilerParams(dimension_semantics=sem, vmem_limit_bytes=_vmem_limit(nbytes))


def _ingest_kernel(*refs):
    x_ref, o_ref = refs[-2:]
    o_ref[...] = x_ref[...].astype(BF16)


def _ingest(xp, xs):
    def call(x, tm, o_blk0, into):
        n_alias = len(into)
        return pl.pallas_call(
            _ingest_kernel,
            name="ingest",
            grid=(x.shape[0] // tm,),
            in_specs=[pl.BlockSpec(memory_space=pl.ANY)] * n_alias
                     + [pl.BlockSpec((tm, D_MODEL), lambda i: (i, 0))],
            out_specs=pl.BlockSpec((tm, D_MODEL), lambda i: (o_blk0 + i, 0)),
            out_shape=jax.ShapeDtypeStruct((M_ALL, D_MODEL), BF16),
            input_output_aliases={j: j for j in range(n_alias)},
            compiler_params=_params(("arbitrary",), 2 * tm * D_MODEL * (4 + 2)),
        )(*into, x)

    out = call(xp, TM_INGEST, 0, ())
    return call(xs, M_SAMPLE, M_PROMPT // M_SAMPLE, (out,))


def _cast_rider(cast, n_inner, n_steps):
    if cast is None:
        return [], [], [], 0
    w, layer = cast
    _, k, d = w.shape
    rows = next(r for r in (16, 32, 64, 128, 256, 512) if k % r == 0 and k // r <= n_steps)
    last = k // rows - 1
    chunk = lambda j, i: jnp.minimum(j * n_inner + i, last)
    return ([pl.BlockSpec((None, rows, d), lambda j, i: (layer, chunk(j, i), 0))],
            [pl.BlockSpec((rows, d), lambda j, i: (chunk(j, i), 0))],
            [jax.ShapeDtypeStruct((k, d), BF16)], 2 * rows * d * 6)


def _mm_kernel(*refs, scale, n_out, rider):
    n_in = 3 if rider else 2
    a_ref, w_ref = refs[:2]
    o_refs = refs[n_in:n_in + n_out]
    wb_ref = refs[-1]
    if rider:
        refs[n_in + n_out][...] = refs[2][...].astype(BF16)

    @pl.when(pl.program_id(1) == 0)
    def _():
        wb_ref[...] = w_ref[...].astype(BF16)

    acc = jnp.dot(a_ref[...], wb_ref[...], preferred_element_type=F32)
    if scale != 1.0:
        acc = acc * scale
    for o_ref in o_refs:
        o_ref[...] = acc.astype(o_ref.dtype)


def _matmul(a, w, layer, col0, n_out, tm, out_dtypes, scale=1.0, cast=None):
    m, k = a.shape
    tn = TN_MM
    assert m % tm == 0 and n_out % tn == 0 and col0 % tn == 0
    cb = col0 // tn
    grid = (n_out // tn, m // tm)
    r_in, r_out, r_shape, r_bytes = _cast_rider(cast, grid[1], grid[0] * grid[1])
    osz = sum(jnp.dtype(d).itemsize for d in out_dtypes)
    nbytes = 2 * tm * k * 2 + 2 * k * tn * 4 + k * tn * 2 + 2 * tm * tn * osz + tm * tn * 4 + r_bytes
    return pl.pallas_call(
        functools.partial(_mm_kernel, scale=scale, n_out=len(out_dtypes), rider=cast is not None),
        name="mm",
        grid=grid,
        in_specs=[pl.BlockSpec((tm, k), lambda j, i: (i, 0)),
                  pl.BlockSpec((None, k, tn), lambda j, i: (layer, 0, j + cb))] + r_in,
        out_specs=[pl.BlockSpec((tm, tn), lambda j, i: (i, j)) for _ in out_dtypes] + r_out,
        out_shape=[jax.ShapeDtypeStruct((m, n_out), d) for d in out_dtypes] + r_shape,
        scratch_shapes=[pltpu.VMEM((k, tn), BF16)],
        compiler_params=_params(("arbitrary", "arbitrary"), nbytes),
    )(a, w, *(cast[:1] if cast else ()))


def _mem_kv_kernel(a_ref, w_ref, o_ref):
    o_ref[...] = jnp.dot(a_ref[...], w_ref[...].astype(BF16), preferred_element_type=F32)


def _mem_kv_all(mem, w):
    m, k = mem.shape
    n = w.shape[2]
    tn = TN_MM
    nbytes = 2 * m * k * 2 + 2 * k * tn * 4 + k * tn * 2 + 2 * m * tn * 4 + m * tn * 4
    return pl.pallas_call(
        _mem_kv_kernel,
        name="mem_kv",
        grid=(w.shape[0], n // tn),
        in_specs=[pl.BlockSpec((m, k), lambda l, j: (0, 0)),
                  pl.BlockSpec((None, k, tn), lambda l, j: (l, 0, j))],
        out_specs=pl.BlockSpec((None, m, tn), lambda l, j: (l, 0, j)),
        out_shape=jax.ShapeDtypeStruct((w.shape[0], m, n), F32),
        compiler_params=_params(("arbitrary", "arbitrary"), nbytes),
    )(mem, w)


def _swiglu_kernel(a_ref, wg_ref, wu_ref, wc_ref, o_ref, wcb_ref, wgb_ref, wub_ref):
    wcb_ref[...] = wc_ref[...].astype(BF16)

    @pl.when(pl.program_id(1) == 0)
    def _():
        wgb_ref[...] = wg_ref[...].astype(BF16)
        wub_ref[...] = wu_ref[...].astype(BF16)

    ts = a_ref.shape[0] // FF_SUB

    def project(r):
        a = a_ref[r * ts:(r + 1) * ts, :]
        return (jnp.dot(a, wgb_ref[...], preferred_element_type=F32),
                jnp.dot(a, wub_ref[...], preferred_element_type=F32))

    def finish(r, gate, up):
        o_ref[r * ts:(r + 1) * ts, :] = (gate / (1.0 + jnp.exp(-gate)) * up).astype(o_ref.dtype)

    pending = None
    for r in range(FF_SUB):
        gate_up = project(r)
        if pending is not None:
            finish(*pending)
        pending = (r, *gate_up)
    finish(*pending)


def _swiglu_in(a, w_in, w_out, layer):
    m, k = a.shape
    tm, tn = TM_MM, TN_FF
    nj = D_FF // tn
    grid = (nj, m // tm)
    r_in, r_out, r_shape, r_bytes = _cast_rider((w_out, layer), grid[1], grid[0] * grid[1])
    nbytes = (2 * tm * k * 2 + 4 * k * tn * 4 + 2 * k * tn * 2 + 2 * tm * tn * 2 + 3 * tm * tn * 4
              + r_bytes)
    return pl.pallas_call(
        _swiglu_kernel,
        name="swiglu_in",
        grid=grid,
        in_specs=[pl.BlockSpec((tm, k), lambda j, i: (i, 0)),
                  pl.BlockSpec((None, k, tn), lambda j, i: (layer, 0, j)),
                  pl.BlockSpec((None, k, tn), lambda j, i: (layer, 0, j + nj))] + r_in,
        out_specs=[pl.BlockSpec((tm, tn), lambda j, i: (i, j))] + r_out,
        out_shape=[jax.ShapeDtypeStruct((m, D_FF), BF16)] + r_shape,
        scratch_shapes=[pltpu.VMEM((k, tn), BF16), pltpu.VMEM((k, tn), BF16)],
        compiler_params=_params(("arbitrary", "arbitrary"), nbytes),
    )(a, w_in, w_in, w_out)


def _layer_norm(z, g, b):
    mu = jnp.mean(z, axis=-1, keepdims=True)
    zc = z - mu
    var = jnp.mean(zc * zc, axis=-1, keepdims=True)
    return zc * lax.rsqrt(var + LN_EPS) * g + b


def _mm_ln_kernel(*refs, n_alias, n_sub):
    a_ref, w_ref, x_ref, g_ref, b_ref = refs[n_alias:n_alias + 5]
    o_refs = refs[n_alias + 5:-1]
    z_ref = refs[-1]
    ts = z_ref.shape[1]
    tc = D_MODEL // LN_PIECES
    tr = ts // LN_PIECES

    def pre_norm(r, c):
        rs, cs = slice(r * ts, (r + 1) * ts), slice(c * tc, (c + 1) * tc)
        z_ref[r, :, cs] = ALPHA * x_ref[rs, cs] + jnp.dot(a_ref[rs, :], w_ref[:, cs],
                                                         preferred_element_type=F32)

    def finish(r, c):
        y = _layer_norm(z_ref[r, c * tr:(c + 1) * tr, :], g_ref[...], b_ref[...])
        for o_ref in o_refs:
            o_ref[r * ts + c * tr:r * ts + (c + 1) * tr, :] = y.astype(o_ref.dtype)

    for r in range(n_sub + 1):
        for c in range(LN_PIECES):
            if r < n_sub:
                pre_norm(r, c)
            if r > 0:
                finish(r - 1, c)


def _ln_param_specs(gi):
    return [pl.BlockSpec((None, 1, D_MODEL), lambda *_: (gi, 0, 0)),
            pl.BlockSpec((None, 1, D_MODEL), lambda *_: (gi, 0, 0))]


def _mm_ln(a, w, layer, x, gb, gi, *, tm, n_tiles, a_blk0, x_blk0, o_blk0, out_rows, out_dtypes, into=None):
    kdim = a.shape[1]
    assert tm % LN_SLAB == 0
    n_sub = tm // LN_SLAB
    n_alias = 0 if into is None else len(into)
    if layer is None:
        w_spec = pl.BlockSpec((kdim, D_MODEL), lambda i: (0, 0), pipeline_mode=pl.Buffered(1))
    else:
        w_spec = pl.BlockSpec((None, kdim, D_MODEL), lambda i: (layer, 0, 0), pipeline_mode=pl.Buffered(1))
    osz = sum(jnp.dtype(d).itemsize for d in out_dtypes)
    nbytes = (2 * tm * kdim * 2 + kdim * D_MODEL * 2 + 2 * tm * D_MODEL * 4 + 2 * tm * D_MODEL * osz
              + tm * D_MODEL * 4 + 2 * LN_SLAB * D_MODEL * 4)
    return pl.pallas_call(
        functools.partial(_mm_ln_kernel, n_alias=n_alias, n_sub=n_sub),
        name="mm_ln",
        grid=(n_tiles,),
        in_specs=[pl.BlockSpec(memory_space=pl.ANY)] * n_alias + [
            pl.BlockSpec((tm, kdim), lambda i: (a_blk0 + i, 0)),
            w_spec,
            pl.BlockSpec((tm, D_MODEL), lambda i: (x_blk0 + i, 0))] + _ln_param_specs(gi),
        out_specs=[pl.BlockSpec((tm, D_MODEL), lambda i: (o_blk0 + i, 0)) for _ in out_dtypes],
        out_shape=[jax.ShapeDtypeStruct((out_rows, D_MODEL), d) for d in out_dtypes],
        input_output_aliases={j: j for j in range(n_alias)},
        scratch_shapes=[pltpu.VMEM((n_sub, LN_SLAB, D_MODEL), F32)],
        compiler_params=_params(("arbitrary",), nbytes),
    )(*(into or ()), a, w, x, *gb)


def _mm_ln_all(a, w, layer, x, gb, gi):
    tm = TM_LN if a.shape[1] <= D_MODEL else TM_LN_WIDE
    return _mm_ln(a, w, layer, x, gb, gi, tm=tm, n_tiles=M_ALL // tm, a_blk0=0, x_blk0=0, o_blk0=0,
                  out_rows=M_ALL, out_dtypes=(F32, BF16))


def _dot_nt(a, b):
    return lax.dot_general(a, b, (((1,), (1,)), ((), ())), preferred_element_type=F32)


def _band_attn_kernel(q_ref, kl_ref, kc_ref, vl_ref, vc_ref, bias_ref, o_ref, k_sc, v_sc):
    hg = pl.program_id(2)
    k_sc[0:QB, :] = kl_ref[...]
    k_sc[QB:2 * QB, :] = kc_ref[...]
    v_sc[0:QB, :] = vl_ref[...]
    v_sc[QB:2 * QB, :] = vc_ref[...]
    no_left = pl.program_id(1) == 0
    col = lax.broadcasted_iota(jnp.int32, (PAIR, PAIR_KEYS), 1)

    def scores(h, r0):
        sl = slice(h * HEAD_DIM, (h + 1) * HEAD_DIM)
        s = _dot_nt(q_ref[r0:r0 + PAIR, sl], k_sc[r0:r0 + PAIR_KEYS, sl]) + bias_ref[hg * HG + h]
        return jnp.where(jnp.logical_and(no_left, col < QB - r0), NEG_BIG, s)

    def weights(s):
        m = jnp.max(s, axis=-1, keepdims=True)
        p = jnp.exp(s - m)
        return p.astype(BF16), jnp.sum(p, axis=-1, keepdims=True)

    def finish(h, r0, p, denom):
        sl = slice(h * HEAD_DIM, (h + 1) * HEAD_DIM)
        o = jnp.dot(p, v_sc[r0:r0 + PAIR_KEYS, sl], preferred_element_type=F32)
        o_ref[r0:r0 + PAIR, sl] = (o / denom).astype(o_ref.dtype)

    units = [(h, pr * PAIR) for h in range(HG) for pr in range(QB // PAIR)]
    scored, weighted = None, None
    for unit in units + [None, None]:
        s = scores(*unit) if unit is not None else None
        nxt = (*scored[0], *weights(scored[1])) if scored is not None else None
        if weighted is not None:
            finish(*weighted)
        scored = (unit, s) if unit is not None else None
        weighted = nxt


def _band_attn_prompt(q, kv, bias, a):
    nb = SEQ // QB
    w = HG * HEAD_DIM
    vb = D_MODEL // w
    cur = lambda b, i, g: (b * nb + i, g)
    left = lambda b, i, g: (b * nb + jnp.maximum(i - 1, 0), g)
    cur_v = lambda b, i, g: (b * nb + i, vb + g)
    left_v = lambda b, i, g: (b * nb + jnp.maximum(i - 1, 0), vb + g)
    nbytes = (2 * 6 * QB * w * 2 + 2 * N_HEADS * PAIR * PAIR_KEYS * 4 + 4 * QB * w * 2
              + 8 * PAIR * PAIR_KEYS * 4)
    return pl.pallas_call(
        _band_attn_kernel,
        name="band_attn_prompt",
        grid=(BATCH, nb, N_HEADS // HG),
        in_specs=[pl.BlockSpec((QB, w), cur),
                  pl.BlockSpec((QB, w), left),
                  pl.BlockSpec((QB, w), cur),
                  pl.BlockSpec((QB, w), left_v),
                  pl.BlockSpec((QB, w), cur_v),
                  pl.BlockSpec((None, N_HEADS, PAIR, PAIR_KEYS), lambda b, i, g: (a, 0, 0, 0))],
        out_specs=pl.BlockSpec((QB, w), cur),
        out_shape=jax.ShapeDtypeStruct((M_ALL, D_MODEL), BF16),
        scratch_shapes=[pltpu.VMEM((2 * QB, w), BF16), pltpu.VMEM((2 * QB, w), BF16)],
        compiler_params=_params(("arbitrary", "arbitrary", "arbitrary"), nbytes),
    )(q, kv, kv, kv, kv, bias)


def _band_attn_sample_kernel(o_any, q_ref, kn_ref, vn_ref, ck_ref, cv_ref, bias_ref, o_ref):
    del o_any
    past = ck_ref.shape[0] // N_HEADS
    for h in range(N_HEADS):
        sl = slice(h * HEAD_DIM, (h + 1) * HEAD_DIM)
        rows = pl.ds(h, past, stride=N_HEADS)
        q = q_ref[:, sl]
        s_c = _dot_nt(q, ck_ref[rows, :].astype(BF16)) + bias_ref[h, 0:DEC_SEQ, 0:past]
        s_n = _dot_nt(q, kn_ref[:, sl]) + bias_ref[h, 0:DEC_SEQ, past:past + DEC_SEQ]
        m = jnp.maximum(jnp.max(s_c, axis=-1, keepdims=True), jnp.max(s_n, axis=-1, keepdims=True))
        p_c = jnp.exp(s_c - m)
        p_n = jnp.exp(s_n - m)
        denom = jnp.sum(p_c, axis=-1, keepdims=True) + jnp.sum(p_n, axis=-1, keepdims=True)
        o = (jnp.dot(p_c.astype(BF16), cv_ref[rows, :].astype(BF16), preferred_element_type=F32)
             + jnp.dot(p_n.astype(BF16), vn_ref[:, sl], preferred_element_type=F32))
        o_ref[:, sl] = (o / denom).astype(o_ref.dtype)


def _band_attn_sample(o, q, kv, cache_k, cache_v, bias, a):
    t = DEC_SEQ
    rb = M_PROMPT // t
    cr = BAND_LEFT * N_HEADS
    nbytes = (2 * (4 * t * D_MODEL * 2 + 2 * cr * HEAD_DIM * 4 + N_HEADS * PAIR * PAIR_KEYS * 4)
              + 4 * BAND_LEFT * HEAD_DIM * 4)
    return pl.pallas_call(
        _band_attn_sample_kernel,
        name="band_attn_sample",
        grid=(DEC_BATCH,),
        in_specs=[pl.BlockSpec(memory_space=pl.ANY),
                  pl.BlockSpec((t, D_MODEL), lambda b: (rb + b, 0)),
                  pl.BlockSpec((t, D_MODEL), lambda b: (rb + b, 0)),
                  pl.BlockSpec((t, D_MODEL), lambda b: (rb + b, 1)),
                  pl.BlockSpec((cr, HEAD_DIM), lambda b: (a * DEC_BATCH + b, 0)),
                  pl.BlockSpec((cr, HEAD_DIM), lambda b: (a * DEC_BATCH + b, 0)),
                  pl.BlockSpec((None, N_HEADS, PAIR, PAIR_KEYS), lambda b: (a, 0, 0, 0))],
        out_specs=pl.BlockSpec((t, D_MODEL), lambda b: (rb + b, 0)),
        out_shape=jax.ShapeDtypeStruct(o.shape, o.dtype),
        input_output_aliases={0: 0},
        compiler_params=_params(("arbitrary",), nbytes),
    )(o, q, kv, kv, cache_k, cache_v, bias)


def _mem_attn_body(q_ref, k_ref, v_ref, o_ref):
    ts = min(q_ref.shape[0], MEM_SLAB)
    head = lambda h: slice(h * MEM_HEAD_DIM, (h + 1) * MEM_HEAD_DIM)
    kb = [k_ref[:, head(h)].astype(BF16) for h in range(MEM_HEADS)]
    vb = [v_ref[:, head(h)].astype(BF16) for h in range(MEM_HEADS)]

    def scores(h, r0):
        return _dot_nt(q_ref[r0:r0 + ts, head(h)], kb[h])

    def finish(h, r0, s):
        m = jnp.max(s, axis=-1, keepdims=True)
        p = jnp.exp(s - m)
        denom = jnp.sum(p, axis=-1, keepdims=True)
        o = jnp.dot(p.astype(BF16), vb[h], preferred_element_type=F32)
        o_ref[r0:r0 + ts, head(h)] = (o / denom).astype(o_ref.dtype)

    units = [(h, r0) for h in range(MEM_HEADS) for r0 in range(0, q_ref.shape[0], ts)]
    pending = None
    for h, r0 in units:
        s = scores(h, r0)
        if pending is not None:
            finish(*pending)
        pending = (h, r0, s)
    finish(*pending)


def _mem_attn_prompt_kernel(q_ref, k_ref, v_ref, o_ref):
    _mem_attn_body(q_ref, k_ref, v_ref, o_ref)


def _mem_attn_sample_kernel(o_any, q_ref, k_ref, v_ref, o_ref):
    del o_any
    _mem_attn_body(q_ref, k_ref, v_ref, o_ref)


def _mem_attn_prompt(q, mem_kv, layer):
    tm = TM_MEM
    nt = SEQ // tm
    nbytes = 2 * (tm * D_MODEL * 2 + 2 * N_MEM * D_MODEL * 4 + tm * D_MODEL * 2) + 6 * tm * N_MEM * 4
    return pl.pallas_call(
        _mem_attn_prompt_kernel,
        name="mem_attn_prompt",
        grid=(BATCH, nt),
        in_specs=[pl.BlockSpec((tm, D_MODEL), lambda b, i: (b * nt + i, 0)),
                  pl.BlockSpec((None, N_MEM, D_MODEL), lambda b, i: (layer, b, 0)),
                  pl.BlockSpec((None, N_MEM, D_MODEL), lambda b, i: (layer, b, 1))],
        out_specs=pl.BlockSpec((tm, D_MODEL), lambda b, i: (b * nt + i, 0)),
        out_shape=jax.ShapeDtypeStruct((M_ALL, D_MODEL), BF16),
        compiler_params=_params(("arbitrary", "arbitrary"), nbytes),
    )(q, mem_kv, mem_kv)


def _mem_attn_sample(o, q, cache_k, cache_v, layer):
    t = DEC_SEQ
    rb = M_PROMPT // t
    nbytes = 2 * (2 * t * D_MODEL * 2 + 2 * N_MEM * D_MODEL * 4) + 6 * t * N_MEM * 4
    return pl.pallas_call(
        _mem_attn_sample_kernel,
        name="mem_attn_sample",
        grid=(DEC_BATCH,),
        in_specs=[pl.BlockSpec(memory_space=pl.ANY),
                  pl.BlockSpec((t, D_MODEL), lambda b: (rb + b, 0)),
                  pl.BlockSpec((N_MEM, D_MODEL), lambda b: (layer * DEC_BATCH + b, 0)),
                  pl.BlockSpec((N_MEM, D_MODEL), lambda b: (layer * DEC_BATCH + b, 0))],
        out_specs=pl.BlockSpec((t, D_MODEL), lambda b: (rb + b, 0)),
        out_shape=jax.ShapeDtypeStruct(o.shape, o.dtype),
        input_output_aliases={0: 0},
        compiler_params=_params(("arbitrary",), nbytes),
    )(o, q, cache_k, cache_v)


def _pool_body(x_ref, halo, pos0, n_hist, wp_ref, ps_ref, g_ref, b_ref, of_ref, ob_ref,
               ext_ref, sa_ref, sb_ref):
    tm = x_ref.shape[0]
    top = 2 * HALO
    rows = tm + top
    zeros = jnp.zeros((HALO, D_MODEL), F32)
    ext_ref[0:HALO, :] = zeros
    ext_ref[HALO:top, :] = halo
    ext_ref[top:rows, :] = x_ref[...]
    sa_ref[0:HALO, :] = zeros[:, :POOL_GROUP]
    sb_ref[0:HALO, :] = zeros[:, :POOL_GROUP]
    seen = (pos0 + n_hist + 1 + lax.broadcasted_iota(jnp.int32, (tm, 1), 0)).astype(F32)
    bufs = (sa_ref, sb_ref)
    for gi, win in enumerate(POOL_WINDOWS):
        cs = slice(gi * POOL_GROUP, (gi + 1) * POOL_GROUP)
        src, d, lvl = None, 1, 0
        while 2 * d < win:
            if src is None:
                val = ext_ref[HALO:rows, cs] + ext_ref[HALO - d:rows - d, cs]
            else:
                val = src[HALO:rows, :] + src[HALO - d:rows - d, :]
            dst = bufs[lvl % 2]
            dst[HALO:rows, :] = val
            src, d, lvl = dst, 2 * d, lvl + 1
        if src is None:
            wsum = ext_ref[top:rows, cs] + ext_ref[top - d:rows - d, cs]
        else:
            wsum = src[top:rows, :] + src[top - d:rows - d, :]
        xg = x_ref[:, cs]
        u = wsum / jnp.minimum(seen, float(win)) - xg
        y = jnp.dot(u.astype(BF16), wp_ref[gi], preferred_element_type=F32) * ps_ref[:, cs]
        of_ref[:, cs] = ALPHA * xg + y
    out = _layer_norm(of_ref[...], g_ref[...], b_ref[...])
    of_ref[...] = out
    ob_ref[...] = out.astype(BF16)


def _pool_prompt_kernel(x_ref, halo_ref, wp_ref, ps_ref, g_ref, b_ref, of_ref, ob_ref,
                        ext_ref, sa_ref, sb_ref):
    it = pl.program_id(0) % (SEQ // TM_POOL)
    halo = jnp.where(it > 0, halo_ref[...], 0.0)
    _pool_body(x_ref, halo, it * TM_POOL, 0, wp_ref, ps_ref, g_ref, b_ref, of_ref, ob_ref,
               ext_ref, sa_ref, sb_ref)


def _pool_sample_kernel(of_any, ob_any, x_ref, halo_ref, wp_ref, ps_ref, g_ref, b_ref, of_ref, ob_ref,
                        ext_ref, sa_ref, sb_ref):
    del of_any, ob_any
    _pool_body(x_ref, halo_ref[...], 0, POOL_HIST, wp_ref, ps_ref, g_ref, b_ref, of_ref, ob_ref,
               ext_ref, sa_ref, sb_ref)


def _pool_scratch(tm):
    return [pltpu.VMEM((tm + 2 * HALO, D_MODEL), F32),
            pltpu.VMEM((tm + 2 * HALO, POOL_GROUP), F32),
            pltpu.VMEM((tm + 2 * HALO, POOL_GROUP), F32)]


def _pool_const_specs(p, gi):
    return [pl.BlockSpec((None, len(POOL_WINDOWS), POOL_GROUP, POOL_GROUP), lambda i: (p, 0, 0, 0)),
            pl.BlockSpec((None, 1, D_MODEL), lambda i: (p, 0, 0))] + _ln_param_specs(gi)


def _pool_bytes(tm):
    return (2 * tm * D_MODEL * 4 + 2 * HALO * D_MODEL * 4 + 2 * D_MODEL * POOL_GROUP * 2 + 2 * tm * D_MODEL * 6
            + (tm + 2 * HALO) * (D_MODEL + 2 * POOL_GROUP) * 4 + 2 * tm * D_MODEL * 4)


def _pool_prompt(x, wp, ps, p, gb, gi):
    tm = TM_POOL
    hb = tm // HALO
    return pl.pallas_call(
        _pool_prompt_kernel,
        name="pool_prompt",
        grid=(M_PROMPT // tm,),
        in_specs=[pl.BlockSpec((tm, D_MODEL), lambda i: (i, 0)),
                  pl.BlockSpec((HALO, D_MODEL), lambda i: (jnp.maximum(i * hb - 1, 0), 0))]
                 + _pool_const_specs(p, gi),
        out_specs=[pl.BlockSpec((tm, D_MODEL), lambda i: (i, 0)),
                   pl.BlockSpec((tm, D_MODEL), lambda i: (i, 0))],
        out_shape=[jax.ShapeDtypeStruct((M_ALL, D_MODEL), F32),
                   jax.ShapeDtypeStruct((M_ALL, D_MODEL), BF16)],
        scratch_shapes=_pool_scratch(tm),
        compiler_params=_params(("arbitrary",), _pool_bytes(tm)),
    )(x, x, wp, ps, *gb)


def _pool_sample(of, ob, x, state, wp, ps, p, gb, gi):
    tm = DEC_SEQ
    rb = M_PROMPT // tm
    return pl.pallas_call(
        _pool_sample_kernel,
        name="pool_sample",
        grid=(DEC_BATCH,),
        in_specs=[pl.BlockSpec(memory_space=pl.ANY),
                  pl.BlockSpec(memory_space=pl.ANY),
                  pl.BlockSpec((tm, D_MODEL), lambda i: (rb + i, 0)),
                  pl.BlockSpec((None, HALO, D_MODEL), lambda i: (p * DEC_BATCH + i, 0, 0))]
                 + _pool_const_specs(p, gi),
        out_specs=[pl.BlockSpec((tm, D_MODEL), lambda i: (rb + i, 0)),
                   pl.BlockSpec((tm, D_MODEL), lambda i: (rb + i, 0))],
        out_shape=[jax.ShapeDtypeStruct(of.shape, of.dtype),
                   jax.ShapeDtypeStruct(ob.shape, ob.dtype)],
        input_output_aliases={0: 0, 1: 1},
        scratch_shapes=_pool_scratch(tm),
        compiler_params=_params(("arbitrary",), _pool_bytes(tm)),
    )(of, ob, x, state, wp, ps, *gb)


def _pair_bias(table):
    lead = table.shape[:-1]
    period = PAIR + PAIR_KEYS
    far = table[..., 2 * REL_CLIP:]
    vec = jnp.concatenate([jnp.broadcast_to(far, lead + (BAND_LEFT - REL_CLIP,)),
                           jnp.flip(table[..., 1:], axis=-1),
                           jnp.broadcast_to(far, lead + (period - BAND_LEFT - REL_CLIP,))], axis=-1)
    flat = jnp.tile(vec, (1,) * len(lead) + (PAIR,))[..., :PAIR * (period - 1)]
    toe = flat.reshape(lead + (PAIR, period - 1))[..., :PAIR_KEYS]
    r = np.arange(PAIR)[:, None]
    c = np.arange(PAIR_KEYS)[None, :]
    off = c - (r // CHUNK) * CHUNK
    in_band = np.logical_and(off >= 0, off < BAND_KEYS)
    return jnp.where(in_band, toe, NEG_BIG).astype(F32)


def _tail_rows(x, n_tail, cols):
    return jnp.stack([x[(bi + 1) * SEQ - n_tail:(bi + 1) * SEQ, cols] for bi in range(BATCH)])


def kernel(x_prompt, x_sample, cache_attn_k, cache_attn_v, state_pool, cache_mem_k, cache_mem_v,
           mem_prompt, w_qkv, w_attn_o, rel_bias, w_pool, pool_scale, w_mem_q, w_mem_kv, w_mem_o,
           w_ffn_in, w_ffn_out, ln_g, ln_b):
    D = D_MODEL
    assert cache_attn_k.shape[2] == BAND_LEFT and DEPTH % 2 == 0
    xp = x_prompt.reshape(M_PROMPT, D)
    xs = x_sample.reshape(M_SAMPLE, D)
    x_f = None
    x_b = _ingest(xp, xs)
    mem_b = mem_prompt.reshape(BATCH * N_MEM, D).astype(BF16)
    w_pool_b = w_pool.astype(BF16)
    gb = (ln_g.reshape(DEPTH * 3, 1, D), ln_b.reshape(DEPTH * 3, 1, D))
    pscale = pool_scale.reshape(-1, 1, D)
    cache_ak = cache_attn_k.reshape(-1, HEAD_DIM)
    cache_av = cache_attn_v.reshape(-1, HEAD_DIM)
    cache_mk = cache_mem_k.reshape(-1, D)
    cache_mv = cache_mem_v.reshape(-1, D)
    state_pad = jnp.pad(state_pool, ((0, 0), (0, 0), (HALO - POOL_HIST, 0), (0, 0))).reshape(-1, HALO, D)
    bias = _pair_bias(rel_bias)
    mem_kv = _mem_kv_all(mem_b, w_mem_kv)
    k_cols, v_cols = slice(0, D), slice(D, 2 * D)

    ak_p, av_p, pool_p, ak_s, av_s, pool_s = [], [], [], [], [], []
    for i in range(DEPTH):
        if i % 2 == 0:
            a = i // 2
            q, wo_b = _matmul(x_b, w_qkv, a, 0, D, TM_MM, (BF16,), scale=HEAD_DIM ** -0.5, cast=(w_attn_o, a))
            kv, kv_b = _matmul(x_b, w_qkv, a, D, 2 * D, TM_MM, (F32, BF16))
            o = _band_attn_prompt(q, kv_b, bias, a)
            o = _band_attn_sample(o, q, kv_b, cache_ak, cache_av, bias, a)
            ak_p.append(_tail_rows(kv, BAND_LEFT, k_cols).reshape(BATCH, BAND_LEFT, N_HEADS, HEAD_DIM))
            av_p.append(_tail_rows(kv, BAND_LEFT, v_cols).reshape(BATCH, BAND_LEFT, N_HEADS, HEAD_DIM))
            ak_s.append(kv[M_PROMPT:, k_cols].reshape(DEC_BATCH, DEC_SEQ, N_HEADS, HEAD_DIM))
            av_s.append(kv[M_PROMPT:, v_cols].reshape(DEC_BATCH, DEC_SEQ, N_HEADS, HEAD_DIM))
            if i == 0:
                outs = _mm_ln(o, wo_b, None, xp, gb, 0, tm=TM_LN_PROMPT, n_tiles=M_PROMPT // TM_LN_PROMPT,
                              a_blk0=0, x_blk0=0, o_blk0=0, out_rows=M_ALL, out_dtypes=(F32, BF16))
                x_f, x_b = _mm_ln(o, wo_b, None, xs, gb, 0, tm=M_SAMPLE, n_tiles=1,
                                  a_blk0=M_PROMPT // M_SAMPLE, x_blk0=0, o_blk0=M_PROMPT // M_SAMPLE,
                                  out_rows=M_ALL, out_dtypes=(F32, BF16), into=outs)
            else:
                x_f, x_b = _mm_ln_all(o, wo_b, None, x_f, gb, 3 * i)
        else:
            p = i // 2
            pool_p.append(_tail_rows(x_f, POOL_HIST, slice(None)))
            ext = jnp.concatenate([state_pool[p], x_f[M_PROMPT:].reshape(DEC_BATCH, DEC_SEQ, D)], axis=1)
            pool_s.append(ext[:, DEC_SEQ:])
            y_f, y_b = _pool_prompt(x_f, w_pool_b, pscale, p, gb, 3 * i)
            x_f, x_b = _pool_sample(y_f, y_b, x_f, state_pad, w_pool_b, pscale, p, gb, 3 * i)
        q, wmo_b = _matmul(x_b, w_mem_q, i, 0, D, TM_MM, (BF16,), scale=MEM_HEAD_DIM ** -0.5, cast=(w_mem_o, i))
        o = _mem_attn_prompt(q, mem_kv, i)
        o = _mem_attn_sample(o, q, cache_mk, cache_mv, i)
        x_f, x_b = _mm_ln_all(o, wmo_b, None, x_f, gb, 3 * i + 1)
        h, wfo_b = _swiglu_in(x_b, w_ffn_in, w_ffn_out, i)
        if i < DEPTH - 1:
            x_f, x_b = _mm_ln_all(h, wfo_b, None, x_f, gb, 3 * i + 2)
        else:
            y_p, = _mm_ln(h, wfo_b, None, x_f, gb, 3 * i + 2, tm=TM_LN_WIDE,
                          n_tiles=M_PROMPT // TM_LN_WIDE, a_blk0=0, x_blk0=0, o_blk0=0,
                          out_rows=M_PROMPT, out_dtypes=(F32,))
            y_s, = _mm_ln(h, wfo_b, None, x_f, gb, 3 * i + 2, tm=M_SAMPLE, n_tiles=1,
                          a_blk0=M_PROMPT // M_SAMPLE, x_blk0=M_PROMPT // M_SAMPLE, o_blk0=0,
                          out_rows=M_SAMPLE, out_dtypes=(F32,))

    mem_shape = (DEPTH, BATCH, N_MEM, MEM_HEADS, MEM_HEAD_DIM)
    return (y_p.reshape(BATCH, SEQ, D), y_s.reshape(DEC_BATCH, DEC_SEQ, D),
            jnp.stack(ak_p), jnp.stack(av_p), jnp.stack(pool_p),
            mem_kv[:, :, k_cols].reshape(mem_shape), mem_kv[:, :, v_cols].reshape(mem_shape),
            jnp.stack(ak_s), jnp.stack(av_s), jnp.stack(pool_s))
```

```python
import functools

import numpy as np
import jax
import jax.numpy as jnp
from jax import lax
from jax.experimental import pallas as pl
from jax.experimental.pallas import tpu as pltpu

F32 = jnp.float32
BF16 = jnp.bfloat16

D_MODEL = 2048
BATCH = 4
SEQ = 4096
DEPTH = 4
DEC_BATCH = 8
DEC_SEQ = 32
CHUNK = 64
BAND_LEFT = 512
BAND_KEYS = BAND_LEFT + CHUNK
N_HEADS = 16
HEAD_DIM = D_MODEL // N_HEADS
REL_CLIP = 2 * CHUNK
POOL_WINDOWS = (2, 4, 8, 16)
POOL_GROUP = D_MODEL // len(POOL_WINDOWS)
POOL_HIST = max(POOL_WINDOWS) - 1
N_MEM = 256
MEM_HEADS = 4
MEM_HEAD_DIM = D_MODEL // MEM_HEADS
D_FF = 5632
ALPHA = (2.0 * DEPTH) ** 0.25
LN_EPS = 1e-5
NEG_BIG = -1e30

M_PROMPT = BATCH * SEQ
M_SAMPLE = DEC_BATCH * DEC_SEQ
M_ALL = M_PROMPT + M_SAMPLE

TM_INGEST = 1024
TM_MM = 1280
TN_MM = 1024
TN_FF = 512
FF_SUB = 2
TM_LN = 640
TM_LN_PROMPT = 512
TM_LN_WIDE = 256
LN_SLAB = 128
LN_PIECES = 2
QB = 512
PAIR = 2 * CHUNK
PAIR_KEYS = BAND_LEFT + PAIR
TM_MEM = 1024
MEM_SLAB = 256
TM_POOL = 512
HALO = 16

V7X_VMEM_BYTES = 64 * 1024 * 1024


def _vmem_limit(nbytes):
    return int(min(nbytes + (12 << 20), V7X_VMEM_BYTES - (6 << 20)))


def _params(sem, nbytes):
    return pltpu.CompilerParams(dimension_semantics=sem, vmem_limit_bytes=_vmem_limit(nbytes))


def _ingest_kernel(*refs):
    x_ref, o_ref = refs[-2:]
    o_ref[...] = x_ref[...].astype(BF16)


def _ingest(xp, xs):
    def call(x, tm, o_blk0, into):
        n_alias = len(into)
        return pl.pallas_call(
            _ingest_kernel,
            name="ingest",
            grid=(x.shape[0] // tm,),
            in_specs=[pl.BlockSpec(memory_space=pl.ANY)] * n_alias
                     + [pl.BlockSpec((tm, D_MODEL), lambda i: (i, 0))],
            out_specs=pl.BlockSpec((tm, D_MODEL), lambda i: (o_blk0 + i, 0)),
            out_shape=jax.ShapeDtypeStruct((M_ALL, D_MODEL), BF16),
            input_output_aliases={j: j for j in range(n_alias)},
            compiler_params=_params(("arbitrary",), 2 * tm * D_MODEL * (4 + 2)),
        )(*into, x)

    out = call(xp, TM_INGEST, 0, ())
    return call(xs, M_SAMPLE, M_PROMPT // M_SAMPLE, (out,))


def _cast_rider(cast, n_inner, n_steps):
    if cast is None:
        return [], [], [], 0
    w, layer = cast
    _, k, d = w.shape
    rows = next(r for r in (16, 32, 64, 128, 256, 512) if k % r == 0 and k // r <= n_steps)
    last = k // rows - 1
    chunk = lambda j, i: jnp.minimum(j * n_inner + i, last)
    return ([pl.BlockSpec((None, rows, d), lambda j, i: (layer, chunk(j, i), 0))],
            [pl.BlockSpec((rows, d), lambda j, i: (chunk(j, i), 0))],
            [jax.ShapeDtypeStruct((k, d), BF16)], 2 * rows * d * 6)


def _mm_kernel(*refs, scale, n_out, rider, transposed):
    n_in = 3 if rider else 2
    a_ref, w_ref = refs[:2]
    o_refs = refs[n_in:n_in + n_out]
    wb_ref = refs[-1]
    if rider:
        refs[n_in + n_out + transposed][...] = refs[2][...].astype(BF16)

    @pl.when(pl.program_id(1) == 0)
    def _():
        wb_ref[...] = w_ref[...].astype(BF16)

    acc = jnp.dot(a_ref[...], wb_ref[...], preferred_element_type=F32)
    if scale != 1.0:
        acc = acc * scale
    for o_ref in o_refs:
        o_ref[...] = acc.astype(o_ref.dtype)
    if transposed:
        refs[n_in + n_out][...] = acc.T.astype(BF16)


def _matmul(a, w, layer, col0, n_out, tm, out_dtypes, scale=1.0, cast=None, transposed=False):
    m, k = a.shape
    tn = TN_MM
    assert m % tm == 0 and n_out % tn == 0 and col0 % tn == 0
    cb = col0 // tn
    grid = (n_out // tn, m // tm)
    r_in, r_out, r_shape, r_bytes = _cast_rider(cast, grid[1], grid[0] * grid[1])
    osz = sum(jnp.dtype(d).itemsize for d in out_dtypes)
    t_out = [pl.BlockSpec((tn, tm), lambda j, i: (j, i))] if transposed else []
    t_shape = [jax.ShapeDtypeStruct((n_out, m), BF16)] if transposed else []
    nbytes = (2 * tm * k * 2 + 2 * k * tn * 4 + k * tn * 2 + 2 * tm * tn * osz + tm * tn * 4 + r_bytes
              + transposed * tm * tn * (2 * 2 + 4))
    return pl.pallas_call(
        functools.partial(_mm_kernel, scale=scale, n_out=len(out_dtypes), rider=cast is not None,
                          transposed=transposed),
        name="mm",
        grid=grid,
        in_specs=[pl.BlockSpec((tm, k), lambda j, i: (i, 0)),
                  pl.BlockSpec((None, k, tn), lambda j, i: (layer, 0, j + cb))] + r_in,
        out_specs=[pl.BlockSpec((tm, tn), lambda j, i: (i, j)) for _ in out_dtypes] + t_out + r_out,
        out_shape=[jax.ShapeDtypeStruct((m, n_out), d) for d in out_dtypes] + t_shape + r_shape,
        scratch_shapes=[pltpu.VMEM((k, tn), BF16)],
        compiler_params=_params(("arbitrary", "arbitrary"), nbytes),
    )(a, w, *(cast[:1] if cast else ()))


def _mem_kv_kernel(a_ref, w_ref, o_ref, kt_ref, *, n_key_steps):
    acc = jnp.dot(a_ref[...], w_ref[...].astype(BF16), preferred_element_type=F32)
    o_ref[...] = acc

    @pl.when(pl.program_id(1) < n_key_steps)
    def _():
        kt_ref[...] = acc.T.astype(BF16)


def _mem_kv_all(mem, w):
    m, k = mem.shape
    n = w.shape[2]
    tn = TN_MM
    nk = n // 2 // tn
    nbytes = 2 * m * k * 2 + 2 * k * tn * 4 + k * tn * 2 + 2 * m * tn * 4 + 2 * m * tn * 4 + 2 * m * tn * 2
    return pl.pallas_call(
        functools.partial(_mem_kv_kernel, n_key_steps=nk),
        name="mem_kv",
        grid=(w.shape[0], n // tn),
        in_specs=[pl.BlockSpec((m, k), lambda l, j: (0, 0)),
                  pl.BlockSpec((None, k, tn), lambda l, j: (l, 0, j))],
        out_specs=[pl.BlockSpec((None, m, tn), lambda l, j: (l, 0, j)),
                   pl.BlockSpec((None, tn, m), lambda l, j: (l, jnp.minimum(j, nk - 1), 0))],
        out_shape=[jax.ShapeDtypeStruct((w.shape[0], m, n), F32),
                   jax.ShapeDtypeStruct((w.shape[0], n // 2, m), BF16)],
        compiler_params=_params(("arbitrary", "arbitrary"), nbytes),
    )(mem, w)


def _swiglu_kernel(a_ref, wg_ref, wu_ref, wc_ref, o_ref, wcb_ref, wgb_ref, wub_ref):
    wcb_ref[...] = wc_ref[...].astype(BF16)

    @pl.when(pl.program_id(1) == 0)
    def _():
        wgb_ref[...] = wg_ref[...].astype(BF16)
        wub_ref[...] = wu_ref[...].astype(BF16)

    ts = a_ref.shape[0] // FF_SUB

    def project(r):
        a = a_ref[r * ts:(r + 1) * ts, :]
        return (jnp.dot(a, wgb_ref[...], preferred_element_type=F32),
                jnp.dot(a, wub_ref[...], preferred_element_type=F32))

    def finish(r, gate, up):
        o_ref[r * ts:(r + 1) * ts, :] = (gate / (1.0 + jnp.exp(-gate)) * up).astype(o_ref.dtype)

    pending = None
    for r in range(FF_SUB):
        gate_up = project(r)
        if pending is not None:
            finish(*pending)
        pending = (r, *gate_up)
    finish(*pending)


def _swiglu_in(a, w_in, w_out, layer):
    m, k = a.shape
    tm, tn = TM_MM, TN_FF
    nj = D_FF // tn
    grid = (nj, m // tm)
    r_in, r_out, r_shape, r_bytes = _cast_rider((w_out, layer), grid[1], grid[0] * grid[1])
    nbytes = (2 * tm * k * 2 + 4 * k * tn * 4 + 2 * k * tn * 2 + 2 * tm * tn * 2 + 3 * tm * tn * 4
              + r_bytes)
    return pl.pallas_call(
        _swiglu_kernel,
        name="swiglu_in",
        grid=grid,
        in_specs=[pl.BlockSpec((tm, k), lambda j, i: (i, 0)),
                  pl.BlockSpec((None, k, tn), lambda j, i: (layer, 0, j)),
                  pl.BlockSpec((None, k, tn), lambda j, i: (layer, 0, j + nj))] + r_in,
        out_specs=[pl.BlockSpec((tm, tn), lambda j, i: (i, j))] + r_out,
        out_shape=[jax.ShapeDtypeStruct((m, D_FF), BF16)] + r_shape,
        scratch_shapes=[pltpu.VMEM((k, tn), BF16), pltpu.VMEM((k, tn), BF16)],
        compiler_params=_params(("arbitrary", "arbitrary"), nbytes),
    )(a, w_in, w_in, w_out)


def _layer_norm(z, g, b):
    mu = jnp.mean(z, axis=-1, keepdims=True)
    zc = z - mu
    var = jnp.mean(zc * zc, axis=-1, keepdims=True)
    return zc * lax.rsqrt(var + LN_EPS) * g + b


def _mm_ln_kernel(*refs, n_alias, n_sub):
    a_ref, w_ref, x_ref, g_ref, b_ref = refs[n_alias:n_alias + 5]
    o_refs = refs[n_alias + 5:-1]
    z_ref = refs[-1]
    ts = z_ref.shape[1]
    tc = D_MODEL // LN_PIECES
    tr = ts // LN_PIECES

    def pre_norm(r, c):
        rs, cs = slice(r * ts, (r + 1) * ts), slice(c * tc, (c + 1) * tc)
        z_ref[r, :, cs] = ALPHA * x_ref[rs, cs] + jnp.dot(a_ref[rs, :], w_ref[:, cs],
                                                         preferred_element_type=F32)

    def finish(r, c):
        y = _layer_norm(z_ref[r, c * tr:(c + 1) * tr, :], g_ref[...], b_ref[...])
        for o_ref in o_refs:
            o_ref[r * ts + c * tr:r * ts + (c + 1) * tr, :] = y.astype(o_ref.dtype)

    for r in range(n_sub + 1):
        for c in range(LN_PIECES):
            if r < n_sub:
                pre_norm(r, c)
            if r > 0:
                finish(r - 1, c)


def _ln_param_specs(gi):
    return [pl.BlockSpec((None, 1, D_MODEL), lambda *_: (gi, 0, 0)),
            pl.BlockSpec((None, 1, D_MODEL), lambda *_: (gi, 0, 0))]


def _mm_ln(a, w, layer, x, gb, gi, *, tm, n_tiles, a_blk0, x_blk0, o_blk0, out_rows, out_dtypes, into=None):
    kdim = a.shape[1]
    assert tm % LN_SLAB == 0
    n_sub = tm // LN_SLAB
    n_alias = 0 if into is None else len(into)
    if layer is None:
        w_spec = pl.BlockSpec((kdim, D_MODEL), lambda i: (0, 0), pipeline_mode=pl.Buffered(1))
    else:
        w_spec = pl.BlockSpec((None, kdim, D_MODEL), lambda i: (layer, 0, 0), pipeline_mode=pl.Buffered(1))
    osz = sum(jnp.dtype(d).itemsize for d in out_dtypes)
    nbytes = (2 * tm * kdim * 2 + kdim * D_MODEL * 2 + 2 * tm * D_MODEL * 4 + 2 * tm * D_MODEL * osz
              + tm * D_MODEL * 4 + 2 * LN_SLAB * D_MODEL * 4)
    return pl.pallas_call(
        functools.partial(_mm_ln_kernel, n_alias=n_alias, n_sub=n_sub),
        name="mm_ln",
        grid=(n_tiles,),
        in_specs=[pl.BlockSpec(memory_space=pl.ANY)] * n_alias + [
            pl.BlockSpec((tm, kdim), lambda i: (a_blk0 + i, 0)),
            w_spec,
            pl.BlockSpec((tm, D_MODEL), lambda i: (x_blk0 + i, 0))] + _ln_param_specs(gi),
        out_specs=[pl.BlockSpec((tm, D_MODEL), lambda i: (o_blk0 + i, 0)) for _ in out_dtypes],
        out_shape=[jax.ShapeDtypeStruct((out_rows, D_MODEL), d) for d in out_dtypes],
        input_output_aliases={j: j for j in range(n_alias)},
        scratch_shapes=[pltpu.VMEM((n_sub, LN_SLAB, D_MODEL), F32)],
        compiler_params=_params(("arbitrary",), nbytes),
    )(*(into or ()), a, w, x, *gb)


def _mm_ln_all(a, w, layer, x, gb, gi):
    tm = TM_LN if a.shape[1] <= D_MODEL else TM_LN_WIDE
    return _mm_ln(a, w, layer, x, gb, gi, tm=tm, n_tiles=M_ALL // tm, a_blk0=0, x_blk0=0, o_blk0=0,
                  out_rows=M_ALL, out_dtypes=(F32, BF16))


def _dot_nt(a, b):
    return lax.dot_general(a, b, (((1,), (1,)), ((), ())), preferred_element_type=F32)


def _band_attn_kernel(q_ref, ktl_ref, ktc_ref, vl_ref, vc_ref, bias_ref, o_ref, kt_sc, v_sc):
    kt_sc[:, 0:QB] = ktl_ref[...]
    kt_sc[:, QB:2 * QB] = ktc_ref[...]
    v_sc[0:QB, :] = vl_ref[...]
    v_sc[QB:2 * QB, :] = vc_ref[...]
    no_left = pl.program_id(1) == 0
    col = lax.broadcasted_iota(jnp.int32, (PAIR, PAIR_KEYS), 1)

    def scores(h, r0):
        sl = slice(h * HEAD_DIM, (h + 1) * HEAD_DIM)
        s = jnp.dot(q_ref[r0:r0 + PAIR, sl], kt_sc[sl, r0:r0 + PAIR_KEYS],
                    preferred_element_type=F32) + bias_ref[h]
        return jnp.where(jnp.logical_and(no_left, col < QB - r0), NEG_BIG, s)

    def weights(s):
        m = jnp.max(s, axis=-1, keepdims=True)
        p = jnp.exp(s - m)
        return p.astype(BF16), jnp.sum(p, axis=-1, keepdims=True)

    def finish(h, r0, p, denom):
        sl = slice(h * HEAD_DIM, (h + 1) * HEAD_DIM)
        o = jnp.dot(p, v_sc[r0:r0 + PAIR_KEYS, sl], preferred_element_type=F32)
        o_ref[r0:r0 + PAIR, sl] = (o / denom).astype(o_ref.dtype)

    units = [(h, pr * PAIR) for h in range(N_HEADS) for pr in range(QB // PAIR)]
    scored, weighted = None, None
    for unit in units + [None, None]:
        s = scores(*unit) if unit is not None else None
        nxt = (*scored[0], *weights(scored[1])) if scored is not None else None
        if weighted is not None:
            finish(*weighted)
        scored = (unit, s) if unit is not None else None
        weighted = nxt


def _band_attn_prompt(q, kt, v, bias, a):
    nb = SEQ // QB
    cur = lambda b, i: (b * nb + i, 0)
    left = lambda b, i: (b * nb + jnp.maximum(i - 1, 0), 0)
    cur_t = lambda b, i: (0, b * nb + i)
    left_t = lambda b, i: (0, b * nb + jnp.maximum(i - 1, 0))
    nbytes = (2 * 6 * QB * D_MODEL * 2 + 2 * N_HEADS * PAIR * PAIR_KEYS * 4 + 4 * QB * D_MODEL * 2
              + 8 * PAIR * PAIR_KEYS * 4)
    return pl.pallas_call(
        _band_attn_kernel,
        name="band_attn_prompt",
        grid=(BATCH, nb),
        in_specs=[pl.BlockSpec((QB, D_MODEL), cur),
                  pl.BlockSpec((D_MODEL, QB), left_t),
                  pl.BlockSpec((D_MODEL, QB), cur_t),
                  pl.BlockSpec((QB, D_MODEL), left),
                  pl.BlockSpec((QB, D_MODEL), cur),
                  pl.BlockSpec((None, N_HEADS, PAIR, PAIR_KEYS), lambda b, i: (a, 0, 0, 0))],
        out_specs=pl.BlockSpec((QB, D_MODEL), cur),
        out_shape=jax.ShapeDtypeStruct((M_ALL, D_MODEL), BF16),
        scratch_shapes=[pltpu.VMEM((D_MODEL, 2 * QB), BF16), pltpu.VMEM((2 * QB, D_MODEL), BF16)],
        compiler_params=_params(("arbitrary", "arbitrary"), nbytes),
    )(q, kt, kt, v, v, bias)


def _band_attn_sample_kernel(o_any, q_ref, kn_ref, vn_ref, ck_ref, cv_ref, bias_ref, o_ref):
    del o_any
    past = ck_ref.shape[0] // N_HEADS
    for h in range(N_HEADS):
        sl = slice(h * HEAD_DIM, (h + 1) * HEAD_DIM)
        rows = pl.ds(h, past, stride=N_HEADS)
        q = q_ref[:, sl]
        s_c = _dot_nt(q, ck_ref[rows, :].astype(BF16)) + bias_ref[h, 0:DEC_SEQ, 0:past]
        s_n = _dot_nt(q, kn_ref[:, sl].astype(BF16)) + bias_ref[h, 0:DEC_SEQ, past:past + DEC_SEQ]
        m = jnp.maximum(jnp.max(s_c, axis=-1, keepdims=True), jnp.max(s_n, axis=-1, keepdims=True))
        p_c = jnp.exp(s_c - m)
        p_n = jnp.exp(s_n - m)
        denom = jnp.sum(p_c, axis=-1, keepdims=True) + jnp.sum(p_n, axis=-1, keepdims=True)
        o = (jnp.dot(p_c.astype(BF16), cv_ref[rows, :].astype(BF16), preferred_element_type=F32)
             + jnp.dot(p_n.astype(BF16), vn_ref[:, sl], preferred_element_type=F32))
        o_ref[:, sl] = (o / denom).astype(o_ref.dtype)


def _band_attn_sample(o, q, k, v, cache_k, cache_v, bias, a):
    t = DEC_SEQ
    rb = M_PROMPT // t
    cr = BAND_LEFT * N_HEADS
    nbytes = (2 * (5 * t * D_MODEL * 2 + 2 * cr * HEAD_DIM * 4 + N_HEADS * PAIR * PAIR_KEYS * 4)
              + 4 * BAND_LEFT * HEAD_DIM * 4)
    return pl.pallas_call(
        _band_attn_sample_kernel,
        name="band_attn_sample",
        grid=(DEC_BATCH,),
        in_specs=[pl.BlockSpec(memory_space=pl.ANY),
                  pl.BlockSpec((t, D_MODEL), lambda b: (rb + b, 0)),
                  pl.BlockSpec((t, D_MODEL), lambda b: (rb + b, 0)),
                  pl.BlockSpec((t, D_MODEL), lambda b: (rb + b, 0)),
                  pl.BlockSpec((cr, HEAD_DIM), lambda b: (a * DEC_BATCH + b, 0)),
                  pl.BlockSpec((cr, HEAD_DIM), lambda b: (a * DEC_BATCH + b, 0)),
                  pl.BlockSpec((None, N_HEADS, PAIR, PAIR_KEYS), lambda b: (a, 0, 0, 0))],
        out_specs=pl.BlockSpec((t, D_MODEL), lambda b: (rb + b, 0)),
        out_shape=jax.ShapeDtypeStruct(o.shape, o.dtype),
        input_output_aliases={0: 0},
        compiler_params=_params(("arbitrary",), nbytes),
    )(o, q, k, v, cache_k, cache_v, bias)


def _mem_attn_body(q_ref, k_ref, v_ref, o_ref, keys_transposed):
    ts = min(q_ref.shape[0], MEM_SLAB)
    head = lambda h: slice(h * MEM_HEAD_DIM, (h + 1) * MEM_HEAD_DIM)
    vb = [v_ref[:, head(h)].astype(BF16) for h in range(MEM_HEADS)]
    if not keys_transposed:
        kb = [k_ref[:, head(h)].astype(BF16) for h in range(MEM_HEADS)]

    def scores(h, r0):
        q = q_ref[r0:r0 + ts, head(h)]
        if keys_transposed:
            return jnp.dot(q, k_ref[head(h), :], preferred_element_type=F32)
        return _dot_nt(q, kb[h])

    def finish(h, r0, s):
        m = jnp.max(s, axis=-1, keepdims=True)
        p = jnp.exp(s - m)
        denom = jnp.sum(p, axis=-1, keepdims=True)
        o = jnp.dot(p.astype(BF16), vb[h], preferred_element_type=F32)
        o_ref[r0:r0 + ts, head(h)] = (o / denom).astype(o_ref.dtype)

    units = [(h, r0) for h in range(MEM_HEADS) for r0 in range(0, q_ref.shape[0], ts)]
    pending = None
    for h, r0 in units:
        s = scores(h, r0)
        if pending is not None:
            finish(*pending)
        pending = (h, r0, s)
    finish(*pending)


def _mem_attn_prompt_kernel(q_ref, kt_ref, v_ref, o_ref):
    _mem_attn_body(q_ref, kt_ref, v_ref, o_ref, True)


def _mem_attn_sample_kernel(o_any, q_ref, k_ref, v_ref, o_ref):
    del o_any
    _mem_attn_body(q_ref, k_ref, v_ref, o_ref, False)


def _mem_attn_prompt(q, mem_kt, mem_kv, layer):
    tm = TM_MEM
    nt = SEQ // tm
    nbytes = 2 * (tm * D_MODEL * 2 + N_MEM * D_MODEL * 6 + tm * D_MODEL * 2) + 6 * tm * N_MEM * 4
    return pl.pallas_call(
        _mem_attn_prompt_kernel,
        name="mem_attn_prompt",
        grid=(BATCH, nt),
        in_specs=[pl.BlockSpec((tm, D_MODEL), lambda b, i: (b * nt + i, 0)),
                  pl.BlockSpec((None, D_MODEL, N_MEM), lambda b, i: (layer, 0, b)),
                  pl.BlockSpec((None, N_MEM, D_MODEL), lambda b, i: (layer, b, 1))],
        out_specs=pl.BlockSpec((tm, D_MODEL), lambda b, i: (b * nt + i, 0)),
        out_shape=jax.ShapeDtypeStruct((M_ALL, D_MODEL), BF16),
        compiler_params=_params(("arbitrary", "arbitrary"), nbytes),
    )(q, mem_kt, mem_kv)


def _mem_attn_sample(o, q, cache_k, cache_v, layer):
    t = DEC_SEQ
    rb = M_PROMPT // t
    nbytes = 2 * (2 * t * D_MODEL * 2 + 2 * N_MEM * D_MODEL * 4) + 6 * t * N_MEM * 4
    return pl.pallas_call(
        _mem_attn_sample_kernel,
        name="mem_attn_sample",
        grid=(DEC_BATCH,),
        in_specs=[pl.BlockSpec(memory_space=pl.ANY),
                  pl.BlockSpec((t, D_MODEL), lambda b: (rb + b, 0)),
                  pl.BlockSpec((N_MEM, D_MODEL), lambda b: (layer * DEC_BATCH + b, 0)),
                  pl.BlockSpec((N_MEM, D_MODEL), lambda b: (layer * DEC_BATCH + b, 0))],
        out_specs=pl.BlockSpec((t, D_MODEL), lambda b: (rb + b, 0)),
        out_shape=jax.ShapeDtypeStruct(o.shape, o.dtype),
        input_output_aliases={0: 0},
        compiler_params=_params(("arbitrary",), nbytes),
    )(o, q, cache_k, cache_v)


def _pool_body(x_ref, halo, pos0, n_hist, wp_ref, ps_ref, g_ref, b_ref, of_ref, ob_ref,
               ext_ref, sa_ref, sb_ref):
    tm = x_ref.shape[0]
    top = 2 * HALO
    rows = tm + top
    zeros = jnp.zeros((HALO, D_MODEL), F32)
    ext_ref[0:HALO, :] = zeros
    ext_ref[HALO:top, :] = halo
    ext_ref[top:rows, :] = x_ref[...]
    sa_ref[0:HALO, :] = zeros[:, :POOL_GROUP]
    sb_ref[0:HALO, :] = zeros[:, :POOL_GROUP]
    seen = (pos0 + n_hist + 1 + lax.broadcasted_iota(jnp.int32, (tm, 1), 0)).astype(F32)
    bufs = (sa_ref, sb_ref)
    for gi, win in enumerate(POOL_WINDOWS):
        cs = slice(gi * POOL_GROUP, (gi + 1) * POOL_GROUP)
        src, d, lvl = None, 1, 0
        while 2 * d < win:
            if src is None:
                val = ext_ref[HALO:rows, cs] + ext_ref[HALO - d:rows - d, cs]
            else:
                val = src[HALO:rows, :] + src[HALO - d:rows - d, :]
            dst = bufs[lvl % 2]
            dst[HALO:rows, :] = val
            src, d, lvl = dst, 2 * d, lvl + 1
        if src is None:
            wsum = ext_ref[top:rows, cs] + ext_ref[top - d:rows - d, cs]
        else:
            wsum = src[top:rows, :] + src[top - d:rows - d, :]
        xg = x_ref[:, cs]
        u = wsum / jnp.minimum(seen, float(win)) - xg
        y = jnp.dot(u.astype(BF16), wp_ref[gi], preferred_element_type=F32) * ps_ref[:, cs]
        of_ref[:, cs] = ALPHA * xg + y
    out = _layer_norm(of_ref[...], g_ref[...], b_ref[...])
    of_ref[...] = out
    ob_ref[...] = out.astype(BF16)


def _pool_prompt_kernel(x_ref, halo_ref, wp_ref, ps_ref, g_ref, b_ref, of_ref, ob_ref,
                        ext_ref, sa_ref, sb_ref):
    it = pl.program_id(0) % (SEQ // TM_POOL)
    halo = jnp.where(it > 0, halo_ref[...], 0.0)
    _pool_body(x_ref, halo, it * TM_POOL, 0, wp_ref, ps_ref, g_ref, b_ref, of_ref, ob_ref,
               ext_ref, sa_ref, sb_ref)


def _pool_sample_kernel(of_any, ob_any, x_ref, halo_ref, wp_ref, ps_ref, g_ref, b_ref, of_ref, ob_ref,
                        ext_ref, sa_ref, sb_ref):
    del of_any, ob_any
    _pool_body(x_ref, halo_ref[...], 0, POOL_HIST, wp_ref, ps_ref, g_ref, b_ref, of_ref, ob_ref,
               ext_ref, sa_ref, sb_ref)


def _pool_scratch(tm):
    return [pltpu.VMEM((tm + 2 * HALO, D_MODEL), F32),
            pltpu.VMEM((tm + 2 * HALO, POOL_GROUP), F32),
            pltpu.VMEM((tm + 2 * HALO, POOL_GROUP), F32)]


def _pool_const_specs(p, gi):
    return [pl.BlockSpec((None, len(POOL_WINDOWS), POOL_GROUP, POOL_GROUP), lambda i: (p, 0, 0, 0)),
            pl.BlockSpec((None, 1, D_MODEL), lambda i: (p, 0, 0))] + _ln_param_specs(gi)


def _pool_bytes(tm):
    return (2 * tm * D_MODEL * 4 + 2 * HALO * D_MODEL * 4 + 2 * D_MODEL * POOL_GROUP * 2 + 2 * tm * D_MODEL * 6
            + (tm + 2 * HALO) * (D_MODEL + 2 * POOL_GROUP) * 4 + 2 * tm * D_MODEL * 4)


def _pool_prompt(x, wp, ps, p, gb, gi):
    tm = TM_POOL
    hb = tm // HALO
    return pl.pallas_call(
        _pool_prompt_kernel,
        name="pool_prompt",
        grid=(M_PROMPT // tm,),
        in_specs=[pl.BlockSpec((tm, D_MODEL), lambda i: (i, 0)),
                  pl.BlockSpec((HALO, D_MODEL), lambda i: (jnp.maximum(i * hb - 1, 0), 0))]
                 + _pool_const_specs(p, gi),
        out_specs=[pl.BlockSpec((tm, D_MODEL), lambda i: (i, 0)),
                   pl.BlockSpec((tm, D_MODEL), lambda i: (i, 0))],
        out_shape=[jax.ShapeDtypeStruct((M_ALL, D_MODEL), F32),
                   jax.ShapeDtypeStruct((M_ALL, D_MODEL), BF16)],
        scratch_shapes=_pool_scratch(tm),
        compiler_params=_params(("arbitrary",), _pool_bytes(tm)),
    )(x, x, wp, ps, *gb)


def _pool_sample(of, ob, x, state, wp, ps, p, gb, gi):
    tm = DEC_SEQ
    rb = M_PROMPT // tm
    return pl.pallas_call(
        _pool_sample_kernel,
        name="pool_sample",
        grid=(DEC_BATCH,),
        in_specs=[pl.BlockSpec(memory_space=pl.ANY),
                  pl.BlockSpec(memory_space=pl.ANY),
                  pl.BlockSpec((tm, D_MODEL), lambda i: (rb + i, 0)),
                  pl.BlockSpec((None, HALO, D_MODEL), lambda i: (p * DEC_BATCH + i, 0, 0))]
                 + _pool_const_specs(p, gi),
        out_specs=[pl.BlockSpec((tm, D_MODEL), lambda i: (rb + i, 0)),
                   pl.BlockSpec((tm, D_MODEL), lambda i: (rb + i, 0))],
        out_shape=[jax.ShapeDtypeStruct(of.shape, of.dtype),
                   jax.ShapeDtypeStruct(ob.shape, ob.dtype)],
        input_output_aliases={0: 0, 1: 1},
        scratch_shapes=_pool_scratch(tm),
        compiler_params=_params(("arbitrary",), _pool_bytes(tm)),
    )(of, ob, x, state, wp, ps, *gb)


def _pair_bias(table):
    lead = table.shape[:-1]
    period = PAIR + PAIR_KEYS
    far = table[..., 2 * REL_CLIP:]
    vec = jnp.concatenate([jnp.broadcast_to(far, lead + (BAND_LEFT - REL_CLIP,)),
                           jnp.flip(table[..., 1:], axis=-1),
                           jnp.broadcast_to(far, lead + (period - BAND_LEFT - REL_CLIP,))], axis=-1)
    flat = jnp.tile(vec, (1,) * len(lead) + (PAIR,))[..., :PAIR * (period - 1)]
    toe = flat.reshape(lead + (PAIR, period - 1))[..., :PAIR_KEYS]
    r = np.arange(PAIR)[:, None]
    c = np.arange(PAIR_KEYS)[None, :]
    off = c - (r // CHUNK) * CHUNK
    in_band = np.logical_and(off >= 0, off < BAND_KEYS)
    return jnp.where(in_band, toe, NEG_BIG).astype(F32)


def _tail_rows(x, n_tail):
    return jnp.stack([x[(bi + 1) * SEQ - n_tail:(bi + 1) * SEQ] for bi in range(BATCH)])


def kernel(x_prompt, x_sample, cache_attn_k, cache_attn_v, state_pool, cache_mem_k, cache_mem_v,
           mem_prompt, w_qkv, w_attn_o, rel_bias, w_pool, pool_scale, w_mem_q, w_mem_kv, w_mem_o,
           w_ffn_in, w_ffn_out, ln_g, ln_b):
    D = D_MODEL
    assert cache_attn_k.shape[2] == BAND_LEFT and DEPTH % 2 == 0
    xp = x_prompt.reshape(M_PROMPT, D)
    xs = x_sample.reshape(M_SAMPLE, D)
    x_f = None
    x_b = _ingest(xp, xs)
    mem_b = mem_prompt.reshape(BATCH * N_MEM, D).astype(BF16)
    w_pool_b = w_pool.astype(BF16)
    gb = (ln_g.reshape(DEPTH * 3, 1, D), ln_b.reshape(DEPTH * 3, 1, D))
    pscale = pool_scale.reshape(-1, 1, D)
    cache_ak = cache_attn_k.reshape(-1, HEAD_DIM)
    cache_av = cache_attn_v.reshape(-1, HEAD_DIM)
    cache_mk = cache_mem_k.reshape(-1, D)
    cache_mv = cache_mem_v.reshape(-1, D)
    state_pad = jnp.pad(state_pool, ((0, 0), (0, 0), (HALO - POOL_HIST, 0), (0, 0))).reshape(-1, HALO, D)
    bias = _pair_bias(rel_bias)
    mem_kv, mem_kt = _mem_kv_all(mem_b, w_mem_kv)
    k_cols, v_cols = slice(0, D), slice(D, 2 * D)

    ak_p, av_p, pool_p, ak_s, av_s, pool_s = [], [], [], [], [], []
    for i in range(DEPTH):
        if i % 2 == 0:
            a = i // 2
            q, wo_b = _matmul(x_b, w_qkv, a, 0, D, TM_MM, (BF16,), scale=HEAD_DIM ** -0.5, cast=(w_attn_o, a))
            k, kt = _matmul(x_b, w_qkv, a, D, D, TM_MM, (F32,), transposed=True)
            v, v_b = _matmul(x_b, w_qkv, a, 2 * D, D, TM_MM, (F32, BF16))
            o = _band_attn_prompt(q, kt, v_b, bias, a)
            o = _band_attn_sample(o, q, k, v_b, cache_ak, cache_av, bias, a)
            ak_p.append(_tail_rows(k, BAND_LEFT).reshape(BATCH, BAND_LEFT, N_HEADS, HEAD_DIM))
            av_p.append(_tail_rows(v, BAND_LEFT).reshape(BATCH, BAND_LEFT, N_HEADS, HEAD_DIM))
            ak_s.append(k[M_PROMPT:].reshape(DEC_BATCH, DEC_SEQ, N_HEADS, HEAD_DIM))
            av_s.append(v[M_PROMPT:].reshape(DEC_BATCH, DEC_SEQ, N_HEADS, HEAD_DIM))
            if i == 0:
                outs = _mm_ln(o, wo_b, None, xp, gb, 0, tm=TM_LN_PROMPT, n_tiles=M_PROMPT // TM_LN_PROMPT,
                              a_blk0=0, x_blk0=0, o_blk0=0, out_rows=M_ALL, out_dtypes=(F32, BF16))
                x_f, x_b = _mm_ln(o, wo_b, None, xs, gb, 0, tm=M_SAMPLE, n_tiles=1,
                                  a_blk0=M_PROMPT // M_SAMPLE, x_blk0=0, o_blk0=M_PROMPT // M_SAMPLE,
                                  out_rows=M_ALL, out_dtypes=(F32, BF16), into=outs)
            else:
                x_f, x_b = _mm_ln_all(o, wo_b, None, x_f, gb, 3 * i)
        else:
            p = i // 2
            pool_p.append(_tail_rows(x_f, POOL_HIST))
            ext = jnp.concatenate([state_pool[p], x_f[M_PROMPT:].reshape(DEC_BATCH, DEC_SEQ, D)], axis=1)
            pool_s.append(ext[:, DEC_SEQ:])
            y_f, y_b = _pool_prompt(x_f, w_pool_b, pscale, p, gb, 3 * i)
            x_f, x_b = _pool_sample(y_f, y_b, x_f, state_pad, w_pool_b, pscale, p, gb, 3 * i)
        q, wmo_b = _matmul(x_b, w_mem_q, i, 0, D, TM_MM, (BF16,), scale=MEM_HEAD_DIM ** -0.5, cast=(w_mem_o, i))
        o = _mem_attn_prompt(q, mem_kt, mem_kv, i)
        o = _mem_attn_sample(o, q, cache_mk, cache_mv, i)
        x_f, x_b = _mm_ln_all(o, wmo_b, None, x_f, gb, 3 * i + 1)
        h, wfo_b = _swiglu_in(x_b, w_ffn_in, w_ffn_out, i)
        if i < DEPTH - 1:
            x_f, x_b = _mm_ln_all(h, wfo_b, None, x_f, gb, 3 * i + 2)
        else:
            y_p, = _mm_ln(h, wfo_b, None, x_f, gb, 3 * i + 2, tm=TM_LN_WIDE,
                          n_tiles=M_PROMPT // TM_LN_WIDE, a_blk0=0, x_blk0=0, o_blk0=0,
                          out_rows=M_PROMPT, out_dtypes=(F32,))
            y_s, = _mm_ln(h, wfo_b, None, x_f, gb, 3 * i + 2, tm=M_SAMPLE, n_tiles=1,
                          a_blk0=M_PROMPT // M_SAMPLE, x_blk0=M_PROMPT // M_SAMPLE, o_blk0=0,
                          out_rows=M_SAMPLE, out_dtypes=(F32,))

    mem_shape = (DEPTH, BATCH, N_MEM, MEM_HEADS, MEM_HEAD_DIM)
    return (y_p.reshape(BATCH, SEQ, D), y_s.reshape(DEC_BATCH, DEC_SEQ, D),
            jnp.stack(ak_p), jnp.stack(av_p), jnp.stack(pool_p),
            mem_kv[:, :, k_cols].reshape(mem_shape), mem_kv[:, :, v_cols].reshape(mem_shape),
            jnp.stack(ak_s), jnp.stack(av_s), jnp.stack(pool_s))
```

```python
import functools

import jax
import jax.numpy as jnp
from jax import lax
from jax.experimental import pallas as pl
from jax.experimental.pallas import tpu as pltpu

F32 = jnp.float32
BF16 = jnp.bfloat16

D_MODEL = 2048
BATCH = 4
SEQ = 4096
DEPTH = 4
DEC_BATCH = 8
DEC_SEQ = 32
CHUNK = 64
BAND_LEFT = 512
BAND_KEYS = BAND_LEFT + CHUNK
N_HEADS = 16
HEAD_DIM = D_MODEL // N_HEADS
REL_CLIP = 2 * CHUNK
POOL_WINDOWS = (2, 4, 8, 16)
POOL_GROUP = D_MODEL // len(POOL_WINDOWS)
POOL_HIST = max(POOL_WINDOWS) - 1
N_MEM = 256
MEM_HEADS = 4
MEM_HEAD_DIM = D_MODEL // MEM_HEADS
D_FF = 5632
ALPHA = (2.0 * DEPTH) ** 0.25
LN_EPS = 1e-5
NEG_BIG = -1e30

M_PROMPT = BATCH * SEQ
M_SAMPLE = DEC_BATCH * DEC_SEQ
M_ALL = M_PROMPT + M_SAMPLE

TM_INGEST = 1024
TM_MM = 1280
TN_MM = 1024
TN_FF = 512
FF_SUB = 2
TM_LN = 640
TM_LN_PROMPT = 512
TM_LN_WIDE = 256
LN_SLAB = 128
LN_PIECES = 2
QB = 512
PAIR = 4 * CHUNK
PAIR_KEYS = BAND_LEFT + PAIR
TM_MEM = 1024
MEM_SLAB = 256
TM_POOL = 512
HALO = 16

V7X_VMEM_BYTES = 64 * 1024 * 1024


def _vmem_limit(nbytes):
    return int(min(nbytes + (12 << 20), V7X_VMEM_BYTES - (6 << 20)))


def _params(sem, nbytes):
    return pltpu.CompilerParams(dimension_semantics=sem, vmem_limit_bytes=_vmem_limit(nbytes))


def _ingest_kernel(*refs):
    x_ref, o_ref = refs[-2:]
    o_ref[...] = x_ref[...].astype(BF16)


def _ingest(xp, xs):
    def call(x, tm, o_blk0, into):
        n_alias = len(into)
        return pl.pallas_call(
            _ingest_kernel,
            name="ingest",
            grid=(x.shape[0] // tm,),
            in_specs=[pl.BlockSpec(memory_space=pl.ANY)] * n_alias
                     + [pl.BlockSpec((tm, D_MODEL), lambda i: (i, 0))],
            out_specs=pl.BlockSpec((tm, D_MODEL), lambda i: (o_blk0 + i, 0)),
            out_shape=jax.ShapeDtypeStruct((M_ALL, D_MODEL), BF16),
            input_output_aliases={j: j for j in range(n_alias)},
            compiler_params=_params(("arbitrary",), 2 * tm * D_MODEL * (4 + 2)),
        )(*into, x)

    out = call(xp, TM_INGEST, 0, ())
    return call(xs, M_SAMPLE, M_PROMPT // M_SAMPLE, (out,))


def _cast_rider(cast, n_inner, n_steps):
    if cast is None:
        return [], [], [], 0
    w, layer = cast
    _, k, d = w.shape
    rows = next(r for r in (16, 32, 64, 128, 256, 512) if k % r == 0 and k // r <= n_steps)
    last = k // rows - 1
    chunk = lambda j, i: jnp.minimum(j * n_inner + i, last)
    return ([pl.BlockSpec((None, rows, d), lambda j, i: (layer, chunk(j, i), 0))],
            [pl.BlockSpec((rows, d), lambda j, i: (chunk(j, i), 0))],
            [jax.ShapeDtypeStruct((k, d), BF16)], 2 * rows * d * 6)


def _mm_kernel(*refs, scale, n_out, rider, transposed):
    n_in = 3 if rider else 2
    a_ref, w_ref = refs[:2]
    o_refs = refs[n_in:n_in + n_out]
    wb_ref = refs[-1]
    if rider:
        refs[n_in + n_out + transposed][...] = refs[2][...].astype(BF16)

    @pl.when(pl.program_id(1) == 0)
    def _():
        wb_ref[...] = w_ref[...].astype(BF16)

    acc = jnp.dot(a_ref[...], wb_ref[...], preferred_element_type=F32)
    if scale != 1.0:
        acc = acc * scale
    for o_ref in o_refs:
        o_ref[...] = acc.astype(o_ref.dtype)
    if transposed:
        refs[n_in + n_out][...] = acc.T.astype(BF16)


def _matmul(a, w, layer, col0, n_out, tm, out_dtypes, scale=1.0, cast=None, transposed=False):
    m, k = a.shape
    tn = TN_MM
    assert m % tm == 0 and n_out % tn == 0 and col0 % tn == 0
    cb = col0 // tn
    grid = (n_out // tn, m // tm)
    r_in, r_out, r_shape, r_bytes = _cast_rider(cast, grid[1], grid[0] * grid[1])
    osz = sum(jnp.dtype(d).itemsize for d in out_dtypes)
    t_out = [pl.BlockSpec((tn, tm), lambda j, i: (j, i))] if transposed else []
    t_shape = [jax.ShapeDtypeStruct((n_out, m), BF16)] if transposed else []
    nbytes = (2 * tm * k * 2 + 2 * k * tn * 4 + k * tn * 2 + 2 * tm * tn * osz + tm * tn * 4 + r_bytes
              + transposed * tm * tn * (2 * 2 + 4))
    return pl.pallas_call(
        functools.partial(_mm_kernel, scale=scale, n_out=len(out_dtypes), rider=cast is not None,
                          transposed=transposed),
        name="mm",
        grid=grid,
        in_specs=[pl.BlockSpec((tm, k), lambda j, i: (i, 0)),
                  pl.BlockSpec((None, k, tn), lambda j, i: (layer, 0, j + cb))] + r_in,
        out_specs=[pl.BlockSpec((tm, tn), lambda j, i: (i, j)) for _ in out_dtypes] + t_out + r_out,
        out_shape=[jax.ShapeDtypeStruct((m, n_out), d) for d in out_dtypes] + t_shape + r_shape,
        scratch_shapes=[pltpu.VMEM((k, tn), BF16)],
        compiler_params=_params(("arbitrary", "arbitrary"), nbytes),
    )(a, w, *(cast[:1] if cast else ()))


def _mem_kv_kernel(a_ref, w_ref, o_ref, kt_ref, *, n_key_steps):
    acc = jnp.dot(a_ref[...], w_ref[...].astype(BF16), preferred_element_type=F32)
    o_ref[...] = acc

    @pl.when(pl.program_id(1) < n_key_steps)
    def _():
        kt_ref[...] = acc.T.astype(BF16)


def _mem_kv_all(mem, w):
    m, k = mem.shape
    n = w.shape[2]
    tn = TN_MM
    nk = n // 2 // tn
    nbytes = 2 * m * k * 2 + 2 * k * tn * 4 + k * tn * 2 + 2 * m * tn * 4 + 2 * m * tn * 4 + 2 * m * tn * 2
    return pl.pallas_call(
        functools.partial(_mem_kv_kernel, n_key_steps=nk),
        name="mem_kv",
        grid=(w.shape[0], n // tn),
        in_specs=[pl.BlockSpec((m, k), lambda l, j: (0, 0)),
                  pl.BlockSpec((None, k, tn), lambda l, j: (l, 0, j))],
        out_specs=[pl.BlockSpec((None, m, tn), lambda l, j: (l, 0, j)),
                   pl.BlockSpec((None, tn, m), lambda l, j: (l, jnp.minimum(j, nk - 1), 0))],
        out_shape=[jax.ShapeDtypeStruct((w.shape[0], m, n), F32),
                   jax.ShapeDtypeStruct((w.shape[0], n // 2, m), BF16)],
        compiler_params=_params(("arbitrary", "arbitrary"), nbytes),
    )(mem, w)


def _swiglu_kernel(a_ref, wg_ref, wu_ref, wc_ref, o_ref, wcb_ref, wgb_ref, wub_ref):
    wcb_ref[...] = wc_ref[...].astype(BF16)

    @pl.when(pl.program_id(1) == 0)
    def _():
        wgb_ref[...] = wg_ref[...].astype(BF16)
        wub_ref[...] = wu_ref[...].astype(BF16)

    ts = a_ref.shape[0] // FF_SUB

    def project(r):
        a = a_ref[r * ts:(r + 1) * ts, :]
        return (jnp.dot(a, wgb_ref[...], preferred_element_type=F32),
                jnp.dot(a, wub_ref[...], preferred_element_type=F32))

    def finish(r, gate, up):
        o_ref[r * ts:(r + 1) * ts, :] = (gate / (1.0 + jnp.exp(-gate)) * up).astype(o_ref.dtype)

    pending = None
    for r in range(FF_SUB):
        gate_up = project(r)
        if pending is not None:
            finish(*pending)
        pending = (r, *gate_up)
    finish(*pending)


def _swiglu_in(a, w_in, w_out, layer):
    m, k = a.shape
    tm, tn = TM_MM, TN_FF
    nj = D_FF // tn
    grid = (nj, m // tm)
    r_in, r_out, r_shape, r_bytes = _cast_rider((w_out, layer), grid[1], grid[0] * grid[1])
    nbytes = (2 * tm * k * 2 + 4 * k * tn * 4 + 2 * k * tn * 2 + 2 * tm * tn * 2 + 3 * tm * tn * 4
              + r_bytes)
    return pl.pallas_call(
        _swiglu_kernel,
        name="swiglu_in",
        grid=grid,
        in_specs=[pl.BlockSpec((tm, k), lambda j, i: (i, 0)),
                  pl.BlockSpec((None, k, tn), lambda j, i: (layer, 0, j)),
                  pl.BlockSpec((None, k, tn), lambda j, i: (layer, 0, j + nj))] + r_in,
        out_specs=[pl.BlockSpec((tm, tn), lambda j, i: (i, j))] + r_out,
        out_shape=[jax.ShapeDtypeStruct((m, D_FF), BF16)] + r_shape,
        scratch_shapes=[pltpu.VMEM((k, tn), BF16), pltpu.VMEM((k, tn), BF16)],
        compiler_params=_params(("arbitrary", "arbitrary"), nbytes),
    )(a, w_in, w_in, w_out)


def _layer_norm(z, g, b):
    mu = jnp.mean(z, axis=-1, keepdims=True)
    zc = z - mu
    var = jnp.mean(zc * zc, axis=-1, keepdims=True)
    return zc * lax.rsqrt(var + LN_EPS) * g + b


def _mm_ln_kernel(*refs, n_alias, n_sub):
    a_ref, w_ref, x_ref, g_ref, b_ref = refs[n_alias:n_alias + 5]
    o_refs = refs[n_alias + 5:-1]
    z_ref = refs[-1]
    ts = z_ref.shape[1]
    tc = D_MODEL // LN_PIECES
    tr = ts // LN_PIECES

    def pre_norm(r, c):
        rs, cs = slice(r * ts, (r + 1) * ts), slice(c * tc, (c + 1) * tc)
        z_ref[r, :, cs] = ALPHA * x_ref[rs, cs] + jnp.dot(a_ref[rs, :], w_ref[:, cs],
                                                         preferred_element_type=F32)

    def finish(r, c):
        y = _layer_norm(z_ref[r, c * tr:(c + 1) * tr, :], g_ref[...], b_ref[...])
        for o_ref in o_refs:
            o_ref[r * ts + c * tr:r * ts + (c + 1) * tr, :] = y.astype(o_ref.dtype)

    for r in range(n_sub + 1):
        for c in range(LN_PIECES):
            if r < n_sub:
                pre_norm(r, c)
            if r > 0:
                finish(r - 1, c)


def _ln_param_specs(gi):
    return [pl.BlockSpec((None, 1, D_MODEL), lambda *_: (gi, 0, 0)),
            pl.BlockSpec((None, 1, D_MODEL), lambda *_: (gi, 0, 0))]


def _mm_ln(a, w, layer, x, gb, gi, *, tm, n_tiles, a_blk0, x_blk0, o_blk0, out_rows, out_dtypes, into=None):
    kdim = a.shape[1]
    assert tm % LN_SLAB == 0
    n_sub = tm // LN_SLAB
    n_alias = 0 if into is None else len(into)
    if layer is None:
        w_spec = pl.BlockSpec((kdim, D_MODEL), lambda i: (0, 0), pipeline_mode=pl.Buffered(1))
    else:
        w_spec = pl.BlockSpec((None, kdim, D_MODEL), lambda i: (layer, 0, 0), pipeline_mode=pl.Buffered(1))
    osz = sum(jnp.dtype(d).itemsize for d in out_dtypes)
    nbytes = (2 * tm * kdim * 2 + kdim * D_MODEL * 2 + 2 * tm * D_MODEL * 4 + 2 * tm * D_MODEL * osz
              + tm * D_MODEL * 4 + 2 * LN_SLAB * D_MODEL * 4)
    return pl.pallas_call(
        functools.partial(_mm_ln_kernel, n_alias=n_alias, n_sub=n_sub),
        name="mm_ln",
        grid=(n_tiles,),
        in_specs=[pl.BlockSpec(memory_space=pl.ANY)] * n_alias + [
            pl.BlockSpec((tm, kdim), lambda i: (a_blk0 + i, 0)),
            w_spec,
            pl.BlockSpec((tm, D_MODEL), lambda i: (x_blk0 + i, 0))] + _ln_param_specs(gi),
        out_specs=[pl.BlockSpec((tm, D_MODEL), lambda i: (o_blk0 + i, 0)) for _ in out_dtypes],
        out_shape=[jax.ShapeDtypeStruct((out_rows, D_MODEL), d) for d in out_dtypes],
        input_output_aliases={j: j for j in range(n_alias)},
        scratch_shapes=[pltpu.VMEM((n_sub, LN_SLAB, D_MODEL), F32)],
        compiler_params=_params(("arbitrary",), nbytes),
    )(*(into or ()), a, w, x, *gb)


def _mm_ln_all(a, w, layer, x, gb, gi):
    tm = TM_LN if a.shape[1] <= D_MODEL else TM_LN_WIDE
    return _mm_ln(a, w, layer, x, gb, gi, tm=tm, n_tiles=M_ALL // tm, a_blk0=0, x_blk0=0, o_blk0=0,
                  out_rows=M_ALL, out_dtypes=(F32, BF16))


def _dot_nt(a, b):
    return lax.dot_general(a, b, (((1,), (1,)), ((), ())), preferred_element_type=F32)


def _band_attn_kernel(q_ref, ktl_ref, ktc_ref, vl_ref, vc_ref, bias_ref, o_ref, kt_sc, v_sc):
    kt_sc[:, 0:QB] = ktl_ref[...]
    kt_sc[:, QB:2 * QB] = ktc_ref[...]
    v_sc[0:QB, :] = vl_ref[...]
    v_sc[QB:2 * QB, :] = vc_ref[...]
    no_left = pl.program_id(1) == 0
    col = lax.broadcasted_iota(jnp.int32, (PAIR, PAIR_KEYS), 1)

    def scores(h, r0):
        sl = slice(h * HEAD_DIM, (h + 1) * HEAD_DIM)
        s = jnp.dot(q_ref[r0:r0 + PAIR, sl], kt_sc[sl, r0:r0 + PAIR_KEYS],
                    preferred_element_type=F32) + bias_ref[h]
        return jnp.where(jnp.logical_and(no_left, col < QB - r0), NEG_BIG, s)

    def weights(s):
        m = jnp.max(s, axis=-1, keepdims=True)
        p = jnp.exp(s - m)
        return p.astype(BF16), jnp.sum(p, axis=-1, keepdims=True)

    def finish(h, r0, p, denom):
        sl = slice(h * HEAD_DIM, (h + 1) * HEAD_DIM)
        o = jnp.dot(p, v_sc[r0:r0 + PAIR_KEYS, sl], preferred_element_type=F32)
        o_ref[r0:r0 + PAIR, sl] = (o / denom).astype(o_ref.dtype)

    units = [(h, pr * PAIR) for h in range(N_HEADS) for pr in range(QB // PAIR)]
    scored, weighted = None, None
    for unit in units + [None, None]:
        s = scores(*unit) if unit is not None else None
        nxt = (*scored[0], *weights(scored[1])) if scored is not None else None
        if weighted is not None:
            finish(*weighted)
        scored = (unit, s) if unit is not None else None
        weighted = nxt


def _band_attn_prompt(q, kt, v, bias, a):
    nb = SEQ // QB
    cur = lambda b, i: (b * nb + i, 0)
    left = lambda b, i: (b * nb + jnp.maximum(i - 1, 0), 0)
    cur_t = lambda b, i: (0, b * nb + i)
    left_t = lambda b, i: (0, b * nb + jnp.maximum(i - 1, 0))
    nbytes = (2 * 6 * QB * D_MODEL * 2 + N_HEADS * PAIR * PAIR_KEYS * 4 + 4 * QB * D_MODEL * 2
              + 8 * PAIR * PAIR_KEYS * 4)
    return pl.pallas_call(
        _band_attn_kernel,
        name="band_attn_prompt",
        grid=(BATCH, nb),
        in_specs=[pl.BlockSpec((QB, D_MODEL), cur),
                  pl.BlockSpec((D_MODEL, QB), left_t),
                  pl.BlockSpec((D_MODEL, QB), cur_t),
                  pl.BlockSpec((QB, D_MODEL), left),
                  pl.BlockSpec((QB, D_MODEL), cur),
                  pl.BlockSpec((None, N_HEADS, PAIR, PAIR_KEYS), lambda b, i: (a, 0, 0, 0),
                               pipeline_mode=pl.Buffered(1))],
        out_specs=pl.BlockSpec((QB, D_MODEL), cur),
        out_shape=jax.ShapeDtypeStruct((M_ALL, D_MODEL), BF16),
        scratch_shapes=[pltpu.VMEM((D_MODEL, 2 * QB), BF16), pltpu.VMEM((2 * QB, D_MODEL), BF16)],
        compiler_params=_params(("arbitrary", "arbitrary"), nbytes),
    )(q, kt, kt, v, v, bias)


def _band_attn_sample_kernel(o_any, q_ref, kn_ref, vn_ref, ck_ref, cv_ref, bias_ref, o_ref):
    del o_any
    past = ck_ref.shape[0] // N_HEADS
    for h in range(N_HEADS):
        sl = slice(h * HEAD_DIM, (h + 1) * HEAD_DIM)
        rows = pl.ds(h, past, stride=N_HEADS)
        q = q_ref[:, sl]
        s_c = _dot_nt(q, ck_ref[rows, :].astype(BF16)) + bias_ref[h, 0:DEC_SEQ, 0:past]
        s_n = _dot_nt(q, kn_ref[:, sl].astype(BF16)) + bias_ref[h, 0:DEC_SEQ, past:past + DEC_SEQ]
        m = jnp.maximum(jnp.max(s_c, axis=-1, keepdims=True), jnp.max(s_n, axis=-1, keepdims=True))
        p_c = jnp.exp(s_c - m)
        p_n = jnp.exp(s_n - m)
        denom = jnp.sum(p_c, axis=-1, keepdims=True) + jnp.sum(p_n, axis=-1, keepdims=True)
        o = (jnp.dot(p_c.astype(BF16), cv_ref[rows, :].astype(BF16), preferred_element_type=F32)
             + jnp.dot(p_n.astype(BF16), vn_ref[:, sl], preferred_element_type=F32))
        o_ref[:, sl] = (o / denom).astype(o_ref.dtype)


def _band_attn_sample(o, q, k, v, cache_k, cache_v, bias, a):
    t = DEC_SEQ
    rb = M_PROMPT // t
    cr = BAND_LEFT * N_HEADS
    nbytes = (2 * (5 * t * D_MODEL * 2 + 2 * cr * HEAD_DIM * 4) + N_HEADS * PAIR * PAIR_KEYS * 4
              + 4 * BAND_LEFT * HEAD_DIM * 4)
    return pl.pallas_call(
        _band_attn_sample_kernel,
        name="band_attn_sample",
        grid=(DEC_BATCH,),
        in_specs=[pl.BlockSpec(memory_space=pl.ANY),
                  pl.BlockSpec((t, D_MODEL), lambda b: (rb + b, 0)),
                  pl.BlockSpec((t, D_MODEL), lambda b: (rb + b, 0)),
                  pl.BlockSpec((t, D_MODEL), lambda b: (rb + b, 0)),
                  pl.BlockSpec((cr, HEAD_DIM), lambda b: (a * DEC_BATCH + b, 0)),
                  pl.BlockSpec((cr, HEAD_DIM), lambda b: (a * DEC_BATCH + b, 0)),
                  pl.BlockSpec((None, N_HEADS, PAIR, PAIR_KEYS), lambda b: (a, 0, 0, 0),
                               pipeline_mode=pl.Buffered(1))],
        out_specs=pl.BlockSpec((t, D_MODEL), lambda b: (rb + b, 0)),
        out_shape=jax.ShapeDtypeStruct(o.shape, o.dtype),
        input_output_aliases={0: 0},
        compiler_params=_params(("arbitrary",), nbytes),
    )(o, q, k, v, cache_k, cache_v, bias)


def _mem_attn_body(q_ref, k_ref, v_ref, o_ref, keys_transposed):
    ts = min(q_ref.shape[0], MEM_SLAB)
    head = lambda h: slice(h * MEM_HEAD_DIM, (h + 1) * MEM_HEAD_DIM)
    vb = [v_ref[:, head(h)].astype(BF16) for h in range(MEM_HEADS)]
    if not keys_transposed:
        kb = [k_ref[:, head(h)].astype(BF16) for h in range(MEM_HEADS)]

    def scores(h, r0):
        q = q_ref[r0:r0 + ts, head(h)]
        if keys_transposed:
            return jnp.dot(q, k_ref[head(h), :], preferred_element_type=F32)
        return _dot_nt(q, kb[h])

    def finish(h, r0, s):
        m = jnp.max(s, axis=-1, keepdims=True)
        p = jnp.exp(s - m)
        denom = jnp.sum(p, axis=-1, keepdims=True)
        o = jnp.dot(p.astype(BF16), vb[h], preferred_element_type=F32)
        o_ref[r0:r0 + ts, head(h)] = (o / denom).astype(o_ref.dtype)

    units = [(h, r0) for h in range(MEM_HEADS) for r0 in range(0, q_ref.shape[0], ts)]
    pending = None
    for h, r0 in units:
        s = scores(h, r0)
        if pending is not None:
            finish(*pending)
        pending = (h, r0, s)
    finish(*pending)


def _mem_attn_prompt_kernel(q_ref, kt_ref, v_ref, o_ref):
    _mem_attn_body(q_ref, kt_ref, v_ref, o_ref, True)


def _mem_attn_sample_kernel(o_any, q_ref, k_ref, v_ref, o_ref):
    del o_any
    _mem_attn_body(q_ref, k_ref, v_ref, o_ref, False)


def _mem_attn_prompt(q, mem_kt, mem_kv, layer):
    tm = TM_MEM
    nt = SEQ // tm
    nbytes = 2 * (tm * D_MODEL * 2 + N_MEM * D_MODEL * 6 + tm * D_MODEL * 2) + 6 * tm * N_MEM * 4
    return pl.pallas_call(
        _mem_attn_prompt_kernel,
        name="mem_attn_prompt",
        grid=(BATCH, nt),
        in_specs=[pl.BlockSpec((tm, D_MODEL), lambda b, i: (b * nt + i, 0)),
                  pl.BlockSpec((None, D_MODEL, N_MEM), lambda b, i: (layer, 0, b)),
                  pl.BlockSpec((None, N_MEM, D_MODEL), lambda b, i: (layer, b, 1))],
        out_specs=pl.BlockSpec((tm, D_MODEL), lambda b, i: (b * nt + i, 0)),
        out_shape=jax.ShapeDtypeStruct((M_ALL, D_MODEL), BF16),
        compiler_params=_params(("arbitrary", "arbitrary"), nbytes),
    )(q, mem_kt, mem_kv)


def _mem_attn_sample(o, q, cache_k, cache_v, layer):
    t = DEC_SEQ
    rb = M_PROMPT // t
    nbytes = 2 * (2 * t * D_MODEL * 2 + 2 * N_MEM * D_MODEL * 4) + 6 * t * N_MEM * 4
    return pl.pallas_call(
        _mem_attn_sample_kernel,
        name="mem_attn_sample",
        grid=(DEC_BATCH,),
        in_specs=[pl.BlockSpec(memory_space=pl.ANY),
                  pl.BlockSpec((t, D_MODEL), lambda b: (rb + b, 0)),
                  pl.BlockSpec((N_MEM, D_MODEL), lambda b: (layer * DEC_BATCH + b, 0)),
                  pl.BlockSpec((N_MEM, D_MODEL), lambda b: (layer * DEC_BATCH + b, 0))],
        out_specs=pl.BlockSpec((t, D_MODEL), lambda b: (rb + b, 0)),
        out_shape=jax.ShapeDtypeStruct(o.shape, o.dtype),
        input_output_aliases={0: 0},
        compiler_params=_params(("arbitrary",), nbytes),
    )(o, q, cache_k, cache_v)


def _pool_body(x_ref, halo, pos0, n_hist, wp_ref, ps_ref, g_ref, b_ref, of_ref, ob_ref,
               ext_ref, sa_ref, sb_ref):
    tm = x_ref.shape[0]
    top = 2 * HALO
    rows = tm + top
    zeros = jnp.zeros((HALO, D_MODEL), F32)
    ext_ref[0:HALO, :] = zeros
    ext_ref[HALO:top, :] = halo
    ext_ref[top:rows, :] = x_ref[...]
    sa_ref[0:HALO, :] = zeros[:, :POOL_GROUP]
    sb_ref[0:HALO, :] = zeros[:, :POOL_GROUP]
    seen = (pos0 + n_hist + 1 + lax.broadcasted_iota(jnp.int32, (tm, 1), 0)).astype(F32)
    bufs = (sa_ref, sb_ref)
    for gi, win in enumerate(POOL_WINDOWS):
        cs = slice(gi * POOL_GROUP, (gi + 1) * POOL_GROUP)
        src, d, lvl = None, 1, 0
        while 2 * d < win:
            if src is None:
                val = ext_ref[HALO:rows, cs] + ext_ref[HALO - d:rows - d, cs]
            else:
                val = src[HALO:rows, :] + src[HALO - d:rows - d, :]
            dst = bufs[lvl % 2]
            dst[HALO:rows, :] = val
            src, d, lvl = dst, 2 * d, lvl + 1
        if src is None:
            wsum = ext_ref[top:rows, cs] + ext_ref[top - d:rows - d, cs]
        else:
            wsum = src[top:rows, :] + src[top - d:rows - d, :]
        xg = x_ref[:, cs]
        u = wsum / jnp.minimum(seen, float(win)) - xg
        y = jnp.dot(u.astype(BF16), wp_ref[gi], preferred_element_type=F32) * ps_ref[:, cs]
        of_ref[:, cs] = ALPHA * xg + y
    out = _layer_norm(of_ref[...], g_ref[...], b_ref[...])
    of_ref[...] = out
    ob_ref[...] = out.astype(BF16)


def _pool_prompt_kernel(x_ref, halo_ref, wp_ref, ps_ref, g_ref, b_ref, of_ref, ob_ref,
                        ext_ref, sa_ref, sb_ref):
    it = pl.program_id(0) % (SEQ // TM_POOL)
    halo = jnp.where(it > 0, halo_ref[...], 0.0)
    _pool_body(x_ref, halo, it * TM_POOL, 0, wp_ref, ps_ref, g_ref, b_ref, of_ref, ob_ref,
               ext_ref, sa_ref, sb_ref)


def _pool_sample_kernel(of_any, ob_any, x_ref, halo_ref, wp_ref, ps_ref, g_ref, b_ref, of_ref, ob_ref,
                        ext_ref, sa_ref, sb_ref):
    del of_any, ob_any
    _pool_body(x_ref, halo_ref[...], 0, POOL_HIST, wp_ref, ps_ref, g_ref, b_ref, of_ref, ob_ref,
               ext_ref, sa_ref, sb_ref)


def _pool_scratch(tm):
    return [pltpu.VMEM((tm + 2 * HALO, D_MODEL), F32),
            pltpu.VMEM((tm + 2 * HALO, POOL_GROUP), F32),
            pltpu.VMEM((tm + 2 * HALO, POOL_GROUP), F32)]


def _pool_const_specs(p, gi):
    return [pl.BlockSpec((None, len(POOL_WINDOWS), POOL_GROUP, POOL_GROUP), lambda i: (p, 0, 0, 0)),
            pl.BlockSpec((None, 1, D_MODEL), lambda i: (p, 0, 0))] + _ln_param_specs(gi)


def _pool_bytes(tm):
    return (2 * tm * D_MODEL * 4 + 2 * HALO * D_MODEL * 4 + 2 * D_MODEL * POOL_GROUP * 2 + 2 * tm * D_MODEL * 6
            + (tm + 2 * HALO) * (D_MODEL + 2 * POOL_GROUP) * 4 + 2 * tm * D_MODEL * 4)


def _pool_prompt(x, wp, ps, p, gb, gi):
    tm = TM_POOL
    hb = tm // HALO
    return pl.pallas_call(
        _pool_prompt_kernel,
        name="pool_prompt",
        grid=(M_PROMPT // tm,),
        in_specs=[pl.BlockSpec((tm, D_MODEL), lambda i: (i, 0)),
                  pl.BlockSpec((HALO, D_MODEL), lambda i: (jnp.maximum(i * hb - 1, 0), 0))]
                 + _pool_const_specs(p, gi),
        out_specs=[pl.BlockSpec((tm, D_MODEL), lambda i: (i, 0)),
                   pl.BlockSpec((tm, D_MODEL), lambda i: (i, 0))],
        out_shape=[jax.ShapeDtypeStruct((M_ALL, D_MODEL), F32),
                   jax.ShapeDtypeStruct((M_ALL, D_MODEL), BF16)],
        scratch_shapes=_pool_scratch(tm),
        compiler_params=_params(("arbitrary",), _pool_bytes(tm)),
    )(x, x, wp, ps, *gb)


def _pool_sample(of, ob, x, state, wp, ps, p, gb, gi):
    tm = DEC_SEQ
    rb = M_PROMPT // tm
    return pl.pallas_call(
        _pool_sample_kernel,
        name="pool_sample",
        grid=(DEC_BATCH,),
        in_specs=[pl.BlockSpec(memory_space=pl.ANY),
                  pl.BlockSpec(memory_space=pl.ANY),
                  pl.BlockSpec((tm, D_MODEL), lambda i: (rb + i, 0)),
                  pl.BlockSpec((None, HALO, D_MODEL), lambda i: (p * DEC_BATCH + i, 0, 0))]
                 + _pool_const_specs(p, gi),
        out_specs=[pl.BlockSpec((tm, D_MODEL), lambda i: (rb + i, 0)),
                   pl.BlockSpec((tm, D_MODEL), lambda i: (rb + i, 0))],
        out_shape=[jax.ShapeDtypeStruct(of.shape, of.dtype),
                   jax.ShapeDtypeStruct(ob.shape, ob.dtype)],
        input_output_aliases={0: 0, 1: 1},
        scratch_shapes=_pool_scratch(tm),
        compiler_params=_params(("arbitrary",), _pool_bytes(tm)),
    )(of, ob, x, state, wp, ps, *gb)


def _pair_bias_kernel(vec_ref, o_ref):
    period = vec_ref.shape[-1]
    toe = pltpu.roll(jnp.broadcast_to(vec_ref[...], (PAIR, period)), 0, 1, stride=1, stride_axis=0)
    r = lax.broadcasted_iota(jnp.int32, (PAIR, PAIR_KEYS), 0)
    c = lax.broadcasted_iota(jnp.int32, (PAIR, PAIR_KEYS), 1)
    off = c - (r // CHUNK) * CHUNK
    in_band = jnp.logical_and(off >= 0, off < BAND_KEYS)
    o_ref[...] = jnp.where(in_band, toe[:, :PAIR_KEYS], NEG_BIG)


def _pair_bias(table):
    lead = table.shape[:-1]
    n = lead[0] * lead[1]
    period = PAIR + PAIR_KEYS
    far = table[..., 2 * REL_CLIP:]
    vec = jnp.concatenate([jnp.broadcast_to(far, lead + (BAND_LEFT - REL_CLIP,)),
                           jnp.flip(table[..., 1:], axis=-1),
                           jnp.broadcast_to(far, lead + (period - BAND_LEFT - REL_CLIP,))], axis=-1)
    out = pl.pallas_call(
        _pair_bias_kernel,
        name="pair_bias",
        grid=(n,),
        in_specs=[pl.BlockSpec((None, 1, period), lambda i: (i, 0, 0))],
        out_specs=pl.BlockSpec((None, PAIR, PAIR_KEYS), lambda i: (i, 0, 0)),
        out_shape=jax.ShapeDtypeStruct((n, PAIR, PAIR_KEYS), F32),
        compiler_params=_params(("arbitrary",), 4 * PAIR * period * 4),
    )(vec.reshape(n, 1, period).astype(F32))
    return out.reshape(lead + (PAIR, PAIR_KEYS))


def _tail_rows(x, n_tail):
    return jnp.stack([x[(bi + 1) * SEQ - n_tail:(bi + 1) * SEQ] for bi in range(BATCH)])


def kernel(x_prompt, x_sample, cache_attn_k, cache_attn_v, state_pool, cache_mem_k, cache_mem_v,
           mem_prompt, w_qkv, w_attn_o, rel_bias, w_pool, pool_scale, w_mem_q, w_mem_kv, w_mem_o,
           w_ffn_in, w_ffn_out, ln_g, ln_b):
    D = D_MODEL
    assert cache_attn_k.shape[2] == BAND_LEFT and DEPTH % 2 == 0
    xp = x_prompt.reshape(M_PROMPT, D)
    xs = x_sample.reshape(M_SAMPLE, D)
    x_f = None
    x_b = _ingest(xp, xs)
    mem_b = mem_prompt.reshape(BATCH * N_MEM, D).astype(BF16)
    w_pool_b = w_pool.astype(BF16)
    gb = (ln_g.reshape(DEPTH * 3, 1, D), ln_b.reshape(DEPTH * 3, 1, D))
    pscale = pool_scale.reshape(-1, 1, D)
    cache_ak = cache_attn_k.reshape(-1, HEAD_DIM)
    cache_av = cache_attn_v.reshape(-1, HEAD_DIM)
    cache_mk = cache_mem_k.reshape(-1, D)
    cache_mv = cache_mem_v.reshape(-1, D)
    state_pad = jnp.pad(state_pool, ((0, 0), (0, 0), (HALO - POOL_HIST, 0), (0, 0))).reshape(-1, HALO, D)
    bias = _pair_bias(rel_bias)
    mem_kv, mem_kt = _mem_kv_all(mem_b, w_mem_kv)
    k_cols, v_cols = slice(0, D), slice(D, 2 * D)

    ak_p, av_p, pool_p, ak_s, av_s, pool_s = [], [], [], [], [], []
    for i in range(DEPTH):
        if i % 2 == 0:
            a = i // 2
            q, wo_b = _matmul(x_b, w_qkv, a, 0, D, TM_MM, (BF16,), scale=HEAD_DIM ** -0.5, cast=(w_attn_o, a))
            k, kt = _matmul(x_b, w_qkv, a, D, D, TM_MM, (F32,), transposed=True)
            v, v_b = _matmul(x_b, w_qkv, a, 2 * D, D, TM_MM, (F32, BF16))
            o = _band_attn_prompt(q, kt, v_b, bias, a)
            o = _band_attn_sample(o, q, k, v_b, cache_ak, cache_av, bias, a)
            ak_p.append(_tail_rows(k, BAND_LEFT).reshape(BATCH, BAND_LEFT, N_HEADS, HEAD_DIM))
            av_p.append(_tail_rows(v, BAND_LEFT).reshape(BATCH, BAND_LEFT, N_HEADS, HEAD_DIM))
            ak_s.append(k[M_PROMPT:].reshape(DEC_BATCH, DEC_SEQ, N_HEADS, HEAD_DIM))
            av_s.append(v[M_PROMPT:].reshape(DEC_BATCH, DEC_SEQ, N_HEADS, HEAD_DIM))
            if i == 0:
                outs = _mm_ln(o, wo_b, None, xp, gb, 0, tm=TM_LN_PROMPT, n_tiles=M_PROMPT // TM_LN_PROMPT,
                              a_blk0=0, x_blk0=0, o_blk0=0, out_rows=M_ALL, out_dtypes=(F32, BF16))
                x_f, x_b = _mm_ln(o, wo_b, None, xs, gb, 0, tm=M_SAMPLE, n_tiles=1,
                                  a_blk0=M_PROMPT // M_SAMPLE, x_blk0=0, o_blk0=M_PROMPT // M_SAMPLE,
                                  out_rows=M_ALL, out_dtypes=(F32, BF16), into=outs)
            else:
                x_f, x_b = _mm_ln_all(o, wo_b, None, x_f, gb, 3 * i)
        else:
            p = i // 2
            pool_p.append(_tail_rows(x_f, POOL_HIST))
            ext = jnp.concatenate([state_pool[p], x_f[M_PROMPT:].reshape(DEC_BATCH, DEC_SEQ, D)], axis=1)
            pool_s.append(ext[:, DEC_SEQ:])
            y_f, y_b = _pool_prompt(x_f, w_pool_b, pscale, p, gb, 3 * i)
            x_f, x_b = _pool_sample(y_f, y_b, x_f, state_pad, w_pool_b, pscale, p, gb, 3 * i)
        q, wmo_b = _matmul(x_b, w_mem_q, i, 0, D, TM_MM, (BF16,), scale=MEM_HEAD_DIM ** -0.5, cast=(w_mem_o, i))
        o = _mem_attn_prompt(q, mem_kt, mem_kv, i)
        o = _mem_attn_sample(o, q, cache_mk, cache_mv, i)
        x_f, x_b = _mm_ln_all(o, wmo_b, None, x_f, gb, 3 * i + 1)
        h, wfo_b = _swiglu_in(x_b, w_ffn_in, w_ffn_out, i)
        if i < DEPTH - 1:
            x_f, x_b = _mm_ln_all(h, wfo_b, None, x_f, gb, 3 * i + 2)
        else:
            y_p, = _mm_ln(h, wfo_b, None, x_f, gb, 3 * i + 2, tm=TM_LN_WIDE,
                          n_tiles=M_PROMPT // TM_LN_WIDE, a_blk0=0, x_blk0=0, o_blk0=0,
                          out_rows=M_PROMPT, out_dtypes=(F32,))
            y_s, = _mm_ln(h, wfo_b, None, x_f, gb, 3 * i + 2, tm=M_SAMPLE, n_tiles=1,
                          a_blk0=M_PROMPT // M_SAMPLE, x_blk0=M_PROMPT // M_SAMPLE, o_blk0=0,
                          out_rows=M_SAMPLE, out_dtypes=(F32,))

    mem_shape = (DEPTH, BATCH, N_MEM, MEM_HEADS, MEM_HEAD_DIM)
    return (y_p.reshape(BATCH, SEQ, D), y_s.reshape(DEC_BATCH, DEC_SEQ, D),
            jnp.stack(ak_p), jnp.stack(av_p), jnp.stack(pool_p),
            mem_kv[:, :, k_cols].reshape(mem_shape), mem_kv[:, :, v_cols].reshape(mem_shape),
            jnp.stack(ak_s), jnp.stack(av_s), jnp.stack(pool_s))
```

```python
import functools

import jax
import jax.numpy as jnp
from jax import lax
from jax.experimental import pallas as pl
from jax.experimental.pallas import tpu as pltpu

F32 = jnp.float32
BF16 = jnp.bfloat16

D_MODEL = 2048
BATCH = 4
SEQ = 4096
DEPTH = 4
DEC_BATCH = 8
DEC_SEQ = 32
CHUNK = 64
BAND_LEFT = 512
BAND_KEYS = BAND_LEFT + CHUNK
N_HEADS = 16
HEAD_DIM = D_MODEL // N_HEADS
REL_CLIP = 2 * CHUNK
POOL_WINDOWS = (2, 4, 8, 16)
POOL_GROUP = D_MODEL // len(POOL_WINDOWS)
POOL_HIST = max(POOL_WINDOWS) - 1
N_MEM = 256
MEM_HEADS = 4
MEM_HEAD_DIM = D_MODEL // MEM_HEADS
D_FF = 5632
ALPHA = (2.0 * DEPTH) ** 0.25
LN_EPS = 1e-5
NEG_BIG = -1e30

M_PROMPT = BATCH * SEQ
M_SAMPLE = DEC_BATCH * DEC_SEQ
M_ALL = M_PROMPT + M_SAMPLE

TM_INGEST = 1024
TM_MM = 1280
TN_MM = 1024
TN_FF = 512
FF_SUB = 2
TM_LN = 640
TM_LN_PROMPT = 512
TM_LN_WIDE = 256
LN_SLAB = 128
LN_PIECES = 2
QB = 512
PAIR = 4 * CHUNK
PAIR_KEYS = BAND_LEFT + PAIR
TM_MEM = 1024
MEM_SLAB = 256
TM_POOL = 512
HALO = 16

V7X_VMEM_BYTES = 64 * 1024 * 1024


def _vmem_limit(nbytes):
    return int(min(nbytes + (12 << 20), V7X_VMEM_BYTES - (6 << 20)))


def _params(sem, nbytes):
    return pltpu.CompilerParams(dimension_semantics=sem, vmem_limit_bytes=_vmem_limit(nbytes))


def _ingest_kernel(*refs):
    x_ref, o_ref = refs[-2:]
    o_ref[...] = x_ref[...].astype(BF16)


def _ingest(xp, xs):
    def call(x, tm, o_blk0, into):
        n_alias = len(into)
        return pl.pallas_call(
            _ingest_kernel,
            name="ingest",
            grid=(x.shape[0] // tm,),
            in_specs=[pl.BlockSpec(memory_space=pl.ANY)] * n_alias
                     + [pl.BlockSpec((tm, D_MODEL), lambda i: (i, 0))],
            out_specs=pl.BlockSpec((tm, D_MODEL), lambda i: (o_blk0 + i, 0)),
            out_shape=jax.ShapeDtypeStruct((M_ALL, D_MODEL), BF16),
            input_output_aliases={j: j for j in range(n_alias)},
            compiler_params=_params(("arbitrary",), 2 * tm * D_MODEL * (4 + 2)),
        )(*into, x)

    out = call(xp, TM_INGEST, 0, ())
    return call(xs, M_SAMPLE, M_PROMPT // M_SAMPLE, (out,))


def _cast_rider(cast, n_inner, n_steps):
    if cast is None:
        return [], [], [], 0
    w, layer = cast
    _, k, d = w.shape
    rows = next(r for r in (16, 32, 64, 128, 256, 512) if k % r == 0 and k // r <= n_steps)
    last = k // rows - 1
    chunk = lambda j, i: jnp.minimum(j * n_inner + i, last)
    return ([pl.BlockSpec((None, rows, d), lambda j, i: (layer, chunk(j, i), 0))],
            [pl.BlockSpec((rows, d), lambda j, i: (chunk(j, i), 0))],
            [jax.ShapeDtypeStruct((k, d), BF16)], 2 * rows * d * 6)


def _mm_kernel(*refs, scale, n_out, rider, transposed):
    n_in = 3 if rider else 2
    a_ref, w_ref = refs[:2]
    o_refs = refs[n_in:n_in + n_out]
    wb_ref = refs[-1]
    if rider:
        refs[n_in + n_out + transposed][...] = refs[2][...].astype(BF16)

    @pl.when(pl.program_id(1) == 0)
    def _():
        wb_ref[...] = w_ref[...].astype(BF16)

    acc = jnp.dot(a_ref[...], wb_ref[...], preferred_element_type=F32)
    if scale != 1.0:
        acc = acc * scale
    for o_ref in o_refs:
        o_ref[...] = acc.astype(o_ref.dtype)
    if transposed:
        refs[n_in + n_out][...] = acc.T.astype(BF16)


def _matmul(a, w, layer, col0, n_out, tm, out_dtypes, scale=1.0, cast=None, transposed=False):
    m, k = a.shape
    tn = TN_MM
    assert m % tm == 0 and n_out % tn == 0 and col0 % tn == 0
    cb = col0 // tn
    grid = (n_out // tn, m // tm)
    r_in, r_out, r_shape, r_bytes = _cast_rider(cast, grid[1], grid[0] * grid[1])
    osz = sum(jnp.dtype(d).itemsize for d in out_dtypes)
    t_out = [pl.BlockSpec((tn, tm), lambda j, i: (j, i))] if transposed else []
    t_shape = [jax.ShapeDtypeStruct((n_out, m), BF16)] if transposed else []
    nbytes = (2 * tm * k * 2 + 2 * k * tn * 4 + k * tn * 2 + 2 * tm * tn * osz + tm * tn * 4 + r_bytes
              + transposed * tm * tn * (2 * 2 + 4))
    return pl.pallas_call(
        functools.partial(_mm_kernel, scale=scale, n_out=len(out_dtypes), rider=cast is not None,
                          transposed=transposed),
        name="mm",
        grid=grid,
        in_specs=[pl.BlockSpec((tm, k), lambda j, i: (i, 0)),
                  pl.BlockSpec((None, k, tn), lambda j, i: (layer, 0, j + cb))] + r_in,
        out_specs=[pl.BlockSpec((tm, tn), lambda j, i: (i, j)) for _ in out_dtypes] + t_out + r_out,
        out_shape=[jax.ShapeDtypeStruct((m, n_out), d) for d in out_dtypes] + t_shape + r_shape,
        scratch_shapes=[pltpu.VMEM((k, tn), BF16)],
        compiler_params=_params(("arbitrary", "arbitrary"), nbytes),
    )(a, w, *(cast[:1] if cast else ()))


def _mem_kv_kernel(a_ref, w_ref, k_ref, v_ref, kt_ref, *, n_key_steps):
    acc = jnp.dot(a_ref[...], w_ref[...].astype(BF16), preferred_element_type=F32)
    is_key = pl.program_id(1) < n_key_steps

    @pl.when(is_key)
    def _():
        k_ref[...] = acc
        kt_ref[...] = acc.T.astype(BF16)

    @pl.when(jnp.logical_not(is_key))
    def _():
        v_ref[...] = acc


def _mem_kv_all(mem, w):
    m, k = mem.shape
    n = w.shape[2] // 2
    tn = TN_MM
    nk = n // tn
    key_tile = lambda j: jnp.minimum(j, nk - 1)
    nbytes = 2 * m * k * 2 + 2 * k * tn * 4 + k * tn * 2 + 4 * m * tn * 4 + 2 * m * tn * 4 + 2 * m * tn * 2
    return pl.pallas_call(
        functools.partial(_mem_kv_kernel, n_key_steps=nk),
        name="mem_kv",
        grid=(w.shape[0], 2 * nk),
        in_specs=[pl.BlockSpec((m, k), lambda l, j: (0, 0)),
                  pl.BlockSpec((None, k, tn), lambda l, j: (l, 0, j))],
        out_specs=[pl.BlockSpec((None, m, tn), lambda l, j: (l, 0, key_tile(j))),
                   pl.BlockSpec((None, m, tn), lambda l, j: (l, 0, jnp.maximum(j - nk, 0))),
                   pl.BlockSpec((None, tn, m), lambda l, j: (l, key_tile(j), 0))],
        out_shape=[jax.ShapeDtypeStruct((w.shape[0], m, n), F32),
                   jax.ShapeDtypeStruct((w.shape[0], m, n), F32),
                   jax.ShapeDtypeStruct((w.shape[0], n, m), BF16)],
        compiler_params=_params(("arbitrary", "arbitrary"), nbytes),
    )(mem, w)


def _swiglu_kernel(a_ref, wg_ref, wu_ref, wc_ref, o_ref, wcb_ref, wgb_ref, wub_ref):
    wcb_ref[...] = wc_ref[...].astype(BF16)

    @pl.when(pl.program_id(1) == 0)
    def _():
        wgb_ref[...] = wg_ref[...].astype(BF16)
        wub_ref[...] = wu_ref[...].astype(BF16)

    ts = a_ref.shape[0] // FF_SUB

    def project(r):
        a = a_ref[r * ts:(r + 1) * ts, :]
        return (jnp.dot(a, wgb_ref[...], preferred_element_type=F32),
                jnp.dot(a, wub_ref[...], preferred_element_type=F32))

    def finish(r, gate, up):
        o_ref[r * ts:(r + 1) * ts, :] = (gate / (1.0 + jnp.exp(-gate)) * up).astype(o_ref.dtype)

    pending = None
    for r in range(FF_SUB):
        gate_up = project(r)
        if pending is not None:
            finish(*pending)
        pending = (r, *gate_up)
    finish(*pending)


def _swiglu_in(a, w_in, w_out, layer):
    m, k = a.shape
    tm, tn = TM_MM, TN_FF
    nj = D_FF // tn
    grid = (nj, m // tm)
    r_in, r_out, r_shape, r_bytes = _cast_rider((w_out, layer), grid[1], grid[0] * grid[1])
    nbytes = (2 * tm * k * 2 + 4 * k * tn * 4 + 2 * k * tn * 2 + 2 * tm * tn * 2 + 3 * tm * tn * 4
              + r_bytes)
    return pl.pallas_call(
        _swiglu_kernel,
        name="swiglu_in",
        grid=grid,
        in_specs=[pl.BlockSpec((tm, k), lambda j, i: (i, 0)),
                  pl.BlockSpec((None, k, tn), lambda j, i: (layer, 0, j)),
                  pl.BlockSpec((None, k, tn), lambda j, i: (layer, 0, j + nj))] + r_in,
        out_specs=[pl.BlockSpec((tm, tn), lambda j, i: (i, j))] + r_out,
        out_shape=[jax.ShapeDtypeStruct((m, D_FF), BF16)] + r_shape,
        scratch_shapes=[pltpu.VMEM((k, tn), BF16), pltpu.VMEM((k, tn), BF16)],
        compiler_params=_params(("arbitrary", "arbitrary"), nbytes),
    )(a, w_in, w_in, w_out)


def _layer_norm(z, g, b):
    mu = jnp.mean(z, axis=-1, keepdims=True)
    zc = z - mu
    var = jnp.mean(zc * zc, axis=-1, keepdims=True)
    return zc * lax.rsqrt(var + LN_EPS) * g + b


def _mm_ln_kernel(*refs, n_alias, n_sub):
    a_ref, w_ref, x_ref, g_ref, b_ref = refs[n_alias:n_alias + 5]
    o_refs = refs[n_alias + 5:-1]
    z_ref = refs[-1]
    ts = z_ref.shape[1]
    tc = D_MODEL // LN_PIECES
    tr = ts // LN_PIECES

    def pre_norm(r, c):
        rs, cs = slice(r * ts, (r + 1) * ts), slice(c * tc, (c + 1) * tc)
        z_ref[r, :, cs] = ALPHA * x_ref[rs, cs] + jnp.dot(a_ref[rs, :], w_ref[:, cs],
                                                         preferred_element_type=F32)

    def finish(r, c):
        y = _layer_norm(z_ref[r, c * tr:(c + 1) * tr, :], g_ref[...], b_ref[...])
        for o_ref in o_refs:
            o_ref[r * ts + c * tr:r * ts + (c + 1) * tr, :] = y.astype(o_ref.dtype)

    for r in range(n_sub + 1):
        for c in range(LN_PIECES):
            if r < n_sub:
                pre_norm(r, c)
            if r > 0:
                finish(r - 1, c)


def _ln_param_specs(gi):
    return [pl.BlockSpec((None, 1, D_MODEL), lambda *_: (gi, 0, 0)),
            pl.BlockSpec((None, 1, D_MODEL), lambda *_: (gi, 0, 0))]


def _mm_ln(a, w, layer, x, gb, gi, *, tm, n_tiles, a_blk0, x_blk0, o_blk0, out_rows, out_dtypes, into=None):
    kdim = a.shape[1]
    assert tm % LN_SLAB == 0
    n_sub = tm // LN_SLAB
    n_alias = 0 if into is None else len(into)
    if layer is None:
        w_spec = pl.BlockSpec((kdim, D_MODEL), lambda i: (0, 0), pipeline_mode=pl.Buffered(1))
    else:
        w_spec = pl.BlockSpec((None, kdim, D_MODEL), lambda i: (layer, 0, 0), pipeline_mode=pl.Buffered(1))
    osz = sum(jnp.dtype(d).itemsize for d in out_dtypes)
    nbytes = (2 * tm * kdim * 2 + kdim * D_MODEL * 2 + 2 * tm * D_MODEL * 4 + 2 * tm * D_MODEL * osz
              + tm * D_MODEL * 4 + 2 * LN_SLAB * D_MODEL * 4)
    return pl.pallas_call(
        functools.partial(_mm_ln_kernel, n_alias=n_alias, n_sub=n_sub),
        name="mm_ln",
        grid=(n_tiles,),
        in_specs=[pl.BlockSpec(memory_space=pl.ANY)] * n_alias + [
            pl.BlockSpec((tm, kdim), lambda i: (a_blk0 + i, 0)),
            w_spec,
            pl.BlockSpec((tm, D_MODEL), lambda i: (x_blk0 + i, 0))] + _ln_param_specs(gi),
        out_specs=[pl.BlockSpec((tm, D_MODEL), lambda i: (o_blk0 + i, 0)) for _ in out_dtypes],
        out_shape=[jax.ShapeDtypeStruct((out_rows, D_MODEL), d) for d in out_dtypes],
        input_output_aliases={j: j for j in range(n_alias)},
        scratch_shapes=[pltpu.VMEM((n_sub, LN_SLAB, D_MODEL), F32)],
        compiler_params=_params(("arbitrary",), nbytes),
    )(*(into or ()), a, w, x, *gb)


def _mm_ln_all(a, w, layer, x, gb, gi, out_dtypes=(F32, BF16)):
    tm = TM_LN if a.shape[1] <= D_MODEL else TM_LN_WIDE
    return _mm_ln(a, w, layer, x, gb, gi, tm=tm, n_tiles=M_ALL // tm, a_blk0=0, x_blk0=0, o_blk0=0,
                  out_rows=M_ALL, out_dtypes=out_dtypes)


def _dot_nt(a, b):
    return lax.dot_general(a, b, (((1,), (1,)), ((), ())), preferred_element_type=F32)


def _band_attn_kernel(q_ref, ktl_ref, ktc_ref, vl_ref, vc_ref, bias_ref, o_ref, kt_sc, v_sc):
    kt_sc[:, 0:QB] = ktl_ref[...]
    kt_sc[:, QB:2 * QB] = ktc_ref[...]
    v_sc[0:QB, :] = vl_ref[...]
    v_sc[QB:2 * QB, :] = vc_ref[...]
    no_left = pl.program_id(1) == 0
    col = lax.broadcasted_iota(jnp.int32, (PAIR, PAIR_KEYS), 1)

    def scores(h, r0):
        sl = slice(h * HEAD_DIM, (h + 1) * HEAD_DIM)
        s = jnp.dot(q_ref[r0:r0 + PAIR, sl], kt_sc[sl, r0:r0 + PAIR_KEYS],
                    preferred_element_type=F32) + bias_ref[h]
        return jnp.where(jnp.logical_and(no_left, col < QB - r0), NEG_BIG, s)

    def weights(s):
        m = jnp.max(s, axis=-1, keepdims=True)
        p = jnp.exp(s - m)
        return p.astype(BF16), jnp.sum(p, axis=-1, keepdims=True)

    def finish(h, r0, p, denom):
        sl = slice(h * HEAD_DIM, (h + 1) * HEAD_DIM)
        o = jnp.dot(p, v_sc[r0:r0 + PAIR_KEYS, sl], preferred_element_type=F32)
        o_ref[r0:r0 + PAIR, sl] = (o / denom).astype(o_ref.dtype)

    units = [(h, pr * PAIR) for h in range(N_HEADS) for pr in range(QB // PAIR)]
    scored, weighted = None, None
    for unit in units + [None, None]:
        s = scores(*unit) if unit is not None else None
        nxt = (*scored[0], *weights(scored[1])) if scored is not None else None
        if weighted is not None:
            finish(*weighted)
        scored = (unit, s) if unit is not None else None
        weighted = nxt


def _band_attn_prompt(q, kt, v, bias, a):
    nb = SEQ // QB
    cur = lambda b, i: (b * nb + i, 0)
    left = lambda b, i: (b * nb + jnp.maximum(i - 1, 0), 0)
    cur_t = lambda b, i: (0, b * nb + i)
    left_t = lambda b, i: (0, b * nb + jnp.maximum(i - 1, 0))
    nbytes = (2 * 6 * QB * D_MODEL * 2 + N_HEADS * PAIR * PAIR_KEYS * 4 + 4 * QB * D_MODEL * 2
              + 8 * PAIR * PAIR_KEYS * 4)
    return pl.pallas_call(
        _band_attn_kernel,
        name="band_attn_prompt",
        grid=(BATCH, nb),
        in_specs=[pl.BlockSpec((QB, D_MODEL), cur),
                  pl.BlockSpec((D_MODEL, QB), left_t),
                  pl.BlockSpec((D_MODEL, QB), cur_t),
                  pl.BlockSpec((QB, D_MODEL), left),
                  pl.BlockSpec((QB, D_MODEL), cur),
                  pl.BlockSpec((None, N_HEADS, PAIR, PAIR_KEYS), lambda b, i: (a, 0, 0, 0),
                               pipeline_mode=pl.Buffered(1))],
        out_specs=pl.BlockSpec((QB, D_MODEL), cur),
        out_shape=jax.ShapeDtypeStruct((M_ALL, D_MODEL), BF16),
        scratch_shapes=[pltpu.VMEM((D_MODEL, 2 * QB), BF16), pltpu.VMEM((2 * QB, D_MODEL), BF16)],
        compiler_params=_params(("arbitrary", "arbitrary"), nbytes),
    )(q, kt, kt, v, v, bias)


def _band_attn_sample_kernel(o_any, q_ref, kn_ref, vn_ref, ck_ref, cv_ref, bias_ref, o_ref):
    del o_any
    past = ck_ref.shape[0] // N_HEADS
    for h in range(N_HEADS):
        sl = slice(h * HEAD_DIM, (h + 1) * HEAD_DIM)
        rows = pl.ds(h, past, stride=N_HEADS)
        q = q_ref[:, sl]
        s_c = _dot_nt(q, ck_ref[rows, :].astype(BF16)) + bias_ref[h, 0:DEC_SEQ, 0:past]
        s_n = _dot_nt(q, kn_ref[:, sl].astype(BF16)) + bias_ref[h, 0:DEC_SEQ, past:past + DEC_SEQ]
        m = jnp.maximum(jnp.max(s_c, axis=-1, keepdims=True), jnp.max(s_n, axis=-1, keepdims=True))
        p_c = jnp.exp(s_c - m)
        p_n = jnp.exp(s_n - m)
        denom = jnp.sum(p_c, axis=-1, keepdims=True) + jnp.sum(p_n, axis=-1, keepdims=True)
        o = (jnp.dot(p_c.astype(BF16), cv_ref[rows, :].astype(BF16), preferred_element_type=F32)
             + jnp.dot(p_n.astype(BF16), vn_ref[:, sl], preferred_element_type=F32))
        o_ref[:, sl] = (o / denom).astype(o_ref.dtype)


def _band_attn_sample(o, q, k, v, cache_k, cache_v, bias, a):
    t = DEC_SEQ
    rb = M_PROMPT // t
    cr = BAND_LEFT * N_HEADS
    nbytes = (2 * (5 * t * D_MODEL * 2 + 2 * cr * HEAD_DIM * 4) + N_HEADS * PAIR * PAIR_KEYS * 4
              + 4 * BAND_LEFT * HEAD_DIM * 4)
    return pl.pallas_call(
        _band_attn_sample_kernel,
        name="band_attn_sample",
        grid=(DEC_BATCH,),
        in_specs=[pl.BlockSpec(memory_space=pl.ANY),
                  pl.BlockSpec((t, D_MODEL), lambda b: (rb + b, 0)),
                  pl.BlockSpec((t, D_MODEL), lambda b: (rb + b, 0)),
                  pl.BlockSpec((t, D_MODEL), lambda b: (rb + b, 0)),
                  pl.BlockSpec((cr, HEAD_DIM), lambda b: (a * DEC_BATCH + b, 0)),
                  pl.BlockSpec((cr, HEAD_DIM), lambda b: (a * DEC_BATCH + b, 0)),
                  pl.BlockSpec((None, N_HEADS, PAIR, PAIR_KEYS), lambda b: (a, 0, 0, 0),
                               pipeline_mode=pl.Buffered(1))],
        out_specs=pl.BlockSpec((t, D_MODEL), lambda b: (rb + b, 0)),
        out_shape=jax.ShapeDtypeStruct(o.shape, o.dtype),
        input_output_aliases={0: 0},
        compiler_params=_params(("arbitrary",), nbytes),
    )(o, q, k, v, cache_k, cache_v, bias)


def _mem_attn_body(q_ref, k_ref, v_ref, o_ref, keys_transposed):
    ts = min(q_ref.shape[0], MEM_SLAB)
    head = lambda h: slice(h * MEM_HEAD_DIM, (h + 1) * MEM_HEAD_DIM)
    vb = [v_ref[:, head(h)].astype(BF16) for h in range(MEM_HEADS)]
    if not keys_transposed:
        kb = [k_ref[:, head(h)].astype(BF16) for h in range(MEM_HEADS)]

    def scores(h, r0):
        q = q_ref[r0:r0 + ts, head(h)]
        if keys_transposed:
            return jnp.dot(q, k_ref[head(h), :], preferred_element_type=F32)
        return _dot_nt(q, kb[h])

    def finish(h, r0, s):
        m = jnp.max(s, axis=-1, keepdims=True)
        p = jnp.exp(s - m)
        denom = jnp.sum(p, axis=-1, keepdims=True)
        o = jnp.dot(p.astype(BF16), vb[h], preferred_element_type=F32)
        o_ref[r0:r0 + ts, head(h)] = (o / denom).astype(o_ref.dtype)

    units = [(h, r0) for h in range(MEM_HEADS) for r0 in range(0, q_ref.shape[0], ts)]
    pending = None
    for h, r0 in units:
        s = scores(h, r0)
        if pending is not None:
            finish(*pending)
        pending = (h, r0, s)
    finish(*pending)


def _mem_attn_prompt_kernel(q_ref, kt_ref, v_ref, o_ref):
    _mem_attn_body(q_ref, kt_ref, v_ref, o_ref, True)


def _mem_attn_sample_kernel(o_any, q_ref, k_ref, v_ref, o_ref):
    del o_any
    _mem_attn_body(q_ref, k_ref, v_ref, o_ref, False)


def _mem_attn_prompt(q, mem_kt, mem_v, layer):
    tm = TM_MEM
    nt = SEQ // tm
    nbytes = 2 * (tm * D_MODEL * 2 + N_MEM * D_MODEL * 6 + tm * D_MODEL * 2) + 6 * tm * N_MEM * 4
    return pl.pallas_call(
        _mem_attn_prompt_kernel,
        name="mem_attn_prompt",
        grid=(BATCH, nt),
        in_specs=[pl.BlockSpec((tm, D_MODEL), lambda b, i: (b * nt + i, 0)),
                  pl.BlockSpec((None, D_MODEL, N_MEM), lambda b, i: (layer, 0, b)),
                  pl.BlockSpec((None, N_MEM, D_MODEL), lambda b, i: (layer, b, 0))],
        out_specs=pl.BlockSpec((tm, D_MODEL), lambda b, i: (b * nt + i, 0)),
        out_shape=jax.ShapeDtypeStruct((M_ALL, D_MODEL), BF16),
        compiler_params=_params(("arbitrary", "arbitrary"), nbytes),
    )(q, mem_kt, mem_v)


def _mem_attn_sample(o, q, cache_k, cache_v, layer):
    t = DEC_SEQ
    rb = M_PROMPT // t
    nbytes = 2 * (2 * t * D_MODEL * 2 + 2 * N_MEM * D_MODEL * 4) + 6 * t * N_MEM * 4
    return pl.pallas_call(
        _mem_attn_sample_kernel,
        name="mem_attn_sample",
        grid=(DEC_BATCH,),
        in_specs=[pl.BlockSpec(memory_space=pl.ANY),
                  pl.BlockSpec((t, D_MODEL), lambda b: (rb + b, 0)),
                  pl.BlockSpec((N_MEM, D_MODEL), lambda b: (layer * DEC_BATCH + b, 0)),
                  pl.BlockSpec((N_MEM, D_MODEL), lambda b: (layer * DEC_BATCH + b, 0))],
        out_specs=pl.BlockSpec((t, D_MODEL), lambda b: (rb + b, 0)),
        out_shape=jax.ShapeDtypeStruct(o.shape, o.dtype),
        input_output_aliases={0: 0},
        compiler_params=_params(("arbitrary",), nbytes),
    )(o, q, cache_k, cache_v)


def _pool_body(x_ref, halo, pos0, n_hist, wp_ref, ps_ref, g_ref, b_ref, of_ref, ob_ref,
               ext_ref, sa_ref, sb_ref):
    tm = x_ref.shape[0]
    top = 2 * HALO
    rows = tm + top
    zeros = jnp.zeros((HALO, D_MODEL), F32)
    ext_ref[0:HALO, :] = zeros
    ext_ref[HALO:top, :] = halo
    ext_ref[top:rows, :] = x_ref[...]
    sa_ref[0:HALO, :] = zeros[:, :POOL_GROUP]
    sb_ref[0:HALO, :] = zeros[:, :POOL_GROUP]
    seen = (pos0 + n_hist + 1 + lax.broadcasted_iota(jnp.int32, (tm, 1), 0)).astype(F32)
    bufs = (sa_ref, sb_ref)
    for gi, win in enumerate(POOL_WINDOWS):
        cs = slice(gi * POOL_GROUP, (gi + 1) * POOL_GROUP)
        src, d, lvl = None, 1, 0
        while 2 * d < win:
            if src is None:
                val = ext_ref[HALO:rows, cs] + ext_ref[HALO - d:rows - d, cs]
            else:
                val = src[HALO:rows, :] + src[HALO - d:rows - d, :]
            dst = bufs[lvl % 2]
            dst[HALO:rows, :] = val
            src, d, lvl = dst, 2 * d, lvl + 1
        if src is None:
            wsum = ext_ref[top:rows, cs] + ext_ref[top - d:rows - d, cs]
        else:
            wsum = src[top:rows, :] + src[top - d:rows - d, :]
        xg = x_ref[:, cs]
        u = wsum / jnp.minimum(seen, float(win)) - xg
        y = jnp.dot(u.astype(BF16), wp_ref[gi], preferred_element_type=F32) * ps_ref[:, cs]
        of_ref[:, cs] = ALPHA * xg + y
    out = _layer_norm(of_ref[...], g_ref[...], b_ref[...])
    of_ref[...] = out
    ob_ref[...] = out.astype(BF16)


def _pool_prompt_kernel(x_ref, halo_ref, wp_ref, ps_ref, g_ref, b_ref, of_ref, ob_ref,
                        ext_ref, sa_ref, sb_ref):
    it = pl.program_id(0) % (SEQ // TM_POOL)
    halo = jnp.where(it > 0, halo_ref[...], 0.0)
    _pool_body(x_ref, halo, it * TM_POOL, 0, wp_ref, ps_ref, g_ref, b_ref, of_ref, ob_ref,
               ext_ref, sa_ref, sb_ref)


def _pool_sample_kernel(of_any, ob_any, x_ref, halo_ref, wp_ref, ps_ref, g_ref, b_ref, of_ref, ob_ref,
                        ext_ref, sa_ref, sb_ref):
    del of_any, ob_any
    _pool_body(x_ref, halo_ref[...], 0, POOL_HIST, wp_ref, ps_ref, g_ref, b_ref, of_ref, ob_ref,
               ext_ref, sa_ref, sb_ref)


def _pool_scratch(tm):
    return [pltpu.VMEM((tm + 2 * HALO, D_MODEL), F32),
            pltpu.VMEM((tm + 2 * HALO, POOL_GROUP), F32),
            pltpu.VMEM((tm + 2 * HALO, POOL_GROUP), F32)]


def _pool_const_specs(p, gi):
    return [pl.BlockSpec((None, len(POOL_WINDOWS), POOL_GROUP, POOL_GROUP), lambda i: (p, 0, 0, 0)),
            pl.BlockSpec((None, 1, D_MODEL), lambda i: (p, 0, 0))] + _ln_param_specs(gi)


def _pool_bytes(tm):
    return (2 * tm * D_MODEL * 4 + 2 * HALO * D_MODEL * 4 + 2 * D_MODEL * POOL_GROUP * 2 + 2 * tm * D_MODEL * 6
            + (tm + 2 * HALO) * (D_MODEL + 2 * POOL_GROUP) * 4 + 2 * tm * D_MODEL * 4)


def _pool_prompt(x, wp, ps, p, gb, gi):
    tm = TM_POOL
    hb = tm // HALO
    return pl.pallas_call(
        _pool_prompt_kernel,
        name="pool_prompt",
        grid=(M_PROMPT // tm,),
        in_specs=[pl.BlockSpec((tm, D_MODEL), lambda i: (i, 0)),
                  pl.BlockSpec((HALO, D_MODEL), lambda i: (jnp.maximum(i * hb - 1, 0), 0))]
                 + _pool_const_specs(p, gi),
        out_specs=[pl.BlockSpec((tm, D_MODEL), lambda i: (i, 0)),
                   pl.BlockSpec((tm, D_MODEL), lambda i: (i, 0))],
        out_shape=[jax.ShapeDtypeStruct((M_ALL, D_MODEL), F32),
                   jax.ShapeDtypeStruct((M_ALL, D_MODEL), BF16)],
        scratch_shapes=_pool_scratch(tm),
        compiler_params=_params(("arbitrary",), _pool_bytes(tm)),
    )(x, x, wp, ps, *gb)


def _pool_sample(of, ob, x, state, wp, ps, p, gb, gi):
    tm = DEC_SEQ
    rb = M_PROMPT // tm
    return pl.pallas_call(
        _pool_sample_kernel,
        name="pool_sample",
        grid=(DEC_BATCH,),
        in_specs=[pl.BlockSpec(memory_space=pl.ANY),
                  pl.BlockSpec(memory_space=pl.ANY),
                  pl.BlockSpec((tm, D_MODEL), lambda i: (rb + i, 0)),
                  pl.BlockSpec((None, HALO, D_MODEL), lambda i: (p * DEC_BATCH + i, 0, 0))]
                 + _pool_const_specs(p, gi),
        out_specs=[pl.BlockSpec((tm, D_MODEL), lambda i: (rb + i, 0)),
                   pl.BlockSpec((tm, D_MODEL), lambda i: (rb + i, 0))],
        out_shape=[jax.ShapeDtypeStruct(of.shape, of.dtype),
                   jax.ShapeDtypeStruct(ob.shape, ob.dtype)],
        input_output_aliases={0: 0, 1: 1},
        scratch_shapes=_pool_scratch(tm),
        compiler_params=_params(("arbitrary",), _pool_bytes(tm)),
    )(of, ob, x, state, wp, ps, *gb)


def _pair_bias_kernel(vec_ref, o_ref):
    period = vec_ref.shape[-1]
    toe = pltpu.roll(jnp.broadcast_to(vec_ref[...], (PAIR, period)), 0, 1, stride=1, stride_axis=0)
    r = lax.broadcasted_iota(jnp.int32, (PAIR, PAIR_KEYS), 0)
    c = lax.broadcasted_iota(jnp.int32, (PAIR, PAIR_KEYS), 1)
    off = c - (r // CHUNK) * CHUNK
    in_band = jnp.logical_and(off >= 0, off < BAND_KEYS)
    o_ref[...] = jnp.where(in_band, toe[:, :PAIR_KEYS], NEG_BIG)


def _pair_bias(table):
    lead = table.shape[:-1]
    n = lead[0] * lead[1]
    period = PAIR + PAIR_KEYS
    far = table[..., 2 * REL_CLIP:]
    vec = jnp.concatenate([jnp.broadcast_to(far, lead + (BAND_LEFT - REL_CLIP,)),
                           jnp.flip(table[..., 1:], axis=-1),
                           jnp.broadcast_to(far, lead + (period - BAND_LEFT - REL_CLIP,))], axis=-1)
    out = pl.pallas_call(
        _pair_bias_kernel,
        name="pair_bias",
        grid=(n,),
        in_specs=[pl.BlockSpec((None, 1, period), lambda i: (i, 0, 0))],
        out_specs=pl.BlockSpec((None, PAIR, PAIR_KEYS), lambda i: (i, 0, 0)),
        out_shape=jax.ShapeDtypeStruct((n, PAIR, PAIR_KEYS), F32),
        compiler_params=_params(("arbitrary",), 4 * PAIR * period * 4),
    )(vec.reshape(n, 1, period).astype(F32))
    return out.reshape(lead + (PAIR, PAIR_KEYS))


def _tail_rows(x, n_tail):
    return jnp.stack([x[(bi + 1) * SEQ - n_tail:(bi + 1) * SEQ] for bi in range(BATCH)])


def kernel(x_prompt, x_sample, cache_attn_k, cache_attn_v, state_pool, cache_mem_k, cache_mem_v,
           mem_prompt, w_qkv, w_attn_o, rel_bias, w_pool, pool_scale, w_mem_q, w_mem_kv, w_mem_o,
           w_ffn_in, w_ffn_out, ln_g, ln_b):
    D = D_MODEL
    assert cache_attn_k.shape[2] == BAND_LEFT and DEPTH % 2 == 0
    xp = x_prompt.reshape(M_PROMPT, D)
    xs = x_sample.reshape(M_SAMPLE, D)
    x_f = None
    x_b = _ingest(xp, xs)
    mem_b = mem_prompt.reshape(BATCH * N_MEM, D).astype(BF16)
    w_pool_b = w_pool.astype(BF16)
    gb = (ln_g.reshape(DEPTH * 3, 1, D), ln_b.reshape(DEPTH * 3, 1, D))
    pscale = pool_scale.reshape(-1, 1, D)
    cache_ak = cache_attn_k.reshape(-1, HEAD_DIM)
    cache_av = cache_attn_v.reshape(-1, HEAD_DIM)
    cache_mk = cache_mem_k.reshape(-1, D)
    cache_mv = cache_mem_v.reshape(-1, D)
    state_pad = jnp.pad(state_pool, ((0, 0), (0, 0), (HALO - POOL_HIST, 0), (0, 0))).reshape(-1, HALO, D)
    bias = _pair_bias(rel_bias)
    mem_k, mem_v, mem_kt = _mem_kv_all(mem_b, w_mem_kv)

    ak_p, av_p, pool_p, ak_s, av_s, pool_s = [], [], [], [], [], []
    for i in range(DEPTH):
        if i % 2 == 0:
            a = i // 2
            q, wo_b = _matmul(x_b, w_qkv, a, 0, D, TM_MM, (BF16,), scale=HEAD_DIM ** -0.5, cast=(w_attn_o, a))
            k, kt = _matmul(x_b, w_qkv, a, D, D, TM_MM, (F32,), transposed=True)
            v, v_b = _matmul(x_b, w_qkv, a, 2 * D, D, TM_MM, (F32, BF16))
            o = _band_attn_prompt(q, kt, v_b, bias, a)
            o = _band_attn_sample(o, q, k, v_b, cache_ak, cache_av, bias, a)
            ak_p.append(_tail_rows(k, BAND_LEFT).reshape(BATCH, BAND_LEFT, N_HEADS, HEAD_DIM))
            av_p.append(_tail_rows(v, BAND_LEFT).reshape(BATCH, BAND_LEFT, N_HEADS, HEAD_DIM))
            ak_s.append(k[M_PROMPT:].reshape(DEC_BATCH, DEC_SEQ, N_HEADS, HEAD_DIM))
            av_s.append(v[M_PROMPT:].reshape(DEC_BATCH, DEC_SEQ, N_HEADS, HEAD_DIM))
            if i == 0:
                outs = _mm_ln(o, wo_b, None, xp, gb, 0, tm=TM_LN_PROMPT, n_tiles=M_PROMPT // TM_LN_PROMPT,
                              a_blk0=0, x_blk0=0, o_blk0=0, out_rows=M_ALL, out_dtypes=(F32, BF16))
                x_f, x_b = _mm_ln(o, wo_b, None, xs, gb, 0, tm=M_SAMPLE, n_tiles=1,
                                  a_blk0=M_PROMPT // M_SAMPLE, x_blk0=0, o_blk0=M_PROMPT // M_SAMPLE,
                                  out_rows=M_ALL, out_dtypes=(F32, BF16), into=outs)
            else:
                x_f, x_b = _mm_ln_all(o, wo_b, None, x_f, gb, 3 * i)
        else:
            p = i // 2
            pool_p.append(_tail_rows(x_f, POOL_HIST))
            ext = jnp.concatenate([state_pool[p], x_f[M_PROMPT:].reshape(DEC_BATCH, DEC_SEQ, D)], axis=1)
            pool_s.append(ext[:, DEC_SEQ:])
            y_f, y_b = _pool_prompt(x_f, w_pool_b, pscale, p, gb, 3 * i)
            x_f, x_b = _pool_sample(y_f, y_b, x_f, state_pad, w_pool_b, pscale, p, gb, 3 * i)
        q, wmo_b = _matmul(x_b, w_mem_q, i, 0, D, TM_MM, (BF16,), scale=MEM_HEAD_DIM ** -0.5, cast=(w_mem_o, i))
        o = _mem_attn_prompt(q, mem_kt, mem_v, i)
        o = _mem_attn_sample(o, q, cache_mk, cache_mv, i)
        x_f, x_b = _mm_ln_all(o, wmo_b, None, x_f, gb, 3 * i + 1)
        h, wfo_b = _swiglu_in(x_b, w_ffn_in, w_ffn_out, i)
        if i % 2 == 0:
            x_f, = _mm_ln_all(h, wfo_b, None, x_f, gb, 3 * i + 2, out_dtypes=(F32,))
            x_b = None
        elif i < DEPTH - 1:
            x_f, x_b = _mm_ln_all(h, wfo_b, None, x_f, gb, 3 * i + 2)
        else:
            y_p, = _mm_ln(h, wfo_b, None, x_f, gb, 3 * i + 2, tm=TM_LN_WIDE,
                          n_tiles=M_PROMPT // TM_LN_WIDE, a_blk0=0, x_blk0=0, o_blk0=0,
                          out_rows=M_PROMPT, out_dtypes=(F32,))
            y_s, = _mm_ln(h, wfo_b, None, x_f, gb, 3 * i + 2, tm=M_SAMPLE, n_tiles=1,
                          a_blk0=M_PROMPT // M_SAMPLE, x_blk0=M_PROMPT // M_SAMPLE, o_blk0=0,
                          out_rows=M_SAMPLE, out_dtypes=(F32,))

    mem_shape = (DEPTH, BATCH, N_MEM, MEM_HEADS, MEM_HEAD_DIM)
    return (y_p.reshape(BATCH, SEQ, D), y_s.reshape(DEC_BATCH, DEC_SEQ, D),
            jnp.stack(ak_p), jnp.stack(av_p), jnp.stack(pool_p),
            mem_k.reshape(mem_shape), mem_v.reshape(mem_shape),
            jnp.stack(ak_s), jnp.stack(av_s), jnp.stack(pool_s))
```

```python
import functools

import jax
import jax.numpy as jnp
from jax import lax
from jax.experimental import pallas as pl
from jax.experimental.pallas import tpu as pltpu

F32 = jnp.float32
BF16 = jnp.bfloat16

D_MODEL = 2048
BATCH = 4
SEQ = 4096
DEPTH = 4
DEC_BATCH = 8
DEC_SEQ = 32
CHUNK = 64
BAND_LEFT = 512
BAND_KEYS = BAND_LEFT + CHUNK
N_HEADS = 16
HEAD_DIM = D_MODEL // N_HEADS
REL_CLIP = 2 * CHUNK
POOL_WINDOWS = (2, 4, 8, 16)
POOL_GROUP = D_MODEL // len(POOL_WINDOWS)
POOL_HIST = max(POOL_WINDOWS) - 1
N_MEM = 256
MEM_HEADS = 4
MEM_HEAD_DIM = D_MODEL // MEM_HEADS
D_FF = 5632
ALPHA = (2.0 * DEPTH) ** 0.25
LN_EPS = 1e-5
NEG_BIG = -1e30

M_PROMPT = BATCH * SEQ
M_SAMPLE = DEC_BATCH * DEC_SEQ
M_ALL = M_PROMPT + M_SAMPLE

TM_INGEST = 1024
TM_MM = 1280
TN_MM = 1024
TN_FF = 512
FF_SUB = 2
TM_LN = 640
TM_LN_PROMPT = 512
TM_LN_WIDE = 256
LN_SLAB = 128
LN_PIECES = 2
QB = 512
PAIR = 4 * CHUNK
PAIR_KEYS = BAND_LEFT + PAIR
TM_MEM = 1024
MEM_SLAB = 256
TM_POOL = 512
HALO = 16

V7X_VMEM_BYTES = 64 * 1024 * 1024


def _vmem_limit(nbytes):
    return int(min(nbytes + (12 << 20), V7X_VMEM_BYTES - (6 << 20)))


def _params(sem, nbytes):
    return pltpu.CompilerParams(dimension_semantics=sem, vmem_limit_bytes=_vmem_limit(nbytes))


def _ingest_kernel(*refs):
    x_ref, o_ref = refs[-2:]
    o_ref[...] = x_ref[...].astype(BF16)


def _ingest(xp, xs):
    def call(x, tm, o_blk0, into):
        n_alias = len(into)
        return pl.pallas_call(
            _ingest_kernel,
            name="ingest",
            grid=(x.shape[0] // tm,),
            in_specs=[pl.BlockSpec(memory_space=pl.ANY)] * n_alias
                     + [pl.BlockSpec((tm, D_MODEL), lambda i: (i, 0))],
            out_specs=pl.BlockSpec((tm, D_MODEL), lambda i: (o_blk0 + i, 0)),
            out_shape=jax.ShapeDtypeStruct((M_ALL, D_MODEL), BF16),
            input_output_aliases={j: j for j in range(n_alias)},
            compiler_params=_params(("arbitrary",), 2 * tm * D_MODEL * (4 + 2)),
        )(*into, x)

    out = call(xp, TM_INGEST, 0, ())
    return call(xs, M_SAMPLE, M_PROMPT // M_SAMPLE, (out,))


def _cast_rider(cast, n_inner, n_steps):
    if cast is None:
        return [], [], [], 0
    w, layer = cast
    _, k, d = w.shape
    rows = next(r for r in range(16, k + 1, 16) if k % r == 0 and k // r <= n_steps)
    last = k // rows - 1
    chunk = lambda j, i: jnp.minimum(j * n_inner + i, last)
    return ([pl.BlockSpec((None, rows, d), lambda j, i: (layer, chunk(j, i), 0))],
            [pl.BlockSpec((rows, d), lambda j, i: (chunk(j, i), 0))],
            [jax.ShapeDtypeStruct((k, d), BF16)], 2 * rows * d * 6)


def _mm_kernel(*refs, scale, n_out, rider, transposed):
    n_in = 3 if rider else 2
    a_ref, w_ref = refs[:2]
    o_refs = refs[n_in:n_in + n_out]
    wb_ref = refs[-1]
    if rider:
        refs[n_in + n_out + transposed][...] = refs[2][...].astype(BF16)

    @pl.when(pl.program_id(1) == 0)
    def _():
        wb_ref[...] = w_ref[...].astype(BF16)

    acc = jnp.dot(a_ref[...], wb_ref[...], preferred_element_type=F32)
    if scale != 1.0:
        acc = acc * scale
    for o_ref in o_refs:
        o_ref[...] = acc.astype(o_ref.dtype)
    if transposed:
        refs[n_in + n_out][...] = acc.T.astype(BF16)


def _matmul(a, w, layer, col0, n_out, tm, out_dtypes, scale=1.0, cast=None, transposed=False):
    m, k = a.shape
    tn = TN_MM
    assert m % tm == 0 and n_out % tn == 0 and col0 % tn == 0
    cb = col0 // tn
    grid = (n_out // tn, m // tm)
    r_in, r_out, r_shape, r_bytes = _cast_rider(cast, grid[1], grid[0] * grid[1])
    osz = sum(jnp.dtype(d).itemsize for d in out_dtypes)
    t_out = [pl.BlockSpec((tn, tm), lambda j, i: (j, i))] if transposed else []
    t_shape = [jax.ShapeDtypeStruct((n_out, m), BF16)] if transposed else []
    nbytes = (2 * tm * k * 2 + 2 * k * tn * 4 + k * tn * 2 + 2 * tm * tn * osz + tm * tn * 4 + r_bytes
              + transposed * tm * tn * (2 * 2 + 4))
    return pl.pallas_call(
        functools.partial(_mm_kernel, scale=scale, n_out=len(out_dtypes), rider=cast is not None,
                          transposed=transposed),
        name="mm",
        grid=grid,
        in_specs=[pl.BlockSpec((tm, k), lambda j, i: (i, 0)),
                  pl.BlockSpec((None, k, tn), lambda j, i: (layer, 0, j + cb))] + r_in,
        out_specs=[pl.BlockSpec((tm, tn), lambda j, i: (i, j)) for _ in out_dtypes] + t_out + r_out,
        out_shape=[jax.ShapeDtypeStruct((m, n_out), d) for d in out_dtypes] + t_shape + r_shape,
        scratch_shapes=[pltpu.VMEM((k, tn), BF16)],
        compiler_params=_params(("arbitrary", "arbitrary"), nbytes),
    )(a, w, *(cast[:1] if cast else ()))


def _mem_kv_kernel(a_ref, w_ref, k_ref, v_ref, kt_ref, *, n_key_steps):
    acc = jnp.dot(a_ref[...], w_ref[...].astype(BF16), preferred_element_type=F32)
    is_key = pl.program_id(1) < n_key_steps

    @pl.when(is_key)
    def _():
        k_ref[...] = acc
        kt_ref[...] = acc.T.astype(BF16)

    @pl.when(jnp.logical_not(is_key))
    def _():
        v_ref[...] = acc


def _mem_kv_all(mem, w):
    m, k = mem.shape
    n = w.shape[2] // 2
    tn = TN_MM
    nk = n // tn
    key_tile = lambda j: jnp.minimum(j, nk - 1)
    nbytes = 2 * m * k * 2 + 2 * k * tn * 4 + k * tn * 2 + 4 * m * tn * 4 + 2 * m * tn * 4 + 2 * m * tn * 2
    return pl.pallas_call(
        functools.partial(_mem_kv_kernel, n_key_steps=nk),
        name="mem_kv",
        grid=(w.shape[0], 2 * nk),
        in_specs=[pl.BlockSpec((m, k), lambda l, j: (0, 0)),
                  pl.BlockSpec((None, k, tn), lambda l, j: (l, 0, j))],
        out_specs=[pl.BlockSpec((None, m, tn), lambda l, j: (l, 0, key_tile(j))),
                   pl.BlockSpec((None, m, tn), lambda l, j: (l, 0, jnp.maximum(j - nk, 0))),
                   pl.BlockSpec((None, tn, m), lambda l, j: (l, key_tile(j), 0))],
        out_shape=[jax.ShapeDtypeStruct((w.shape[0], m, n), F32),
                   jax.ShapeDtypeStruct((w.shape[0], m, n), F32),
                   jax.ShapeDtypeStruct((w.shape[0], n, m), BF16)],
        compiler_params=_params(("arbitrary", "arbitrary"), nbytes),
    )(mem, w)


def _swiglu_kernel(a_ref, wg_ref, wu_ref, o_ref, wgb_ref, wub_ref):
    @pl.when(pl.program_id(1) == 0)
    def _():
        wgb_ref[...] = wg_ref[...].astype(BF16)
        wub_ref[...] = wu_ref[...].astype(BF16)

    ts = a_ref.shape[0] // FF_SUB

    def project(r):
        a = a_ref[r * ts:(r + 1) * ts, :]
        return (jnp.dot(a, wgb_ref[...], preferred_element_type=F32),
                jnp.dot(a, wub_ref[...], preferred_element_type=F32))

    def finish(r, gate, up):
        o_ref[r * ts:(r + 1) * ts, :] = (gate / (1.0 + jnp.exp(-gate)) * up).astype(o_ref.dtype)

    pending = None
    for r in range(FF_SUB):
        gate_up = project(r)
        if pending is not None:
            finish(*pending)
        pending = (r, *gate_up)
    finish(*pending)


def _swiglu_in(a, w_in, layer):
    m, k = a.shape
    tm, tn = TM_MM, TN_FF
    nj = D_FF // tn
    nbytes = 2 * tm * k * 2 + 4 * k * tn * 4 + 2 * k * tn * 2 + 2 * tm * tn * 2 + 3 * tm * tn * 4
    return pl.pallas_call(
        _swiglu_kernel,
        name="swiglu_in",
        grid=(nj, m // tm),
        in_specs=[pl.BlockSpec((tm, k), lambda j, i: (i, 0)),
                  pl.BlockSpec((None, k, tn), lambda j, i: (layer, 0, j)),
                  pl.BlockSpec((None, k, tn), lambda j, i: (layer, 0, j + nj))],
        out_specs=pl.BlockSpec((tm, tn), lambda j, i: (i, j)),
        out_shape=jax.ShapeDtypeStruct((m, D_FF), BF16),
        scratch_shapes=[pltpu.VMEM((k, tn), BF16), pltpu.VMEM((k, tn), BF16)],
        compiler_params=_params(("arbitrary", "arbitrary"), nbytes),
    )(a, w_in, w_in)


def _layer_norm(z, g, b):
    mu = jnp.mean(z, axis=-1, keepdims=True)
    zc = z - mu
    var = jnp.mean(zc * zc, axis=-1, keepdims=True)
    return zc * lax.rsqrt(var + LN_EPS) * g + b


def _mm_ln_kernel(*refs, n_alias, n_sub):
    a_ref, w_ref, x_ref, g_ref, b_ref = refs[n_alias:n_alias + 5]
    o_refs = refs[n_alias + 5:-1]
    z_ref = refs[-1]
    ts = z_ref.shape[1]
    tc = D_MODEL // LN_PIECES
    tr = ts // LN_PIECES

    def pre_norm(r, c):
        rs, cs = slice(r * ts, (r + 1) * ts), slice(c * tc, (c + 1) * tc)
        z_ref[r, :, cs] = ALPHA * x_ref[rs, cs] + jnp.dot(a_ref[rs, :], w_ref[:, cs],
                                                         preferred_element_type=F32)

    def finish(r, c):
        y = _layer_norm(z_ref[r, c * tr:(c + 1) * tr, :], g_ref[...], b_ref[...])
        for o_ref in o_refs:
            o_ref[r * ts + c * tr:r * ts + (c + 1) * tr, :] = y.astype(o_ref.dtype)

    for r in range(n_sub + 1):
        for c in range(LN_PIECES):
            if r < n_sub:
                pre_norm(r, c)
            if r > 0:
                finish(r - 1, c)


def _ln_param_specs(gi):
    return [pl.BlockSpec((None, 1, D_MODEL), lambda *_: (gi, 0, 0)),
            pl.BlockSpec((None, 1, D_MODEL), lambda *_: (gi, 0, 0))]


def _mm_ln(a, w, layer, x, gb, gi, *, tm, n_tiles, a_blk0, x_blk0, o_blk0, out_rows, out_dtypes, into=None):
    kdim = a.shape[1]
    assert tm % LN_SLAB == 0
    n_sub = tm // LN_SLAB
    n_alias = 0 if into is None else len(into)
    if layer is None:
        w_spec = pl.BlockSpec((kdim, D_MODEL), lambda i: (0, 0), pipeline_mode=pl.Buffered(1))
    else:
        w_spec = pl.BlockSpec((None, kdim, D_MODEL), lambda i: (layer, 0, 0), pipeline_mode=pl.Buffered(1))
    osz = sum(jnp.dtype(d).itemsize for d in out_dtypes)
    nbytes = (2 * tm * kdim * 2 + kdim * D_MODEL * 2 + 2 * tm * D_MODEL * 4 + 2 * tm * D_MODEL * osz
              + tm * D_MODEL * 4 + 2 * LN_SLAB * D_MODEL * 4)
    return pl.pallas_call(
        functools.partial(_mm_ln_kernel, n_alias=n_alias, n_sub=n_sub),
        name="mm_ln",
        grid=(n_tiles,),
        in_specs=[pl.BlockSpec(memory_space=pl.ANY)] * n_alias + [
            pl.BlockSpec((tm, kdim), lambda i: (a_blk0 + i, 0)),
            w_spec,
            pl.BlockSpec((tm, D_MODEL), lambda i: (x_blk0 + i, 0))] + _ln_param_specs(gi),
        out_specs=[pl.BlockSpec((tm, D_MODEL), lambda i: (o_blk0 + i, 0)) for _ in out_dtypes],
        out_shape=[jax.ShapeDtypeStruct((out_rows, D_MODEL), d) for d in out_dtypes],
        input_output_aliases={j: j for j in range(n_alias)},
        scratch_shapes=[pltpu.VMEM((n_sub, LN_SLAB, D_MODEL), F32)],
        compiler_params=_params(("arbitrary",), nbytes),
    )(*(into or ()), a, w, x, *gb)


def _mm_ln_all(a, w, layer, x, gb, gi, out_dtypes=(F32, BF16)):
    tm = TM_LN if a.shape[1] <= D_MODEL else TM_LN_WIDE
    return _mm_ln(a, w, layer, x, gb, gi, tm=tm, n_tiles=M_ALL // tm, a_blk0=0, x_blk0=0, o_blk0=0,
                  out_rows=M_ALL, out_dtypes=out_dtypes)


def _dot_nt(a, b):
    return lax.dot_general(a, b, (((1,), (1,)), ((), ())), preferred_element_type=F32)


def _band_attn_kernel(q_ref, ktl_ref, ktc_ref, vl_ref, vc_ref, bias_ref, o_ref, kt_sc, v_sc):
    kt_sc[:, 0:QB] = ktl_ref[...]
    kt_sc[:, QB:2 * QB] = ktc_ref[...]
    v_sc[0:QB, :] = vl_ref[...]
    v_sc[QB:2 * QB, :] = vc_ref[...]
    no_left = pl.program_id(1) == 0
    col = lax.broadcasted_iota(jnp.int32, (PAIR, PAIR_KEYS), 1)

    def scores(h, r0):
        sl = slice(h * HEAD_DIM, (h + 1) * HEAD_DIM)
        s = jnp.dot(q_ref[r0:r0 + PAIR, sl], kt_sc[sl, r0:r0 + PAIR_KEYS],
                    preferred_element_type=F32) + bias_ref[h]
        return jnp.where(jnp.logical_and(no_left, col < QB - r0), NEG_BIG, s)

    def weights(s):
        m = jnp.max(s, axis=-1, keepdims=True)
        p = jnp.exp(s - m)
        return p.astype(BF16), jnp.sum(p, axis=-1, keepdims=True)

    def finish(h, r0, p, denom):
        sl = slice(h * HEAD_DIM, (h + 1) * HEAD_DIM)
        o = jnp.dot(p, v_sc[r0:r0 + PAIR_KEYS, sl], preferred_element_type=F32)
        o_ref[r0:r0 + PAIR, sl] = (o / denom).astype(o_ref.dtype)

    units = [(h, pr * PAIR) for h in range(N_HEADS) for pr in range(QB // PAIR)]
    scored, weighted = None, None
    for unit in units + [None, None]:
        s = scores(*unit) if unit is not None else None
        nxt = (*scored[0], *weights(scored[1])) if scored is not None else None
        if weighted is not None:
            finish(*weighted)
        scored = (unit, s) if unit is not None else None
        weighted = nxt


def _band_attn_prompt(q, kt, v, bias, a):
    nb = SEQ // QB
    cur = lambda b, i: (b * nb + i, 0)
    left = lambda b, i: (b * nb + jnp.maximum(i - 1, 0), 0)
    cur_t = lambda b, i: (0, b * nb + i)
    left_t = lambda b, i: (0, b * nb + jnp.maximum(i - 1, 0))
    nbytes = (2 * 6 * QB * D_MODEL * 2 + N_HEADS * PAIR * PAIR_KEYS * 4 + 4 * QB * D_MODEL * 2
              + 8 * PAIR * PAIR_KEYS * 4)
    return pl.pallas_call(
        _band_attn_kernel,
        name="band_attn_prompt",
        grid=(BATCH, nb),
        in_specs=[pl.BlockSpec((QB, D_MODEL), cur),
                  pl.BlockSpec((D_MODEL, QB), left_t),
                  pl.BlockSpec((D_MODEL, QB), cur_t),
                  pl.BlockSpec((QB, D_MODEL), left),
                  pl.BlockSpec((QB, D_MODEL), cur),
                  pl.BlockSpec((None, N_HEADS, PAIR, PAIR_KEYS), lambda b, i: (a, 0, 0, 0),
                               pipeline_mode=pl.Buffered(1))],
        out_specs=pl.BlockSpec((QB, D_MODEL), cur),
        out_shape=jax.ShapeDtypeStruct((M_ALL, D_MODEL), BF16),
        scratch_shapes=[pltpu.VMEM((D_MODEL, 2 * QB), BF16), pltpu.VMEM((2 * QB, D_MODEL), BF16)],
        compiler_params=_params(("arbitrary", "arbitrary"), nbytes),
    )(q, kt, kt, v, v, bias)


def _band_attn_sample_kernel(o_any, q_ref, kn_ref, vn_ref, ck_ref, cv_ref, bias_ref, o_ref):
    del o_any
    past = ck_ref.shape[0] // N_HEADS
    for h in range(N_HEADS):
        sl = slice(h * HEAD_DIM, (h + 1) * HEAD_DIM)
        rows = pl.ds(h, past, stride=N_HEADS)
        q = q_ref[:, sl]
        s_c = _dot_nt(q, ck_ref[rows, :].astype(BF16)) + bias_ref[h, 0:DEC_SEQ, 0:past]
        s_n = _dot_nt(q, kn_ref[:, sl].astype(BF16)) + bias_ref[h, 0:DEC_SEQ, past:past + DEC_SEQ]
        m = jnp.maximum(jnp.max(s_c, axis=-1, keepdims=True), jnp.max(s_n, axis=-1, keepdims=True))
        p_c = jnp.exp(s_c - m)
        p_n = jnp.exp(s_n - m)
        denom = jnp.sum(p_c, axis=-1, keepdims=True) + jnp.sum(p_n, axis=-1, keepdims=True)
        o = (jnp.dot(p_c.astype(BF16), cv_ref[rows, :].astype(BF16), preferred_element_type=F32)
             + jnp.dot(p_n.astype(BF16), vn_ref[:, sl], preferred_element_type=F32))
        o_ref[:, sl] = (o / denom).astype(o_ref.dtype)


def _band_attn_sample(o, q, k, v, cache_k, cache_v, bias, a):
    t = DEC_SEQ
    rb = M_PROMPT // t
    cr = BAND_LEFT * N_HEADS
    nbytes = (2 * (5 * t * D_MODEL * 2 + 2 * cr * HEAD_DIM * 4) + N_HEADS * PAIR * PAIR_KEYS * 4
              + 4 * BAND_LEFT * HEAD_DIM * 4)
    return pl.pallas_call(
        _band_attn_sample_kernel,
        name="band_attn_sample",
        grid=(DEC_BATCH,),
        in_specs=[pl.BlockSpec(memory_space=pl.ANY),
                  pl.BlockSpec((t, D_MODEL), lambda b: (rb + b, 0)),
                  pl.BlockSpec((t, D_MODEL), lambda b: (rb + b, 0)),
                  pl.BlockSpec((t, D_MODEL), lambda b: (rb + b, 0)),
                  pl.BlockSpec((cr, HEAD_DIM), lambda b: (a * DEC_BATCH + b, 0)),
                  pl.BlockSpec((cr, HEAD_DIM), lambda b: (a * DEC_BATCH + b, 0)),
                  pl.BlockSpec((None, N_HEADS, PAIR, PAIR_KEYS), lambda b: (a, 0, 0, 0),
                               pipeline_mode=pl.Buffered(1))],
        out_specs=pl.BlockSpec((t, D_MODEL), lambda b: (rb + b, 0)),
        out_shape=jax.ShapeDtypeStruct(o.shape, o.dtype),
        input_output_aliases={0: 0},
        compiler_params=_params(("arbitrary",), nbytes),
    )(o, q, k, v, cache_k, cache_v, bias)


def _mem_attn_body(q_ref, k_ref, v_ref, o_ref, keys_transposed):
    ts = min(q_ref.shape[0], MEM_SLAB)
    head = lambda h: slice(h * MEM_HEAD_DIM, (h + 1) * MEM_HEAD_DIM)
    vb = [v_ref[:, head(h)].astype(BF16) for h in range(MEM_HEADS)]
    if not keys_transposed:
        kb = [k_ref[:, head(h)].astype(BF16) for h in range(MEM_HEADS)]

    def scores(h, r0):
        q = q_ref[r0:r0 + ts, head(h)]
        if keys_transposed:
            return jnp.dot(q, k_ref[head(h), :], preferred_element_type=F32)
        return _dot_nt(q, kb[h])

    def finish(h, r0, s):
        m = jnp.max(s, axis=-1, keepdims=True)
        p = jnp.exp(s - m)
        denom = jnp.sum(p, axis=-1, keepdims=True)
        o = jnp.dot(p.astype(BF16), vb[h], preferred_element_type=F32)
        o_ref[r0:r0 + ts, head(h)] = (o / denom).astype(o_ref.dtype)

    units = [(h, r0) for h in range(MEM_HEADS) for r0 in range(0, q_ref.shape[0], ts)]
    pending = None
    for h, r0 in units:
        s = scores(h, r0)
        if pending is not None:
            finish(*pending)
        pending = (h, r0, s)
    finish(*pending)


def _mem_attn_prompt_kernel(q_ref, kt_ref, v_ref, wc_ref, o_ref, wcb_ref):
    wcb_ref[...] = wc_ref[...].astype(BF16)
    _mem_attn_body(q_ref, kt_ref, v_ref, o_ref, True)


def _mem_attn_sample_kernel(o_any, q_ref, k_ref, v_ref, o_ref):
    del o_any
    _mem_attn_body(q_ref, k_ref, v_ref, o_ref, False)


def _mem_attn_prompt(q, mem_kt, mem_v, layer, cast):
    tm = TM_MEM
    nt = SEQ // tm
    r_in, r_out, r_shape, r_bytes = _cast_rider(cast, nt, BATCH * nt)
    nbytes = (2 * (tm * D_MODEL * 2 + N_MEM * D_MODEL * 6 + tm * D_MODEL * 2) + 6 * tm * N_MEM * 4
              + r_bytes)
    return pl.pallas_call(
        _mem_attn_prompt_kernel,
        name="mem_attn_prompt",
        grid=(BATCH, nt),
        in_specs=[pl.BlockSpec((tm, D_MODEL), lambda b, i: (b * nt + i, 0)),
                  pl.BlockSpec((None, D_MODEL, N_MEM), lambda b, i: (layer, 0, b)),
                  pl.BlockSpec((None, N_MEM, D_MODEL), lambda b, i: (layer, b, 0))] + r_in,
        out_specs=[pl.BlockSpec((tm, D_MODEL), lambda b, i: (b * nt + i, 0))] + r_out,
        out_shape=[jax.ShapeDtypeStruct((M_ALL, D_MODEL), BF16)] + r_shape,
        compiler_params=_params(("arbitrary", "arbitrary"), nbytes),
    )(q, mem_kt, mem_v, cast[0])


def _mem_attn_sample(o, q, cache_k, cache_v, layer):
    t = DEC_SEQ
    rb = M_PROMPT // t
    nbytes = 2 * (2 * t * D_MODEL * 2 + 2 * N_MEM * D_MODEL * 4) + 6 * t * N_MEM * 4
    return pl.pallas_call(
        _mem_attn_sample_kernel,
        name="mem_attn_sample",
        grid=(DEC_BATCH,),
        in_specs=[pl.BlockSpec(memory_space=pl.ANY),
                  pl.BlockSpec((t, D_MODEL), lambda b: (rb + b, 0)),
                  pl.BlockSpec((N_MEM, D_MODEL), lambda b: (layer * DEC_BATCH + b, 0)),
                  pl.BlockSpec((N_MEM, D_MODEL), lambda b: (layer * DEC_BATCH + b, 0))],
        out_specs=pl.BlockSpec((t, D_MODEL), lambda b: (rb + b, 0)),
        out_shape=jax.ShapeDtypeStruct(o.shape, o.dtype),
        input_output_aliases={0: 0},
        compiler_params=_params(("arbitrary",), nbytes),
    )(o, q, cache_k, cache_v)


def _pool_body(x_ref, halo, pos0, n_hist, wp_ref, ps_ref, g_ref, b_ref, of_ref, ob_ref,
               ext_ref, sa_ref, sb_ref):
    tm = x_ref.shape[0]
    top = 2 * HALO
    rows = tm + top
    zeros = jnp.zeros((HALO, D_MODEL), F32)
    ext_ref[0:HALO, :] = zeros
    ext_ref[HALO:top, :] = halo
    ext_ref[top:rows, :] = x_ref[...]
    sa_ref[0:HALO, :] = zeros[:, :POOL_GROUP]
    sb_ref[0:HALO, :] = zeros[:, :POOL_GROUP]
    seen = (pos0 + n_hist + 1 + lax.broadcasted_iota(jnp.int32, (tm, 1), 0)).astype(F32)
    bufs = (sa_ref, sb_ref)
    for gi, win in enumerate(POOL_WINDOWS):
        cs = slice(gi * POOL_GROUP, (gi + 1) * POOL_GROUP)
        src, d, lvl = None, 1, 0
        while 2 * d < win:
            if src is None:
                val = ext_ref[HALO:rows, cs] + ext_ref[HALO - d:rows - d, cs]
            else:
                val = src[HALO:rows, :] + src[HALO - d:rows - d, :]
            dst = bufs[lvl % 2]
            dst[HALO:rows, :] = val
            src, d, lvl = dst, 2 * d, lvl + 1
        if src is None:
            wsum = ext_ref[top:rows, cs] + ext_ref[top - d:rows - d, cs]
        else:
            wsum = src[top:rows, :] + src[top - d:rows - d, :]
        xg = x_ref[:, cs]
        u = wsum / jnp.minimum(seen, float(win)) - xg
        y = jnp.dot(u.astype(BF16), wp_ref[gi], preferred_element_type=F32) * ps_ref[:, cs]
        of_ref[:, cs] = ALPHA * xg + y
    out = _layer_norm(of_ref[...], g_ref[...], b_ref[...])
    of_ref[...] = out
    ob_ref[...] = out.astype(BF16)


def _pool_prompt_kernel(x_ref, halo_ref, wp_ref, ps_ref, g_ref, b_ref, of_ref, ob_ref,
                        ext_ref, sa_ref, sb_ref):
    it = pl.program_id(0) % (SEQ // TM_POOL)
    halo = jnp.where(it > 0, halo_ref[...], 0.0)
    _pool_body(x_ref, halo, it * TM_POOL, 0, wp_ref, ps_ref, g_ref, b_ref, of_ref, ob_ref,
               ext_ref, sa_ref, sb_ref)


def _pool_sample_kernel(of_any, ob_any, x_ref, halo_ref, wp_ref, ps_ref, g_ref, b_ref, of_ref, ob_ref,
                        ext_ref, sa_ref, sb_ref):
    del of_any, ob_any
    _pool_body(x_ref, halo_ref[...], 0, POOL_HIST, wp_ref, ps_ref, g_ref, b_ref, of_ref, ob_ref,
               ext_ref, sa_ref, sb_ref)


def _pool_scratch(tm):
    return [pltpu.VMEM((tm + 2 * HALO, D_MODEL), F32),
            pltpu.VMEM((tm + 2 * HALO, POOL_GROUP), F32),
            pltpu.VMEM((tm + 2 * HALO, POOL_GROUP), F32)]


def _pool_const_specs(p, gi):
    return [pl.BlockSpec((None, len(POOL_WINDOWS), POOL_GROUP, POOL_GROUP), lambda i: (p, 0, 0, 0)),
            pl.BlockSpec((None, 1, D_MODEL), lambda i: (p, 0, 0))] + _ln_param_specs(gi)


def _pool_bytes(tm):
    return (2 * tm * D_MODEL * 4 + 2 * HALO * D_MODEL * 4 + 2 * D_MODEL * POOL_GROUP * 2 + 2 * tm * D_MODEL * 6
            + (tm + 2 * HALO) * (D_MODEL + 2 * POOL_GROUP) * 4 + 2 * tm * D_MODEL * 4)


def _pool_prompt(x, wp, ps, p, gb, gi):
    tm = TM_POOL
    hb = tm // HALO
    return pl.pallas_call(
        _pool_prompt_kernel,
        name="pool_prompt",
        grid=(M_PROMPT // tm,),
        in_specs=[pl.BlockSpec((tm, D_MODEL), lambda i: (i, 0)),
                  pl.BlockSpec((HALO, D_MODEL), lambda i: (jnp.maximum(i * hb - 1, 0), 0))]
                 + _pool_const_specs(p, gi),
        out_specs=[pl.BlockSpec((tm, D_MODEL), lambda i: (i, 0)),
                   pl.BlockSpec((tm, D_MODEL), lambda i: (i, 0))],
        out_shape=[jax.ShapeDtypeStruct((M_ALL, D_MODEL), F32),
                   jax.ShapeDtypeStruct((M_ALL, D_MODEL), BF16)],
        scratch_shapes=_pool_scratch(tm),
        compiler_params=_params(("arbitrary",), _pool_bytes(tm)),
    )(x, x, wp, ps, *gb)


def _pool_sample(of, ob, x, state, wp, ps, p, gb, gi):
    tm = DEC_SEQ
    rb = M_PROMPT // tm
    return pl.pallas_call(
        _pool_sample_kernel,
        name="pool_sample",
        grid=(DEC_BATCH,),
        in_specs=[pl.BlockSpec(memory_space=pl.ANY),
                  pl.BlockSpec(memory_space=pl.ANY),
                  pl.BlockSpec((tm, D_MODEL), lambda i: (rb + i, 0)),
                  pl.BlockSpec((None, HALO, D_MODEL), lambda i: (p * DEC_BATCH + i, 0, 0))]
                 + _pool_const_specs(p, gi),
        out_specs=[pl.BlockSpec((tm, D_MODEL), lambda i: (rb + i, 0)),
                   pl.BlockSpec((tm, D_MODEL), lambda i: (rb + i, 0))],
        out_shape=[jax.ShapeDtypeStruct(of.shape, of.dtype),
                   jax.ShapeDtypeStruct(ob.shape, ob.dtype)],
        input_output_aliases={0: 0, 1: 1},
        scratch_shapes=_pool_scratch(tm),
        compiler_params=_params(("arbitrary",), _pool_bytes(tm)),
    )(of, ob, x, state, wp, ps, *gb)


def _pair_bias_kernel(vec_ref, o_ref):
    period = vec_ref.shape[-1]
    toe = pltpu.roll(jnp.broadcast_to(vec_ref[...], (PAIR, period)), 0, 1, stride=1, stride_axis=0)
    r = lax.broadcasted_iota(jnp.int32, (PAIR, PAIR_KEYS), 0)
    c = lax.broadcasted_iota(jnp.int32, (PAIR, PAIR_KEYS), 1)
    off = c - (r // CHUNK) * CHUNK
    in_band = jnp.logical_and(off >= 0, off < BAND_KEYS)
    o_ref[...] = jnp.where(in_band, toe[:, :PAIR_KEYS], NEG_BIG)


def _pair_bias(table):
    lead = table.shape[:-1]
    n = lead[0] * lead[1]
    period = PAIR + PAIR_KEYS
    far = table[..., 2 * REL_CLIP:]
    vec = jnp.concatenate([jnp.broadcast_to(far, lead + (BAND_LEFT - REL_CLIP,)),
                           jnp.flip(table[..., 1:], axis=-1),
                           jnp.broadcast_to(far, lead + (period - BAND_LEFT - REL_CLIP,))], axis=-1)
    out = pl.pallas_call(
        _pair_bias_kernel,
        name="pair_bias",
        grid=(n,),
        in_specs=[pl.BlockSpec((None, 1, period), lambda i: (i, 0, 0))],
        out_specs=pl.BlockSpec((None, PAIR, PAIR_KEYS), lambda i: (i, 0, 0)),
        out_shape=jax.ShapeDtypeStruct((n, PAIR, PAIR_KEYS), F32),
        compiler_params=_params(("arbitrary",), 4 * PAIR * period * 4),
    )(vec.reshape(n, 1, period).astype(F32))
    return out.reshape(lead + (PAIR, PAIR_KEYS))


def _tail_rows(x, n_tail):
    return jnp.stack([x[(bi + 1) * SEQ - n_tail:(bi + 1) * SEQ] for bi in range(BATCH)])


def kernel(x_prompt, x_sample, cache_attn_k, cache_attn_v, state_pool, cache_mem_k, cache_mem_v,
           mem_prompt, w_qkv, w_attn_o, rel_bias, w_pool, pool_scale, w_mem_q, w_mem_kv, w_mem_o,
           w_ffn_in, w_ffn_out, ln_g, ln_b):
    D = D_MODEL
    assert cache_attn_k.shape[2] == BAND_LEFT and DEPTH % 2 == 0
    xp = x_prompt.reshape(M_PROMPT, D)
    xs = x_sample.reshape(M_SAMPLE, D)
    x_f = None
    x_b = _ingest(xp, xs)
    mem_b = mem_prompt.reshape(BATCH * N_MEM, D).astype(BF16)
    w_pool_b = w_pool.astype(BF16)
    gb = (ln_g.reshape(DEPTH * 3, 1, D), ln_b.reshape(DEPTH * 3, 1, D))
    pscale = pool_scale.reshape(-1, 1, D)
    cache_ak = cache_attn_k.reshape(-1, HEAD_DIM)
    cache_av = cache_attn_v.reshape(-1, HEAD_DIM)
    cache_mk = cache_mem_k.reshape(-1, D)
    cache_mv = cache_mem_v.reshape(-1, D)
    state_pad = jnp.pad(state_pool, ((0, 0), (0, 0), (HALO - POOL_HIST, 0), (0, 0))).reshape(-1, HALO, D)
    bias = _pair_bias(rel_bias)
    mem_k, mem_v, mem_kt = _mem_kv_all(mem_b, w_mem_kv)

    ak_p, av_p, pool_p, ak_s, av_s, pool_s = [], [], [], [], [], []
    for i in range(DEPTH):
        if i % 2 == 0:
            a = i // 2
            q, wo_b = _matmul(x_b, w_qkv, a, 0, D, TM_MM, (BF16,), scale=HEAD_DIM ** -0.5, cast=(w_attn_o, a))
            k, kt = _matmul(x_b, w_qkv, a, D, D, TM_MM, (F32,), transposed=True)
            v, v_b = _matmul(x_b, w_qkv, a, 2 * D, D, TM_MM, (F32, BF16))
            o = _band_attn_prompt(q, kt, v_b, bias, a)
            o = _band_attn_sample(o, q, k, v_b, cache_ak, cache_av, bias, a)
            ak_p.append(_tail_rows(k, BAND_LEFT).reshape(BATCH, BAND_LEFT, N_HEADS, HEAD_DIM))
            av_p.append(_tail_rows(v, BAND_LEFT).reshape(BATCH, BAND_LEFT, N_HEADS, HEAD_DIM))
            ak_s.append(k[M_PROMPT:].reshape(DEC_BATCH, DEC_SEQ, N_HEADS, HEAD_DIM))
            av_s.append(v[M_PROMPT:].reshape(DEC_BATCH, DEC_SEQ, N_HEADS, HEAD_DIM))
            if i == 0:
                outs = _mm_ln(o, wo_b, None, xp, gb, 0, tm=TM_LN_PROMPT, n_tiles=M_PROMPT // TM_LN_PROMPT,
                              a_blk0=0, x_blk0=0, o_blk0=0, out_rows=M_ALL, out_dtypes=(F32, BF16))
                x_f, x_b = _mm_ln(o, wo_b, None, xs, gb, 0, tm=M_SAMPLE, n_tiles=1,
                                  a_blk0=M_PROMPT // M_SAMPLE, x_blk0=0, o_blk0=M_PROMPT // M_SAMPLE,
                                  out_rows=M_ALL, out_dtypes=(F32, BF16), into=outs)
            else:
                x_f, x_b = _mm_ln_all(o, wo_b, None, x_f, gb, 3 * i)
        else:
            p = i // 2
            pool_p.append(_tail_rows(x_f, POOL_HIST))
            ext = jnp.concatenate([state_pool[p], x_f[M_PROMPT:].reshape(DEC_BATCH, DEC_SEQ, D)], axis=1)
            pool_s.append(ext[:, DEC_SEQ:])
            y_f, y_b = _pool_prompt(x_f, w_pool_b, pscale, p, gb, 3 * i)
            x_f, x_b = _pool_sample(y_f, y_b, x_f, state_pad, w_pool_b, pscale, p, gb, 3 * i)
        q, wmo_b = _matmul(x_b, w_mem_q, i, 0, D, TM_MM, (BF16,), scale=MEM_HEAD_DIM ** -0.5, cast=(w_mem_o, i))
        o, wfo_b = _mem_attn_prompt(q, mem_kt, mem_v, i, (w_ffn_out, i))
        o = _mem_attn_sample(o, q, cache_mk, cache_mv, i)
        x_f, x_b = _mm_ln_all(o, wmo_b, None, x_f, gb, 3 * i + 1)
        h = _swiglu_in(x_b, w_ffn_in, i)
        if i % 2 == 0:
            x_f, = _mm_ln_all(h, wfo_b, None, x_f, gb, 3 * i + 2, out_dtypes=(F32,))
            x_b = None
        elif i < DEPTH - 1:
            x_f, x_b = _mm_ln_all(h, wfo_b, None, x_f, gb, 3 * i + 2)
        else:
            y_p, = _mm_ln(h, wfo_b, None, x_f, gb, 3 * i + 2, tm=TM_LN_WIDE,
                          n_tiles=M_PROMPT // TM_LN_WIDE, a_blk0=0, x_blk0=0, o_blk0=0,
                          out_rows=M_PROMPT, out_dtypes=(F32,))
            y_s, = _mm_ln(h, wfo_b, None, x_f, gb, 3 * i + 2, tm=M_SAMPLE, n_tiles=1,
                          a_blk0=M_PROMPT // M_SAMPLE, x_blk0=M_PROMPT // M_SAMPLE, o_blk0=0,
                          out_rows=M_SAMPLE, out_dtypes=(F32,))

    mem_shape = (DEPTH, BATCH, N_MEM, MEM_HEADS, MEM_HEAD_DIM)
    return (y_p.reshape(BATCH, SEQ, D), y_s.reshape(DEC_BATCH, DEC_SEQ, D),
            jnp.stack(ak_p), jnp.stack(av_p), jnp.stack(pool_p),
            mem_k.reshape(mem_shape), mem_v.reshape(mem_shape),
            jnp.stack(ak_s), jnp.stack(av_s), jnp.stack(pool_s))
```
